```python
import math
import jax, jax.numpy as jnp
from jax import lax
import numpy as np

D_MODEL = 2048
BATCH = 1
SEQ = 8192
DEPTH = 1
DEC_BATCH = 2
DEC_SEQ = 4096
PAST_LEN = 128

GRID_W = 64
WIN_H = 8
WIN_W = 16
QBLK_W = 16
KBLK_W = 32
N_QBLK = GRID_W // QBLK_W
HEAD_DIM = 64
NA_HEADS = 16
NA_WIDTH = NA_HEADS * HEAD_DIM
CONV_GROUPS = 8
CONV_WIDTH = CONV_GROUPS * HEAD_DIM
CONV_K = 3
MEM_HEADS = 4
MEM_HEAD_DIM = 128
MEM_WIDTH = MEM_HEADS * MEM_HEAD_DIM
N_MEM = 256
MIX_WIDTH = NA_WIDTH + CONV_WIDTH + MEM_WIDTH
IN_WIDTH = 3 * NA_WIDTH + 3 * CONV_WIDTH + MEM_WIDTH
D_FF = 5632
EPS = 1e-6
NEG_INF = -1e30

kernel_name = "hymba_natten_shortconv_memxattn_encoder"


def rmsnorm(x, g):
    x32 = x.astype(jnp.float32)
    y = x32 * lax.rsqrt(jnp.mean(x32 * x32, axis=-1, keepdims=True) + EPS) * g.astype(jnp.float32)
    return y.astype(x.dtype)


def dwconv3(u, w):
    up = jnp.pad(u, ((0, 0), (1, 1), (0, 0)))
    return up[:, :-2] * w[0] + up[:, 1:-1] * w[1] + up[:, 2:] * w[2]


def neighbourhood_attention(q, k, v, rel_bias):
    b, t, h, dh = q.shape
    rows = t // GRID_W
    wh = min(WIN_H, rows)
    r = jnp.arange(rows)
    row_idx = jnp.clip(r - wh // 2, 0, rows - wh)[:, None] + jnp.arange(wh)[None, :]
    qc = (jnp.arange(N_QBLK) * QBLK_W)[:, None] + jnp.arange(QBLK_W)[None, :]
    q_cs = jnp.clip(qc - WIN_W // 2, 0, GRID_W - WIN_W)
    k_cs = jnp.clip(jnp.arange(N_QBLK) * QBLK_W - WIN_W // 2, 0, GRID_W - KBLK_W)
    col_idx = k_cs[:, None] + jnp.arange(KBLK_W)[None, :]
    n_keys = wh * KBLK_W
    ri = row_idx[:, None, :, None]
    ci = col_idx[None, :, None, :]
    kg = k.reshape(b, rows, GRID_W, h, dh)[:, ri, ci].reshape(b, rows, N_QBLK, n_keys, h, dh)
    vg = v.reshape(b, rows, GRID_W, h, dh)[:, ri, ci].reshape(b, rows, N_QBLK, n_keys, h, dh)
    qg = q.reshape(b, rows, N_QBLK, QBLK_W, h, dh)
    kcol = col_idx[:, None, :]
    valid = (kcol >= q_cs[:, :, None]) & (kcol < q_cs[:, :, None] + WIN_W)
    dr = row_idx - r[:, None] + (WIN_H - 1)
    dc = jnp.clip(kcol - qc[:, :, None], -(WIN_W - 1), WIN_W - 1) + (WIN_W - 1)
    bias = rel_bias.astype(jnp.float32)[:, dr[:, None, None, :, None], dc[None, :, :, None, :]]
    bias = jnp.where(valid[None, None, :, :, None, :], bias, NEG_INF)
    bias = bias.reshape(h, rows, N_QBLK, QBLK_W, n_keys)
    s = jnp.einsum('brjqhd,brjkhd->bhrjqk', qg, kg, preferred_element_type=jnp.float32)
    s = s * (1.0 / math.sqrt(dh)) + bias[None]
    p = jax.nn.softmax(s, axis=-1).astype(v.dtype)
    o = jnp.einsum('bhrjqk,brjkhd->brjqhd', p, vg)
    return o.reshape(b, t, h * dh)


def memory_attention(q, mk, mv):
    b, t, h, dh = q.shape
    s = jnp.einsum('bthd,bmhd->bhtm', q, mk, preferred_element_type=jnp.float32) * (1.0 / math.sqrt(dh))
    p = jax.nn.softmax(s, axis=-1).astype(mv.dtype)
    return jnp.einsum('bhtm,bmhd->bthd', p, mv).reshape(b, t, h * dh)


def encoder_layer(x, mem, g_mix, w_in, na_q_gain, na_k_gain, na_rel_bias, conv_w,
                  mem_norm_g, w_mem_kv, mem_q_gain, mem_k_gain, w_out,
                  g_ffn, w_ffn_in, ffn_conv_w, w_ffn_out):
    b, t, _ = x.shape
    n = rmsnorm(x, g_mix)
    z = n @ w_in
    cuts = [NA_WIDTH, 2 * NA_WIDTH, 3 * NA_WIDTH,
            3 * NA_WIDTH + CONV_WIDTH, 3 * NA_WIDTH + 2 * CONV_WIDTH, 3 * NA_WIDTH + 3 * CONV_WIDTH]
    q_na, k_na, v_na, h_c, b_c, c_c, q_m = jnp.split(z, cuts, axis=-1)
    q_na = rmsnorm(q_na.reshape(b, t, NA_HEADS, HEAD_DIM), na_q_gain)
    k_na = rmsnorm(k_na.reshape(b, t, NA_HEADS, HEAD_DIM), na_k_gain)
    v_na = v_na.reshape(b, t, NA_HEADS, HEAD_DIM)
    y_na = neighbourhood_attention(q_na, k_na, v_na, na_rel_bias)
    y_c = b_c * dwconv3(c_c * h_c, conv_w)
    mkv = rmsnorm(mem, mem_norm_g) @ w_mem_kv
    m = mem.shape[1]
    mk, mv = jnp.split(mkv, 2, axis=-1)
    mk = rmsnorm(mk.reshape(b, m, MEM_HEADS, MEM_HEAD_DIM), mem_k_gain)
    mv = mv.reshape(b, m, MEM_HEADS, MEM_HEAD_DIM)
    q_m = rmsnorm(q_m.reshape(b, t, MEM_HEADS, MEM_HEAD_DIM), mem_q_gain)
    y_m = memory_attention(q_m, mk, mv)
    x = x + jnp.concatenate([y_na, y_c, y_m], axis=-1) @ w_out
    u = dwconv3(rmsnorm(x, g_ffn) @ w_ffn_in, ffn_conv_w)
    a, g = jnp.split(u, 2, axis=-1)
    return x + (jax.nn.silu(a) * g) @ w_ffn_out


def setup_inputs(seed: int = 0) -> dict:
    key = jax.random.key(seed)
    ks = jax.random.split(key, 20)
    f32 = jnp.float32

    def nrm(k, shape, scale):
        return jax.random.normal(k, shape, f32) * scale

    def gain(k, shape):
        return 1.0 + 0.02 * jax.random.normal(k, shape, f32)

    return {
        "x_prompt": nrm(ks[0], (BATCH, SEQ, D_MODEL), 1.0),
        "x_sample": nrm(ks[1], (DEC_BATCH, DEC_SEQ, D_MODEL), 1.0),
        "mem_prompt": nrm(ks[2], (BATCH, N_MEM, D_MODEL), 1.0),
        "mem_sample": nrm(ks[3], (DEC_BATCH, N_MEM, D_MODEL), 1.0),
        "g_mix": gain(ks[4], (DEPTH, D_MODEL)),
        "w_in": nrm(ks[5], (DEPTH, D_MODEL, IN_WIDTH), D_MODEL ** -0.5),
        "na_q_gain": gain(ks[6], (DEPTH, HEAD_DIM)),
        "na_k_gain": gain(ks[7], (DEPTH, HEAD_DIM)),
        "na_rel_bias": nrm(ks[8], (DEPTH, NA_HEADS, 2 * WIN_H - 1, 2 * WIN_W - 1), 0.1),
        "conv_w": nrm(ks[9], (DEPTH, CONV_K, CONV_WIDTH), CONV_K ** -0.5),
        "mem_norm_g": gain(ks[10], (DEPTH, D_MODEL)),
        "w_mem_kv": nrm(ks[11], (DEPTH, D_MODEL, 2 * MEM_WIDTH), D_MODEL ** -0.5),
        "mem_q_gain": gain(ks[12], (DEPTH, MEM_HEAD_DIM)),
        "mem_k_gain": gain(ks[13], (DEPTH, MEM_HEAD_DIM)),
        "w_out": nrm(ks[14], (DEPTH, MIX_WIDTH, D_MODEL), MIX_WIDTH ** -0.5),
        "g_ffn": gain(ks[15], (DEPTH, D_MODEL)),
        "w_ffn_in": nrm(ks[16], (DEPTH, D_MODEL, 2 * D_FF), D_MODEL ** -0.5),
        "ffn_conv_w": nrm(ks[17], (DEPTH, CONV_K, 2 * D_FF), CONV_K ** -0.5),
        "w_ffn_out": nrm(ks[18], (DEPTH, D_FF, D_MODEL), D_FF ** -0.5),
    }


def reference(x_prompt, x_sample, mem_prompt, mem_sample, g_mix, w_in, na_q_gain, na_k_gain,
              na_rel_bias, conv_w, mem_norm_g, w_mem_kv, mem_q_gain, mem_k_gain, w_out,
              g_ffn, w_ffn_in, ffn_conv_w, w_ffn_out):
    y_prompt = x_prompt
    y_sample = x_sample
    for l in range(DEPTH):
        p = (g_mix[l], w_in[l], na_q_gain[l], na_k_gain[l], na_rel_bias[l], conv_w[l],
             mem_norm_g[l], w_mem_kv[l], mem_q_gain[l], mem_k_gain[l], w_out[l],
             g_ffn[l], w_ffn_in[l], ffn_conv_w[l], w_ffn_out[l])
        y_prompt = encoder_layer(y_prompt, mem_prompt, *p)
        y_sample = encoder_layer(y_sample, mem_sample, *p)
    return (y_prompt, y_sample)
```

```python
import functools
import math

import jax
import jax.numpy as jnp
from jax import lax
from jax.experimental import pallas as pl
from jax.experimental.pallas import tpu as pltpu

F32 = jnp.float32
BF16 = jnp.bfloat16

GRID_W = 64
WIN_H = 8
WIN_W = 16
HEAD_DIM = 64
NA_HEADS = 16
NA_WIDTH = NA_HEADS * HEAD_DIM
CONV_WIDTH = 512
MEM_HEADS = 4
MEM_HEAD_DIM = 128
MEM_WIDTH = MEM_HEADS * MEM_HEAD_DIM
EPS = 1e-6
NEG_INF = -1e30

LANES = 128
BF16_ROWS = 16
F32_ROWS = 8
VMEM_LIMIT = 56 * 1024 * 1024

HEAD_PAIRS = NA_HEADS // 2
KEY_ROWS_HALO = WIN_H // 2
N_DR = 2 * WIN_H - 1


def _rms(x, g):
    return x * lax.rsqrt(jnp.mean(x * x, axis=-1, keepdims=True) + EPS) * g


def _lo_lanes():
    return lax.broadcasted_iota(jnp.int32, (1, LANES), 1) < HEAD_DIM


def _pair_rms(z, g2):
    lo = _lo_lanes()
    sq = z * z
    tot = jnp.sum(sq, axis=-1, keepdims=True)
    s0 = jnp.sum(jnp.where(lo, sq, 0.0), axis=-1, keepdims=True)
    ms = jnp.where(lo, s0, tot - s0) * (1.0 / HEAD_DIM)
    return z * lax.rsqrt(ms + EPS) * g2


def _softmax_pv(s, v):
    m = jnp.max(s, axis=-1, keepdims=True)
    e = jnp.exp(s - m)
    l = jnp.sum(e, axis=-1, keepdims=True)
    o = jnp.dot(e.astype(BF16), v, preferred_element_type=F32)
    return o * (1.0 / l)


def _mem_kv_kernel(mem_ref, g_ref, w_ref, kg_ref, mk_ref, mv_ref):
    n = _rms(mem_ref[...], g_ref[...]).astype(BF16)
    z = jnp.dot(n, w_ref[...], preferred_element_type=F32)
    for h in range(MEM_HEADS):
        c = slice(h * MEM_HEAD_DIM, (h + 1) * MEM_HEAD_DIM)
        mk_ref[:, c] = _rms(z[:, c], kg_ref[...]).astype(BF16)
    mv_ref[...] = z[:, MEM_WIDTH:].astype(BF16)


def _mem_kv(mem, g, w_b, k_gain):
    b, m, d = mem.shape
    out = jax.ShapeDtypeStruct((b, m, MEM_WIDTH), BF16)
    return pl.pallas_call(
        _mem_kv_kernel,
        grid=(b,),
        in_specs=[
            pl.BlockSpec((None, m, d), lambda i: (i, 0, 0)),
            pl.BlockSpec((1, d), lambda i: (0, 0)),
            pl.BlockSpec((d, 2 * MEM_WIDTH), lambda i: (0, 0)),
            pl.BlockSpec((1, MEM_HEAD_DIM), lambda i: (0, 0)),
        ],
        out_specs=[pl.BlockSpec((None, m, MEM_WIDTH), lambda i: (i, 0, 0))] * 2,
        out_shape=[out, out],
        compiler_params=pltpu.CompilerParams(
            dimension_semantics=("parallel",), vmem_limit_bytes=VMEM_LIMIT),
        name="mem_kv",
    )(mem, g, w_b, k_gain)


_Q0, _K0, _V0 = 0, NA_WIDTH, 2 * NA_WIDTH
_H0 = 3 * NA_WIDTH
_B0 = _H0 + CONV_WIDTH
_C0 = _B0 + CONV_WIDTH
_QM0 = _C0 + CONV_WIDTH
_PROJ_CHUNK = 512


def _in_proj_kernel(x_ref, g_ref, w_ref, qg_ref, kg_ref, mqg_ref,
                    q_ref, k_ref, v_ref, p_ref, b_ref, qm_ref):
    n = _rms(x_ref[...], g_ref[...]).astype(BF16)

    def proj(c0):
        return jnp.dot(n, w_ref[:, c0:c0 + _PROJ_CHUNK], preferred_element_type=F32)

    for base, gain_ref, out_ref in ((_Q0, qg_ref, q_ref), (_K0, kg_ref, k_ref)):
        for cc in range(NA_WIDTH // _PROJ_CHUNK):
            z = proj(base + cc * _PROJ_CHUNK)
            for s in range(_PROJ_CHUNK // LANES):
                o = cc * _PROJ_CHUNK + s * LANES
                out_ref[:, o:o + LANES] = _pair_rms(
                    z[:, s * LANES:(s + 1) * LANES], gain_ref[...]).astype(BF16)
    for cc in range(NA_WIDTH // _PROJ_CHUNK):
        o = cc * _PROJ_CHUNK
        v_ref[:, o:o + _PROJ_CHUNK] = proj(_V0 + o).astype(BF16)
    p_ref[...] = (proj(_C0) * proj(_H0)).astype(BF16)
    b_ref[...] = proj(_B0).astype(BF16)
    z = proj(_QM0)
    for h in range(MEM_HEADS):
        c = slice(h * MEM_HEAD_DIM, (h + 1) * MEM_HEAD_DIM)
        qm_ref[:, c] = _rms(z[:, c], mqg_ref[...]).astype(BF16)


def _in_proj(x, g, w_b, q_gain2, k_gain2, mq_gain, tm):
    b, t, d = x.shape
    nt = t // tm
    row = lambda w: pl.BlockSpec((None, tm, w), lambda bi, i: (bi, i, 0))
    const = lambda shape: pl.BlockSpec(shape, lambda bi, i: (0,) * len(shape))
    sds = lambda w: jax.ShapeDtypeStruct((b, t, w), BF16)
    return pl.pallas_call(
        _in_proj_kernel,
        grid=(b, nt),
        in_specs=[row(d), const((1, d)), const(w_b.shape),
                  const((1, LANES)), const((1, LANES)), const((1, MEM_HEAD_DIM))],
        out_specs=[row(NA_WIDTH)] * 3 + [row(CONV_WIDTH)] * 2 + [row(MEM_WIDTH)],
        out_shape=[sds(NA_WIDTH)] * 3 + [sds(CONV_WIDTH)] * 2 + [sds(MEM_WIDTH)],
        compiler_params=pltpu.CompilerParams(
            dimension_semantics=("parallel", "parallel"), vmem_limit_bytes=VMEM_LIMIT),
        name="in_proj",
    )(x, g, w_b, q_gain2, k_gain2, mq_gain)


def _bias_table(rel_bias):
    qc = jnp.arange(GRID_W)[:, None]
    kc = jnp.arange(GRID_W)[None, :]
    q_cs = jnp.clip(qc - WIN_W // 2, 0, GRID_W - WIN_W)
    valid = (kc >= q_cs) & (kc < q_cs + WIN_W)
    dc = jnp.clip(kc - qc, -(WIN_W - 1), WIN_W - 1) + (WIN_W - 1)
    tb = jnp.where(valid[None, None], rel_bias.astype(F32)[:, :, dc], NEG_INF)
    tb = jnp.concatenate([tb[:, :-1], tb[:, 1:]], axis=-1)
    tb = tb.reshape(HEAD_PAIRS, 2, N_DR - 1, GRID_W, LANES).transpose(0, 2, 1, 3, 4)
    return tb.reshape(HEAD_PAIRS, N_DR - 1, 2 * GRID_W, LANES)


def _mixers_kernel(q_ref, k_ref, kp_ref, kn_ref, v_ref, vp_ref, vn_ref,
                   p_ref, pp_ref, pn_ref, b_ref, qm_ref, mk_ref, mv_ref, bt_ref, cw_ref,
                   y_ref, kext, vext, pext, *, rows, tile_rows):
    i = pl.program_id(1)
    nt = pl.num_programs(1)
    tq = tile_rows * GRID_W
    halo = KEY_ROWS_HALO * GRID_W
    win = WIN_H * GRID_W
    lo = _lo_lanes()

    for ext, prev, main, nxt in ((kext, kp_ref, k_ref, kn_ref), (vext, vp_ref, v_ref, vn_ref)):
        ext[0:halo] = prev[...]
        ext[halo:halo + tq] = main[...]
        ext[halo + tq:halo + tq + halo] = nxt[...]

    r0 = i * tile_rows

    def row_body(lr, carry):
        r = r0 + lr
        ws = jnp.clip(r - WIN_H // 2, 0, rows - WIN_H)
        start = pl.multiple_of((ws - r0 + KEY_ROWS_HALO) * GRID_W, GRID_W)
        dr0 = ws - r + (WIN_H - 1)
        qrow = pl.multiple_of(lr * GRID_W, GRID_W)
        for hp in range(HEAD_PAIRS):
            c = slice(hp * LANES, (hp + 1) * LANES)
            q2 = q_ref[pl.ds(qrow, GRID_W), c]
            zero = jnp.zeros_like(q2)
            lhs = jnp.concatenate([jnp.where(lo, q2, zero), jnp.where(lo, zero, q2)], axis=0)
            s = lax.dot_general(lhs, kext[pl.ds(start, win), c], (((1,), (1,)), ((), ())),
                                preferred_element_type=F32)
            bias = jnp.concatenate([bt_ref[hp, dr0 + 2 * a] for a in range(WIN_H // 2)], axis=1)
            o = _softmax_pv(s + bias, vext[pl.ds(start, win), c])
            y_ref[pl.ds(qrow, GRID_W), c] = jnp.where(lo, o[:GRID_W], o[GRID_W:]).astype(BF16)
        return carry

    lax.fori_loop(0, tile_rows, row_body, 0)

    pext[0:F32_ROWS] = jnp.where(i > 0, pp_ref[...].astype(F32)[F32_ROWS:], 0.0)
    pext[F32_ROWS:F32_ROWS + tq] = p_ref[...].astype(F32)
    pext[F32_ROWS + tq:2 * F32_ROWS + tq] = jnp.where(
        i < nt - 1, pn_ref[...].astype(F32)[:F32_ROWS], 0.0)
    conv = (cw_ref[0:1] * pext[F32_ROWS - 1:F32_ROWS - 1 + tq]
            + cw_ref[1:2] * pext[F32_ROWS:F32_ROWS + tq]
            + cw_ref[2:3] * pext[F32_ROWS + 1:F32_ROWS + 1 + tq])
    y_ref[:, NA_WIDTH:NA_WIDTH + CONV_WIDTH] = (b_ref[...].astype(F32) * conv).astype(BF16)

    scale = 1.0 / math.sqrt(MEM_HEAD_DIM)
    for h in range(MEM_HEADS):
        c = slice(h * MEM_HEAD_DIM, (h + 1) * MEM_HEAD_DIM)
        s = lax.dot_general(qm_ref[:, c], mk_ref[:, c], (((1,), (1,)), ((), ())),
                            preferred_element_type=F32) * scale
        o0 = NA_WIDTH + CONV_WIDTH + h * MEM_HEAD_DIM
        y_ref[:, o0:o0 + MEM_HEAD_DIM] = _softmax_pv(s, mv_ref[:, c]).astype(BF16)


def _mixers(q, k, v, p, bb, qm, mk, mv, bias_tab, conv_w, tile_rows):
    b, t, _ = q.shape
    rows = t // GRID_W
    tq = tile_rows * GRID_W
    nt = t // tq
    halo = KEY_ROWS_HALO * GRID_W
    assert rows >= WIN_H and rows % tile_rows == 0 and tile_rows >= KEY_ROWS_HALO
    assert tq % halo == 0 and tq % BF16_ROWS == 0
    n_mem = mk.shape[1]

    def main(w):
        return pl.BlockSpec((None, tq, w), lambda bi, i: (bi, i, 0))

    def prev(blk, w):
        per = tq // blk
        return pl.BlockSpec((None, blk, w), lambda bi, i: (bi, jnp.maximum(i * per - 1, 0), 0))

    def nxt(blk, w):
        per = tq // blk
        last = t // blk - 1
        return pl.BlockSpec((None, blk, w), lambda bi, i: (bi, jnp.minimum((i + 1) * per, last), 0))

    kv_specs = [main(NA_WIDTH), prev(halo, NA_WIDTH), nxt(halo, NA_WIDTH)]
    mem_spec = pl.BlockSpec((None, n_mem, MEM_WIDTH), lambda bi, i: (bi, 0, 0))
    d_mix = NA_WIDTH + CONV_WIDTH + MEM_WIDTH
    return pl.pallas_call(
        functools.partial(_mixers_kernel, rows=rows, tile_rows=tile_rows),
        grid=(b, nt),
        in_specs=[main(NA_WIDTH)] + kv_specs + kv_specs
                 + [main(CONV_WIDTH), prev(BF16_ROWS, CONV_WIDTH), nxt(BF16_ROWS, CONV_WIDTH),
                    main(CONV_WIDTH), main(MEM_WIDTH), mem_spec, mem_spec,
                    pl.BlockSpec(bias_tab.shape, lambda bi, i: (0, 0, 0, 0)),
                    pl.BlockSpec(conv_w.shape, lambda bi, i: (0, 0))],
        out_specs=main(d_mix),
        out_shape=jax.ShapeDtypeStruct((b, t, d_mix), BF16),
        scratch_shapes=[pltpu.VMEM((tq + 2 * halo, NA_WIDTH), BF16),
                        pltpu.VMEM((tq + 2 * halo, NA_WIDTH), BF16),
                        pltpu.VMEM((tq + 2 * F32_ROWS, CONV_WIDTH), F32)],
        compiler_params=pltpu.CompilerParams(
            dimension_semantics=("parallel", "parallel"), vmem_limit_bytes=VMEM_LIMIT),
        name="mixers",
    )(q, k, k, k, v, v, v, p, p, p, bb, qm, mk, mv, bias_tab, conv_w)


def _out_proj_kernel(x_ref, y_ref, w_ref, o_ref):
    o_ref[...] = x_ref[...] + jnp.dot(y_ref[...], w_ref[...], preferred_element_type=F32)


def _out_proj(x, y, w_b, tm):
    b, t, d = x.shape
    row = lambda w: pl.BlockSpec((None, tm, w), lambda bi, i: (bi, i, 0))
    return pl.pallas_call(
        _out_proj_kernel,
        grid=(b, t // tm),
        in_specs=[row(d), row(y.shape[-1]), pl.BlockSpec(w_b.shape, lambda bi, i: (0, 0))],
        out_specs=row(d),
        out_shape=jax.ShapeDtypeStruct((b, t, d), F32),
        compiler_params=pltpu.CompilerParams(
            dimension_semantics=("parallel", "parallel"), vmem_limit_bytes=VMEM_LIMIT),
        name="out_proj",
    )(x, y, w_b)


def _ffn_kernel(x_ref, xp_ref, xn_ref, g_ref, wa_ref, wg_ref, cwa_ref, cwg_ref, wo_ref,
                o_ref, next_ref, pa_ref, pg_ref, *, tm):
    i = pl.program_id(1)
    j = pl.program_id(2)
    nt = pl.num_programs(1)
    h0 = BF16_ROWS

    @pl.when(j == 0)
    def _():
        g = g_ref[...]
        zeros = jnp.zeros((F32_ROWS, x_ref.shape[-1]), F32)
        prev = jnp.where(i > 0, _rms(xp_ref[...], g), 0.0)
        nxt = jnp.where(i < nt - 1, _rms(xn_ref[...], g), 0.0)
        next_ref[0:h0] = jnp.concatenate([zeros, prev], axis=0).astype(BF16)
        next_ref[h0:h0 + tm] = _rms(x_ref[...], g).astype(BF16)
        next_ref[h0 + tm:2 * h0 + tm] = jnp.concatenate([nxt, zeros], axis=0).astype(BF16)
        o_ref[...] = x_ref[...]

    n2 = next_ref[...]
    pa_ref[...] = jnp.dot(n2, wa_ref[...], preferred_element_type=F32)
    pg_ref[...] = jnp.dot(n2, wg_ref[...], preferred_element_type=F32)

    def conv(ref, cw):
        return (cw[0:1] * ref[h0 - 1:h0 - 1 + tm] + cw[1:2] * ref[h0:h0 + tm]
                + cw[2:3] * ref[h0 + 1:h0 + 1 + tm])

    hidden = (jax.nn.silu(conv(pa_ref, cwa_ref)) * conv(pg_ref, cwg_ref)).astype(BF16)
    o_ref[...] += jnp.dot(hidden, wo_ref[...], preferred_element_type=F32)


def _ffn(x1, g, w_in_b, conv_w, w_out_b, tm, tf):
    b, t, d = x1.shape
    d_ff = w_out_b.shape[0]
    nj = d_ff // tf
    assert d_ff % tf == 0 and t % tm == 0 and tm % F32_ROWS == 0
    per = tm // F32_ROWS
    last = t // F32_ROWS - 1
    return pl.pallas_call(
        functools.partial(_ffn_kernel, tm=tm),
        grid=(b, t // tm, nj),
        in_specs=[
            pl.BlockSpec((None, tm, d), lambda bi, i, j: (bi, i, 0)),
            pl.BlockSpec((None, F32_ROWS, d), lambda bi, i, j: (bi, jnp.maximum(i * per - 1, 0), 0)),
            pl.BlockSpec((None, F32_ROWS, d),
                         lambda bi, i, j: (bi, jnp.minimum((i + 1) * per, last), 0)),
            pl.BlockSpec((1, d), lambda bi, i, j: (0, 0)),
            pl.BlockSpec((d, tf), lambda bi, i, j: (0, j)),
            pl.BlockSpec((d, tf), lambda bi, i, j: (0, nj + j)),
            pl.BlockSpec((3, tf), lambda bi, i, j: (0, j)),
            pl.BlockSpec((3, tf), lambda bi, i, j: (0, nj + j)),
            pl.BlockSpec((tf, d), lambda bi, i, j: (j, 0)),
        ],
        out_specs=pl.BlockSpec((None, tm, d), lambda bi, i, j: (bi, i, 0)),
        out_shape=jax.ShapeDtypeStruct((b, t, d), F32),
        scratch_shapes=[pltpu.VMEM((tm + 2 * BF16_ROWS, d), BF16),
                        pltpu.VMEM((tm + 2 * BF16_ROWS, tf), F32),
                        pltpu.VMEM((tm + 2 * BF16_ROWS, tf), F32)],
        compiler_params=pltpu.CompilerParams(
            dimension_semantics=("parallel", "parallel", "arbitrary"),
            vmem_limit_bytes=VMEM_LIMIT),
        name="ffn",
    )(x1, x1, x1, g, w_in_b, w_in_b, conv_w, conv_w, w_out_b)


_TM_PROJ = 512
_TILE_ROWS = 8
_TM_FFN = 512
_TF_FFN = 512


def _layer(x, mem, p):
    mk, mv = _mem_kv(mem, p["mem_norm_g"], p["w_mem_kv"], p["mem_k_gain"])
    q, k, v, ch, bb, qm = _in_proj(x, p["g_mix"], p["w_in"], p["na_q_gain"], p["na_k_gain"],
                                   p["mem_q_gain"], _TM_PROJ)
    y = _mixers(q, k, v, ch, bb, qm, mk, mv, p["bias_tab"], p["conv_w"], _TILE_ROWS)
    x1 = _out_proj(x, y, p["w_out"], _TM_PROJ)
    return _ffn(x1, p["g_ffn"], p["w_ffn_in"], p["ffn_conv_w"], p["w_ffn_out"], _TM_FFN, _TF_FFN)


def kernel(x_prompt, x_sample, mem_prompt, mem_sample, g_mix, w_in, na_q_gain, na_k_gain,
           na_rel_bias, conv_w, mem_norm_g, w_mem_kv, mem_q_gain, mem_k_gain, w_out,
           g_ffn, w_ffn_in, ffn_conv_w, w_ffn_out):
    y_prompt, y_sample = x_prompt, x_sample
    for l in range(g_mix.shape[0]):
        row = lambda a: a[l].reshape(1, -1).astype(F32)
        q_scale = 1.0 / math.sqrt(HEAD_DIM)
        p = dict(
            g_mix=row(g_mix), g_ffn=row(g_ffn), mem_norm_g=row(mem_norm_g),
            na_q_gain=jnp.tile(row(na_q_gain), (1, 2)) * q_scale,
            na_k_gain=jnp.tile(row(na_k_gain), (1, 2)),
            mem_q_gain=row(mem_q_gain), mem_k_gain=row(mem_k_gain),
            w_in=w_in[l].astype(BF16), w_mem_kv=w_mem_kv[l].astype(BF16),
            w_out=w_out[l].astype(BF16), w_ffn_in=w_ffn_in[l].astype(BF16),
            w_ffn_out=w_ffn_out[l].astype(BF16),
            conv_w=conv_w[l].astype(F32), ffn_conv_w=ffn_conv_w[l].astype(F32),
            bias_tab=_bias_table(na_rel_bias[l]),
        )
        y_prompt = _layer(y_prompt, mem_prompt, p)
        y_sample = _layer(y_sample, mem_sample, p)
    return (y_prompt, y_sample)
```

```python
import functools
import math

import jax
import jax.numpy as jnp
from jax import lax
from jax.experimental import pallas as pl
from jax.experimental.pallas import tpu as pltpu

F32 = jnp.float32
BF16 = jnp.bfloat16

GRID_W = 64
WIN_H = 8
WIN_W = 16
HEAD_DIM = 64
NA_HEADS = 16
NA_WIDTH = NA_HEADS * HEAD_DIM
CONV_WIDTH = 512
MEM_HEADS = 4
MEM_HEAD_DIM = 128
MEM_WIDTH = MEM_HEADS * MEM_HEAD_DIM
EPS = 1e-6
NEG_INF = -1e30

LANES = 128
BF16_ROWS = 16
F32_ROWS = 8
VMEM_LIMIT = 56 * 1024 * 1024

HEAD_PAIRS = NA_HEADS // 2
KEY_ROWS_HALO = WIN_H // 2
N_DR = 2 * WIN_H - 1


def _rms(x, g):
    return x * lax.rsqrt(jnp.mean(x * x, axis=-1, keepdims=True) + EPS) * g


def _lo_lanes():
    return lax.broadcasted_iota(jnp.int32, (1, LANES), 1) < HEAD_DIM


def _pair_rms(z, g2):
    lo = _lo_lanes()
    sq = z * z
    tot = jnp.sum(sq, axis=-1, keepdims=True)
    s0 = jnp.sum(jnp.where(lo, sq, 0.0), axis=-1, keepdims=True)
    ms = jnp.where(lo, s0, tot - s0) * (1.0 / HEAD_DIM)
    return z * lax.rsqrt(ms + EPS) * g2


def _softmax_pv(s, v):
    m = jnp.max(s, axis=-1, keepdims=True)
    e = jnp.exp(s - m)
    l = jnp.sum(e, axis=-1, keepdims=True)
    o = jnp.dot(e.astype(BF16), v, preferred_element_type=F32)
    return o * (1.0 / l)


def _mem_kv_kernel(mem_ref, g_ref, w_ref, kg_ref, mk_ref, mv_ref):
    n = _rms(mem_ref[...], g_ref[...]).astype(BF16)
    z = jnp.dot(n, w_ref[...], preferred_element_type=F32)
    for h in range(MEM_HEADS):
        c = slice(h * MEM_HEAD_DIM, (h + 1) * MEM_HEAD_DIM)
        mk_ref[:, c] = _rms(z[:, c], kg_ref[...]).astype(BF16)
    mv_ref[...] = z[:, MEM_WIDTH:].astype(BF16)


def _mem_kv(mem, g, w_b, k_gain):
    b, m, d = mem.shape
    out = jax.ShapeDtypeStruct((b, m, MEM_WIDTH), BF16)
    return pl.pallas_call(
        _mem_kv_kernel,
        grid=(b,),
        in_specs=[
            pl.BlockSpec((None, m, d), lambda i: (i, 0, 0)),
            pl.BlockSpec((1, d), lambda i: (0, 0)),
            pl.BlockSpec((d, 2 * MEM_WIDTH), lambda i: (0, 0)),
            pl.BlockSpec((1, MEM_HEAD_DIM), lambda i: (0, 0)),
        ],
        out_specs=[pl.BlockSpec((None, m, MEM_WIDTH), lambda i: (i, 0, 0))] * 2,
        out_shape=[out, out],
        compiler_params=pltpu.CompilerParams(
            dimension_semantics=("parallel",), vmem_limit_bytes=VMEM_LIMIT),
        name="mem_kv",
    )(mem, g, w_b, k_gain)


_Q0, _K0, _V0 = 0, NA_WIDTH, 2 * NA_WIDTH
_H0 = 3 * NA_WIDTH
_B0 = _H0 + CONV_WIDTH
_C0 = _B0 + CONV_WIDTH
_QM0 = _C0 + CONV_WIDTH
_PROJ_CHUNK = 512


def _in_proj_kernel(x_ref, g_ref, w_ref, qg_ref, kg_ref, mqg_ref,
                    q_ref, k_ref, v_ref, p_ref, b_ref, qm_ref):
    n = _rms(x_ref[...], g_ref[...]).astype(BF16)

    def proj(c0):
        return jnp.dot(n, w_ref[:, c0:c0 + _PROJ_CHUNK], preferred_element_type=F32)

    for base, gain_ref, out_ref in ((_Q0, qg_ref, q_ref), (_K0, kg_ref, k_ref)):
        for cc in range(NA_WIDTH // _PROJ_CHUNK):
            z = proj(base + cc * _PROJ_CHUNK)
            for s in range(_PROJ_CHUNK // LANES):
                o = cc * _PROJ_CHUNK + s * LANES
                out_ref[:, o:o + LANES] = _pair_rms(
                    z[:, s * LANES:(s + 1) * LANES], gain_ref[...]).astype(BF16)
    for cc in range(NA_WIDTH // _PROJ_CHUNK):
        o = cc * _PROJ_CHUNK
        v_ref[:, o:o + _PROJ_CHUNK] = proj(_V0 + o).astype(BF16)
    p_ref[...] = (proj(_C0) * proj(_H0)).astype(BF16)
    b_ref[...] = proj(_B0).astype(BF16)
    z = proj(_QM0)
    for h in range(MEM_HEADS):
        c = slice(h * MEM_HEAD_DIM, (h + 1) * MEM_HEAD_DIM)
        qm_ref[:, c] = _rms(z[:, c], mqg_ref[...]).astype(BF16)


def _in_proj(x, g, w_b, q_gain2, k_gain2, mq_gain, tm):
    b, t, d = x.shape
    nt = t // tm
    row = lambda w: pl.BlockSpec((None, tm, w), lambda bi, i: (bi, i, 0))
    const = lambda shape: pl.BlockSpec(shape, lambda bi, i: (0,) * len(shape))
    sds = lambda w: jax.ShapeDtypeStruct((b, t, w), BF16)
    return pl.pallas_call(
        _in_proj_kernel,
        grid=(b, nt),
        in_specs=[row(d), const((1, d)), const(w_b.shape),
                  const((1, LANES)), const((1, LANES)), const((1, MEM_HEAD_DIM))],
        out_specs=[row(NA_WIDTH)] * 3 + [row(CONV_WIDTH)] * 2 + [row(MEM_WIDTH)],
        out_shape=[sds(NA_WIDTH)] * 3 + [sds(CONV_WIDTH)] * 2 + [sds(MEM_WIDTH)],
        compiler_params=pltpu.CompilerParams(
            dimension_semantics=("parallel", "parallel"), vmem_limit_bytes=VMEM_LIMIT),
        name="in_proj",
    )(x, g, w_b, q_gain2, k_gain2, mq_gain)


COL_BLOCKS = GRID_W // WIN_W
Q_BLOCKS = GRID_W // WIN_W
SLAB = 2 * WIN_W
_NEEDED_COL_BLOCKS = tuple(
    tuple(cb for cb in range(COL_BLOCKS)
          if any(max(0, min(q - WIN_W // 2, GRID_W - WIN_W)) < (cb + 1) * WIN_W
                 and max(0, min(q - WIN_W // 2, GRID_W - WIN_W)) + WIN_W > cb * WIN_W
                 for q in range(j * WIN_W, (j + 1) * WIN_W)))
    for j in range(Q_BLOCKS))


def _bias_table(rel_bias):
    h = rel_bias.shape[0]
    qc = jnp.arange(GRID_W)[:, None]
    kc = jnp.arange(GRID_W)[None, :]
    q_cs = jnp.clip(qc - WIN_W // 2, 0, GRID_W - WIN_W)
    valid = (kc >= q_cs) & (kc < q_cs + WIN_W)
    lo_pad, hi_pad = GRID_W - 1, GRID_W - WIN_W
    padded = jnp.pad(rel_bias.astype(F32), ((0, 0), (0, 0), (lo_pad, hi_pad)))
    off = lo_pad + WIN_W - 1
    tb = jnp.stack([padded[:, :, off - q:off - q + GRID_W] for q in range(GRID_W)], axis=2)
    tb = jnp.where(valid[None, None], tb, NEG_INF)
    tb = tb.reshape(h // 2, 2, N_DR, GRID_W, COL_BLOCKS, WIN_W)
    tb = tb.transpose(0, 4, 2, 5, 1, 3)
    return tb.reshape(h // 2, COL_BLOCKS, N_DR * WIN_W, LANES).astype(BF16)


def _mixers_kernel(q_ref, k_ref, kp_ref, kn_ref, v_ref, vp_ref, vn_ref,
                   p_ref, pp_ref, pn_ref, b_ref, qm_ref, mk_ref, mv_ref, bt_ref, cw_ref,
                   y_ref, kext, vext, pext, s_ref, *, rows, tile_rows):
    i = pl.program_id(1)
    nt = pl.num_programs(1)
    tq = tile_rows * GRID_W
    halo = KEY_ROWS_HALO * GRID_W
    lo = _lo_lanes()

    rho = lax.broadcasted_iota(jnp.int32, (2 * GRID_W, LANES), 0)
    lane = lax.broadcasted_iota(jnp.int32, (2 * GRID_W, LANES), 1)
    target = jnp.where((rho & WIN_W) != 0, GRID_W, 0) + (rho >> (SLAB.bit_length() - 1)) * WIN_W + (rho & (WIN_W - 1))
    onehot = (lane == target).astype(BF16)

    for ext, prev, main, nxt in ((kext, kp_ref, k_ref, kn_ref), (vext, vp_ref, v_ref, vn_ref)):
        ext[0:halo] = prev[...]
        ext[halo:halo + tq] = main[...]
        ext[halo + tq:halo + tq + halo] = nxt[...]

    r0 = i * tile_rows

    def row_body(lr, carry):
        r = r0 + lr
        ws = jnp.clip(r - WIN_H // 2, 0, rows - WIN_H)
        start = pl.multiple_of((ws - r0 + KEY_ROWS_HALO) * GRID_W, GRID_W)
        dr0 = ws - r + (WIN_H - 1)
        qrow = pl.multiple_of(lr * GRID_W, GRID_W)
        brow = pl.multiple_of(dr0 * WIN_W, WIN_W)

        def window(ext, c):
            return jnp.concatenate(
                [ext[pl.ds(pl.multiple_of(start + wr * GRID_W + cb * WIN_W, WIN_W), WIN_W), c]
                 for cb in range(COL_BLOCKS) for wr in range(WIN_H)], axis=0)

        for hp in range(HEAD_PAIRS):
            c = slice(hp * LANES, (hp + 1) * LANES)
            q2 = q_ref[pl.ds(qrow, GRID_W), c]
            zero = jnp.zeros((WIN_W, LANES), BF16)
            pieces = []
            for j in range(Q_BLOCKS):
                qj = q2[j * WIN_W:(j + 1) * WIN_W]
                pieces += [jnp.where(lo, qj, zero), jnp.where(lo, zero, qj)]
            lhs = jnp.concatenate([jnp.concatenate(pieces, axis=0), onehot], axis=1)
            bias = jnp.concatenate(
                [bt_ref[hp, cb, pl.ds(brow, WIN_H * WIN_W), :] for cb in range(COL_BLOCKS)], axis=0)
            rhs = jnp.concatenate([window(kext, c), bias], axis=1)
            s_ref[hp] = lax.dot_general(lhs, rhs, (((1,), (1,)), ((), ())),
                                        preferred_element_type=F32)

        for hp in range(HEAD_PAIRS):
            c = slice(hp * LANES, (hp + 1) * LANES)
            p_rows, inv_l = [], []
            for j in range(Q_BLOCKS):
                rs = slice(j * SLAB, (j + 1) * SLAB)
                need = _NEEDED_COL_BLOCKS[j]
                sb = [s_ref[hp, rs, cb * LANES:(cb + 1) * LANES] for cb in need]
                m = jnp.max(functools.reduce(jnp.maximum, sb), axis=-1, keepdims=True)
                e = [jnp.exp(x - m) for x in sb]
                inv_l.append(1.0 / jnp.sum(functools.reduce(jnp.add, e), axis=-1, keepdims=True))
                zero = jnp.zeros((SLAB, LANES), BF16)
                blocks = [zero] * COL_BLOCKS
                for cb, x in zip(need, e):
                    blocks[cb] = x.astype(BF16)
                p_rows.append(jnp.concatenate(blocks, axis=1))
            o = jnp.dot(jnp.concatenate(p_rows, axis=0), window(vext, c),
                        preferred_element_type=F32)
            for j in range(Q_BLOCKS):
                oj = o[j * SLAB:(j + 1) * SLAB] * inv_l[j]
                y_ref[pl.ds(pl.multiple_of(qrow + j * WIN_W, WIN_W), WIN_W), c] = jnp.where(
                    lo, oj[:WIN_W], oj[WIN_W:]).astype(BF16)
        return carry

    lax.fori_loop(0, tile_rows, row_body, 0)

    pext[0:F32_ROWS] = jnp.where(i > 0, pp_ref[...].astype(F32)[F32_ROWS:], 0.0)
    pext[F32_ROWS:F32_ROWS + tq] = p_ref[...].astype(F32)
    pext[F32_ROWS + tq:2 * F32_ROWS + tq] = jnp.where(
        i < nt - 1, pn_ref[...].astype(F32)[:F32_ROWS], 0.0)
    conv = (cw_ref[0:1] * pext[F32_ROWS - 1:F32_ROWS - 1 + tq]
            + cw_ref[1:2] * pext[F32_ROWS:F32_ROWS + tq]
            + cw_ref[2:3] * pext[F32_ROWS + 1:F32_ROWS + 1 + tq])
    y_ref[:, NA_WIDTH:NA_WIDTH + CONV_WIDTH] = (b_ref[...].astype(F32) * conv).astype(BF16)

    scale = 1.0 / math.sqrt(MEM_HEAD_DIM)
    for h in range(MEM_HEADS):
        c = slice(h * MEM_HEAD_DIM, (h + 1) * MEM_HEAD_DIM)
        s = lax.dot_general(qm_ref[:, c], mk_ref[:, c], (((1,), (1,)), ((), ())),
                            preferred_element_type=F32) * scale
        o0 = NA_WIDTH + CONV_WIDTH + h * MEM_HEAD_DIM
        y_ref[:, o0:o0 + MEM_HEAD_DIM] = _softmax_pv(s, mv_ref[:, c]).astype(BF16)


def _mixers(q, k, v, p, bb, qm, mk, mv, bias_tab, conv_w, tile_rows):
    b, t, _ = q.shape
    rows = t // GRID_W
    tq = tile_rows * GRID_W
    nt = t // tq
    halo = KEY_ROWS_HALO * GRID_W
    assert rows >= WIN_H and rows % tile_rows == 0 and tile_rows >= KEY_ROWS_HALO
    assert tq % halo == 0 and tq % BF16_ROWS == 0
    n_mem = mk.shape[1]

    def main(w):
        return pl.BlockSpec((None, tq, w), lambda bi, i: (bi, i, 0))

    def prev(blk, w):
        per = tq // blk
        return pl.BlockSpec((None, blk, w), lambda bi, i: (bi, jnp.maximum(i * per - 1, 0), 0))

    def nxt(blk, w):
        per = tq // blk
        last = t // blk - 1
        return pl.BlockSpec((None, blk, w), lambda bi, i: (bi, jnp.minimum((i + 1) * per, last), 0))

    kv_specs = [main(NA_WIDTH), prev(halo, NA_WIDTH), nxt(halo, NA_WIDTH)]
    mem_spec = pl.BlockSpec((None, n_mem, MEM_WIDTH), lambda bi, i: (bi, 0, 0))
    d_mix = NA_WIDTH + CONV_WIDTH + MEM_WIDTH
    return pl.pallas_call(
        functools.partial(_mixers_kernel, rows=rows, tile_rows=tile_rows),
        grid=(b, nt),
        in_specs=[main(NA_WIDTH)] + kv_specs + kv_specs
                 + [main(CONV_WIDTH), prev(BF16_ROWS, CONV_WIDTH), nxt(BF16_ROWS, CONV_WIDTH),
                    main(CONV_WIDTH), main(MEM_WIDTH), mem_spec, mem_spec,
                    pl.BlockSpec(bias_tab.shape, lambda bi, i: (0, 0, 0, 0)),
                    pl.BlockSpec(conv_w.shape, lambda bi, i: (0, 0))],
        out_specs=main(d_mix),
        out_shape=jax.ShapeDtypeStruct((b, t, d_mix), BF16),
        scratch_shapes=[pltpu.VMEM((tq + 2 * halo, NA_WIDTH), BF16),
                        pltpu.VMEM((tq + 2 * halo, NA_WIDTH), BF16),
                        pltpu.VMEM((tq + 2 * F32_ROWS, CONV_WIDTH), F32),
                        pltpu.VMEM((HEAD_PAIRS, 2 * GRID_W, WIN_H * GRID_W), F32)],
        compiler_params=pltpu.CompilerParams(
            dimension_semantics=("parallel", "parallel"), vmem_limit_bytes=VMEM_LIMIT),
        name="mixers",
    )(q, k, k, k, v, v, v, p, p, p, bb, qm, mk, mv, bias_tab, conv_w)


def _out_proj_kernel(x_ref, y_ref, w_ref, o_ref):
    o_ref[...] = x_ref[...] + jnp.dot(y_ref[...], w_ref[...], preferred_element_type=F32)


def _out_proj(x, y, w_b, tm):
    b, t, d = x.shape
    row = lambda w: pl.BlockSpec((None, tm, w), lambda bi, i: (bi, i, 0))
    return pl.pallas_call(
        _out_proj_kernel,
        grid=(b, t // tm),
        in_specs=[row(d), row(y.shape[-1]), pl.BlockSpec(w_b.shape, lambda bi, i: (0, 0))],
        out_specs=row(d),
        out_shape=jax.ShapeDtypeStruct((b, t, d), F32),
        compiler_params=pltpu.CompilerParams(
            dimension_semantics=("parallel", "parallel"), vmem_limit_bytes=VMEM_LIMIT),
        name="out_proj",
    )(x, y, w_b)


def _ffn_kernel(x_ref, xp_ref, xn_ref, g_ref, wa_ref, wg_ref, cwa_ref, cwg_ref, wo_ref,
                o_ref, next_ref, pa_ref, pg_ref, *, tm):
    i = pl.program_id(1)
    j = pl.program_id(2)
    nt = pl.num_programs(1)
    h0 = BF16_ROWS

    @pl.when(j == 0)
    def _():
        g = g_ref[...]
        zeros = jnp.zeros((F32_ROWS, x_ref.shape[-1]), F32)
        prev = jnp.where(i > 0, _rms(xp_ref[...], g), 0.0)
        nxt = jnp.where(i < nt - 1, _rms(xn_ref[...], g), 0.0)
        next_ref[0:h0] = jnp.concatenate([zeros, prev], axis=0).astype(BF16)
        next_ref[h0:h0 + tm] = _rms(x_ref[...], g).astype(BF16)
        next_ref[h0 + tm:2 * h0 + tm] = jnp.concatenate([nxt, zeros], axis=0).astype(BF16)
        o_ref[...] = x_ref[...]

    n2 = next_ref[...]
    pa_ref[...] = jnp.dot(n2, wa_ref[...], preferred_element_type=F32)
    pg_ref[...] = jnp.dot(n2, wg_ref[...], preferred_element_type=F32)

    def conv(ref, cw):
        return (cw[0:1] * ref[h0 - 1:h0 - 1 + tm] + cw[1:2] * ref[h0:h0 + tm]
                + cw[2:3] * ref[h0 + 1:h0 + 1 + tm])

    hidden = (jax.nn.silu(conv(pa_ref, cwa_ref)) * conv(pg_ref, cwg_ref)).astype(BF16)
    o_ref[...] += jnp.dot(hidden, wo_ref[...], preferred_element_type=F32)


def _ffn(x1, g, w_in_b, conv_w, w_out_b, tm, tf):
    b, t, d = x1.shape
    d_ff = w_out_b.shape[0]
    nj = d_ff // tf
    assert d_ff % tf == 0 and t % tm == 0 and tm % F32_ROWS == 0
    per = tm // F32_ROWS
    last = t // F32_ROWS - 1
    return pl.pallas_call(
        functools.partial(_ffn_kernel, tm=tm),
        grid=(b, t // tm, nj),
        in_specs=[
            pl.BlockSpec((None, tm, d), lambda bi, i, j: (bi, i, 0)),
            pl.BlockSpec((None, F32_ROWS, d), lambda bi, i, j: (bi, jnp.maximum(i * per - 1, 0), 0)),
            pl.BlockSpec((None, F32_ROWS, d),
                         lambda bi, i, j: (bi, jnp.minimum((i + 1) * per, last), 0)),
            pl.BlockSpec((1, d), lambda bi, i, j: (0, 0)),
            pl.BlockSpec((d, tf), lambda bi, i, j: (0, j)),
            pl.BlockSpec((d, tf), lambda bi, i, j: (0, nj + j)),
            pl.BlockSpec((3, tf), lambda bi, i, j: (0, j)),
            pl.BlockSpec((3, tf), lambda bi, i, j: (0, nj + j)),
            pl.BlockSpec((tf, d), lambda bi, i, j: (j, 0)),
        ],
        out_specs=pl.BlockSpec((None, tm, d), lambda bi, i, j: (bi, i, 0)),
        out_shape=jax.ShapeDtypeStruct((b, t, d), F32),
        scratch_shapes=[pltpu.VMEM((tm + 2 * BF16_ROWS, d), BF16),
                        pltpu.VMEM((tm + 2 * BF16_ROWS, tf), F32),
                        pltpu.VMEM((tm + 2 * BF16_ROWS, tf), F32)],
        compiler_params=pltpu.CompilerParams(
            dimension_semantics=("parallel", "parallel", "arbitrary"),
            vmem_limit_bytes=VMEM_LIMIT),
        name="ffn",
    )(x1, x1, x1, g, w_in_b, w_in_b, conv_w, conv_w, w_out_b)


_TM_PROJ = 512
_TILE_ROWS = 8
_TM_FFN = 512
_TF_FFN = 512


def _layer(x, mem, p):
    mk, mv = _mem_kv(mem, p["mem_norm_g"], p["w_mem_kv"], p["mem_k_gain"])
    q, k, v, ch, bb, qm = _in_proj(x, p["g_mix"], p["w_in"], p["na_q_gain"], p["na_k_gain"],
                                   p["mem_q_gain"], _TM_PROJ)
    y = _mixers(q, k, v, ch, bb, qm, mk, mv, p["bias_tab"], p["conv_w"], _TILE_ROWS)
    x1 = _out_proj(x, y, p["w_out"], _TM_PROJ)
    return _ffn(x1, p["g_ffn"], p["w_ffn_in"], p["ffn_conv_w"], p["w_ffn_out"], _TM_FFN, _TF_FFN)


def kernel(x_prompt, x_sample, mem_prompt, mem_sample, g_mix, w_in, na_q_gain, na_k_gain,
           na_rel_bias, conv_w, mem_norm_g, w_mem_kv, mem_q_gain, mem_k_gain, w_out,
           g_ffn, w_ffn_in, ffn_conv_w, w_ffn_out):
    y_prompt, y_sample = x_prompt, x_sample
    for l in range(g_mix.shape[0]):
        row = lambda a: a[l].reshape(1, -1).astype(F32)
        q_scale = 1.0 / math.sqrt(HEAD_DIM)
        p = dict(
            g_mix=row(g_mix), g_ffn=row(g_ffn), mem_norm_g=row(mem_norm_g),
            na_q_gain=jnp.tile(row(na_q_gain), (1, 2)) * q_scale,
            na_k_gain=jnp.tile(row(na_k_gain), (1, 2)),
            mem_q_gain=row(mem_q_gain), mem_k_gain=row(mem_k_gain),
            w_in=w_in[l].astype(BF16), w_mem_kv=w_mem_kv[l].astype(BF16),
            w_out=w_out[l].astype(BF16), w_ffn_in=w_ffn_in[l].astype(BF16),
            w_ffn_out=w_ffn_out[l].astype(BF16),
            conv_w=conv_w[l].astype(F32), ffn_conv_w=ffn_conv_w[l].astype(F32),
            bias_tab=_bias_table(na_rel_bias[l]),
        )
        y_prompt = _layer(y_prompt, mem_prompt, p)
        y_sample = _layer(y_sample, mem_sample, p)
    return (y_prompt, y_sample)
```

```python
import functools
import math

import jax
import jax.numpy as jnp
import numpy as np
from jax import lax
from jax.experimental import pallas as pl
from jax.experimental.pallas import tpu as pltpu

F32 = jnp.float32
BF16 = jnp.bfloat16

GRID_W = 64
WIN_H = 8
WIN_W = 16
HEAD_DIM = 64
NA_HEADS = 16
NA_WIDTH = NA_HEADS * HEAD_DIM
CONV_WIDTH = 512
MEM_HEADS = 4
MEM_HEAD_DIM = 128
MEM_WIDTH = MEM_HEADS * MEM_HEAD_DIM
EPS = 1e-6
NEG_INF = -1e30

LANES = 128
BF16_ROWS = 16
F32_ROWS = 8
VMEM_LIMIT = 56 * 1024 * 1024

HEAD_PAIRS = NA_HEADS // 2
KEY_ROWS_HALO = WIN_H // 2
N_DR = 2 * WIN_H - 1


def _rms(x, g):
    return x * lax.rsqrt(jnp.mean(x * x, axis=-1, keepdims=True) + EPS) * g


def _lo_lanes():
    return lax.broadcasted_iota(jnp.int32, (1, LANES), 1) < HEAD_DIM


def _pair_rms(z, g2):
    lo = _lo_lanes()
    sq = z * z
    tot = jnp.sum(sq, axis=-1, keepdims=True)
    s0 = jnp.sum(jnp.where(lo, sq, 0.0), axis=-1, keepdims=True)
    ms = jnp.where(lo, s0, tot - s0) * (1.0 / HEAD_DIM)
    return z * lax.rsqrt(ms + EPS) * g2


def _softmax_pv(s, v):
    m = jnp.max(s, axis=-1, keepdims=True)
    e = jnp.exp(s - m)
    l = jnp.sum(e, axis=-1, keepdims=True)
    o = jnp.dot(e.astype(BF16), v, preferred_element_type=F32)
    return o * (1.0 / l)


def _mem_kv_kernel(mem_ref, g_ref, w_ref, kg_ref, mk_ref, mv_ref):
    n = _rms(mem_ref[...], g_ref[...]).astype(BF16)
    z = jnp.dot(n, w_ref[...], preferred_element_type=F32)
    for h in range(MEM_HEADS):
        c = slice(h * MEM_HEAD_DIM, (h + 1) * MEM_HEAD_DIM)
        mk_ref[:, c] = _rms(z[:, c], kg_ref[...]).astype(BF16)
    mv_ref[...] = z[:, MEM_WIDTH:].astype(BF16)


def _mem_kv(mem, g, w_b, k_gain):
    b, m, d = mem.shape
    out = jax.ShapeDtypeStruct((b, m, MEM_WIDTH), BF16)
    return pl.pallas_call(
        _mem_kv_kernel,
        grid=(b,),
        in_specs=[
            pl.BlockSpec((None, m, d), lambda i: (i, 0, 0)),
            pl.BlockSpec((1, d), lambda i: (0, 0)),
            pl.BlockSpec((d, 2 * MEM_WIDTH), lambda i: (0, 0)),
            pl.BlockSpec((1, MEM_HEAD_DIM), lambda i: (0, 0)),
        ],
        out_specs=[pl.BlockSpec((None, m, MEM_WIDTH), lambda i: (i, 0, 0))] * 2,
        out_shape=[out, out],
        compiler_params=pltpu.CompilerParams(
            dimension_semantics=("parallel",), vmem_limit_bytes=VMEM_LIMIT),
        name="mem_kv",
    )(mem, g, w_b, k_gain)


_Q0, _K0, _V0 = 0, NA_WIDTH, 2 * NA_WIDTH
_H0 = 3 * NA_WIDTH
_B0 = _H0 + CONV_WIDTH
_C0 = _B0 + CONV_WIDTH
_QM0 = _C0 + CONV_WIDTH
_PROJ_CHUNK = 512


def _in_proj_kernel(x_ref, g_ref, w_ref, qg_ref, kg_ref, mqg_ref,
                    q_ref, k_ref, v_ref, p_ref, b_ref, qm_ref):
    n = _rms(x_ref[...], g_ref[...]).astype(BF16)

    def proj(c0):
        return jnp.dot(n, w_ref[:, c0:c0 + _PROJ_CHUNK], preferred_element_type=F32)

    for base, gain_ref, out_ref in ((_Q0, qg_ref, q_ref), (_K0, kg_ref, k_ref)):
        for cc in range(NA_WIDTH // _PROJ_CHUNK):
            z = proj(base + cc * _PROJ_CHUNK)
            for s in range(_PROJ_CHUNK // LANES):
                o = cc * _PROJ_CHUNK + s * LANES
                out_ref[:, o:o + LANES] = _pair_rms(
                    z[:, s * LANES:(s + 1) * LANES], gain_ref[...]).astype(BF16)
    for cc in range(NA_WIDTH // _PROJ_CHUNK):
        o = cc * _PROJ_CHUNK
        v_ref[:, o:o + _PROJ_CHUNK] = proj(_V0 + o).astype(BF16)
    p_ref[...] = (proj(_C0) * proj(_H0)).astype(BF16)
    b_ref[...] = proj(_B0).astype(BF16)
    z = proj(_QM0)
    for h in range(MEM_HEADS):
        c = slice(h * MEM_HEAD_DIM, (h + 1) * MEM_HEAD_DIM)
        qm_ref[:, c] = _rms(z[:, c], mqg_ref[...]).astype(BF16)


def _in_proj(x, g, w_b, q_gain2, k_gain2, mq_gain, tm):
    b, t, d = x.shape
    nt = t // tm
    row = lambda w: pl.BlockSpec((None, tm, w), lambda bi, i: (bi, i, 0))
    const = lambda shape: pl.BlockSpec(shape, lambda bi, i: (0,) * len(shape))
    sds = lambda w: jax.ShapeDtypeStruct((b, t, w), BF16)
    return pl.pallas_call(
        _in_proj_kernel,
        grid=(b, nt),
        in_specs=[row(d), const((1, d)), const(w_b.shape),
                  const((1, LANES)), const((1, LANES)), const((1, MEM_HEAD_DIM))],
        out_specs=[row(NA_WIDTH)] * 3 + [row(CONV_WIDTH)] * 2 + [row(MEM_WIDTH)],
        out_shape=[sds(NA_WIDTH)] * 3 + [sds(CONV_WIDTH)] * 2 + [sds(MEM_WIDTH)],
        compiler_params=pltpu.CompilerParams(
            dimension_semantics=("parallel", "parallel"), vmem_limit_bytes=VMEM_LIMIT),
        name="in_proj",
    )(x, g, w_b, q_gain2, k_gain2, mq_gain)


COL_BLOCKS = GRID_W // WIN_W
Q_BLOCKS = GRID_W // WIN_W
SLAB = 2 * WIN_W
_NEEDED_COL_BLOCKS = tuple(
    tuple(cb for cb in range(COL_BLOCKS)
          if any(max(0, min(q - WIN_W // 2, GRID_W - WIN_W)) < (cb + 1) * WIN_W
                 and max(0, min(q - WIN_W // 2, GRID_W - WIN_W)) + WIN_W > cb * WIN_W
                 for q in range(j * WIN_W, (j + 1) * WIN_W)))
    for j in range(Q_BLOCKS))


def _bias_table(rel_bias):
    h = rel_bias.shape[0]
    n_dc = 2 * WIN_W - 1
    kc = np.arange(GRID_W)[:, None]
    qc = np.arange(GRID_W)[None, :]
    q_cs = np.clip(qc - WIN_W // 2, 0, GRID_W - WIN_W)
    valid = (kc >= q_cs) & (kc < q_cs + WIN_W)
    select = (((kc - qc + WIN_W - 1)[None] == np.arange(n_dc)[:, None, None]) & valid[None])
    select = select.astype(np.float32).reshape(n_dc, COL_BLOCKS, WIN_W, GRID_W)
    mask = np.where(valid, 0.0, NEG_INF).astype(np.float32).reshape(COL_BLOCKS, WIN_W, GRID_W)
    rb = rel_bias.astype(F32).reshape(h // 2, 2, N_DR, n_dc)
    tb = jnp.einsum("pedt,tbcq->pbdceq", rb, select, precision=lax.Precision.HIGHEST)
    tb = tb + mask[None, :, None, :, None, :]
    return tb.reshape(h // 2, COL_BLOCKS, N_DR * WIN_W, LANES).astype(BF16)


def _mixers_kernel(q_ref, k_ref, kp_ref, kn_ref, v_ref, vp_ref, vn_ref,
                   p_ref, pp_ref, pn_ref, b_ref, qm_ref, mk_ref, mv_ref, bt_ref, cw_ref,
                   y_ref, kext, vext, pext, s_ref, *, rows, tile_rows):
    i = pl.program_id(1)
    nt = pl.num_programs(1)
    tq = tile_rows * GRID_W
    halo = KEY_ROWS_HALO * GRID_W
    lo = _lo_lanes()

    rho = lax.broadcasted_iota(jnp.int32, (2 * GRID_W, LANES), 0)
    lane = lax.broadcasted_iota(jnp.int32, (2 * GRID_W, LANES), 1)
    target = jnp.where((rho & WIN_W) != 0, GRID_W, 0) + (rho >> (SLAB.bit_length() - 1)) * WIN_W + (rho & (WIN_W - 1))
    onehot = (lane == target).astype(BF16)

    for ext, prev, main, nxt in ((kext, kp_ref, k_ref, kn_ref), (vext, vp_ref, v_ref, vn_ref)):
        ext[0:halo] = prev[...]
        ext[halo:halo + tq] = main[...]
        ext[halo + tq:halo + tq + halo] = nxt[...]

    r0 = i * tile_rows

    def row_body(lr, carry):
        r = r0 + lr
        ws = jnp.clip(r - WIN_H // 2, 0, rows - WIN_H)
        start = pl.multiple_of((ws - r0 + KEY_ROWS_HALO) * GRID_W, GRID_W)
        dr0 = ws - r + (WIN_H - 1)
        qrow = pl.multiple_of(lr * GRID_W, GRID_W)
        brow = pl.multiple_of(dr0 * WIN_W, WIN_W)

        def window(ext, c):
            return jnp.concatenate(
                [ext[pl.ds(pl.multiple_of(start + wr * GRID_W + cb * WIN_W, WIN_W), WIN_W), c]
                 for cb in range(COL_BLOCKS) for wr in range(WIN_H)], axis=0)

        for hp in range(HEAD_PAIRS):
            c = slice(hp * LANES, (hp + 1) * LANES)
            q2 = q_ref[pl.ds(qrow, GRID_W), c]
            zero = jnp.zeros((WIN_W, LANES), BF16)
            pieces = []
            for j in range(Q_BLOCKS):
                qj = q2[j * WIN_W:(j + 1) * WIN_W]
                pieces += [jnp.where(lo, qj, zero), jnp.where(lo, zero, qj)]
            lhs = jnp.concatenate([jnp.concatenate(pieces, axis=0), onehot], axis=1)
            bias = jnp.concatenate(
                [bt_ref[hp, cb, pl.ds(brow, WIN_H * WIN_W), :] for cb in range(COL_BLOCKS)], axis=0)
            rhs = jnp.concatenate([window(kext, c), bias], axis=1)
            s_ref[hp] = lax.dot_general(lhs, rhs, (((1,), (1,)), ((), ())),
                                        preferred_element_type=F32)

        for hp in range(HEAD_PAIRS):
            c = slice(hp * LANES, (hp + 1) * LANES)
            p_rows, inv_l = [], []
            for j in range(Q_BLOCKS):
                rs = slice(j * SLAB, (j + 1) * SLAB)
                need = _NEEDED_COL_BLOCKS[j]
                sb = [s_ref[hp, rs, cb * LANES:(cb + 1) * LANES] for cb in need]
                m = jnp.max(functools.reduce(jnp.maximum, sb), axis=-1, keepdims=True)
                e = [jnp.exp(x - m) for x in sb]
                inv_l.append(1.0 / jnp.sum(functools.reduce(jnp.add, e), axis=-1, keepdims=True))
                zero = jnp.zeros((SLAB, LANES), BF16)
                blocks = [zero] * COL_BLOCKS
                for cb, x in zip(need, e):
                    blocks[cb] = x.astype(BF16)
                p_rows.append(jnp.concatenate(blocks, axis=1))
            o = jnp.dot(jnp.concatenate(p_rows, axis=0), window(vext, c),
                        preferred_element_type=F32)
            for j in range(Q_BLOCKS):
                oj = o[j * SLAB:(j + 1) * SLAB] * inv_l[j]
                y_ref[pl.ds(pl.multiple_of(qrow + j * WIN_W, WIN_W), WIN_W), c] = jnp.where(
                    lo, oj[:WIN_W], oj[WIN_W:]).astype(BF16)
        return carry

    lax.fori_loop(0, tile_rows, row_body, 0)

    pext[0:F32_ROWS] = jnp.where(i > 0, pp_ref[...].astype(F32)[F32_ROWS:], 0.0)
    pext[F32_ROWS:F32_ROWS + tq] = p_ref[...].astype(F32)
    pext[F32_ROWS + tq:2 * F32_ROWS + tq] = jnp.where(
        i < nt - 1, pn_ref[...].astype(F32)[:F32_ROWS], 0.0)
    conv = (cw_ref[0:1] * pext[F32_ROWS - 1:F32_ROWS - 1 + tq]
            + cw_ref[1:2] * pext[F32_ROWS:F32_ROWS + tq]
            + cw_ref[2:3] * pext[F32_ROWS + 1:F32_ROWS + 1 + tq])
    y_ref[:, NA_WIDTH:NA_WIDTH + CONV_WIDTH] = (b_ref[...].astype(F32) * conv).astype(BF16)

    scale = 1.0 / math.sqrt(MEM_HEAD_DIM)
    for h in range(MEM_HEADS):
        c = slice(h * MEM_HEAD_DIM, (h + 1) * MEM_HEAD_DIM)
        s = lax.dot_general(qm_ref[:, c], mk_ref[:, c], (((1,), (1,)), ((), ())),
                            preferred_element_type=F32) * scale
        o0 = NA_WIDTH + CONV_WIDTH + h * MEM_HEAD_DIM
        y_ref[:, o0:o0 + MEM_HEAD_DIM] = _softmax_pv(s, mv_ref[:, c]).astype(BF16)


def _mixers(q, k, v, p, bb, qm, mk, mv, bias_tab, conv_w, tile_rows):
    b, t, _ = q.shape
    rows = t // GRID_W
    tq = tile_rows * GRID_W
    nt = t // tq
    halo = KEY_ROWS_HALO * GRID_W
    assert rows >= WIN_H and rows % tile_rows == 0 and tile_rows >= KEY_ROWS_HALO
    assert tq % halo == 0 and tq % BF16_ROWS == 0
    n_mem = mk.shape[1]

    def main(w):
        return pl.BlockSpec((None, tq, w), lambda bi, i: (bi, i, 0))

    def prev(blk, w):
        per = tq // blk
        return pl.BlockSpec((None, blk, w), lambda bi, i: (bi, jnp.maximum(i * per - 1, 0), 0))

    def nxt(blk, w):
        per = tq // blk
        last = t // blk - 1
        return pl.BlockSpec((None, blk, w), lambda bi, i: (bi, jnp.minimum((i + 1) * per, last), 0))

    kv_specs = [main(NA_WIDTH), prev(halo, NA_WIDTH), nxt(halo, NA_WIDTH)]
    mem_spec = pl.BlockSpec((None, n_mem, MEM_WIDTH), lambda bi, i: (bi, 0, 0))
    d_mix = NA_WIDTH + CONV_WIDTH + MEM_WIDTH
    return pl.pallas_call(
        functools.partial(_mixers_kernel, rows=rows, tile_rows=tile_rows),
        grid=(b, nt),
        in_specs=[main(NA_WIDTH)] + kv_specs + kv_specs
                 + [main(CONV_WIDTH), prev(BF16_ROWS, CONV_WIDTH), nxt(BF16_ROWS, CONV_WIDTH),
                    main(CONV_WIDTH), main(MEM_WIDTH), mem_spec, mem_spec,
                    pl.BlockSpec(bias_tab.shape, lambda bi, i: (0, 0, 0, 0)),
                    pl.BlockSpec(conv_w.shape, lambda bi, i: (0, 0))],
        out_specs=main(d_mix),
        out_shape=jax.ShapeDtypeStruct((b, t, d_mix), BF16),
        scratch_shapes=[pltpu.VMEM((tq + 2 * halo, NA_WIDTH), BF16),
                        pltpu.VMEM((tq + 2 * halo, NA_WIDTH), BF16),
                        pltpu.VMEM((tq + 2 * F32_ROWS, CONV_WIDTH), F32),
                        pltpu.VMEM((HEAD_PAIRS, 2 * GRID_W, WIN_H * GRID_W), F32)],
        compiler_params=pltpu.CompilerParams(
            dimension_semantics=("parallel", "parallel"), vmem_limit_bytes=VMEM_LIMIT),
        name="mixers",
    )(q, k, k, k, v, v, v, p, p, p, bb, qm, mk, mv, bias_tab, conv_w)


def _out_proj_kernel(x_ref, y_ref, w_ref, o_ref):
    o_ref[...] = x_ref[...] + jnp.dot(y_ref[...], w_ref[...], preferred_element_type=F32)


def _out_proj(x, y, w_b, tm):
    b, t, d = x.shape
    row = lambda w: pl.BlockSpec((None, tm, w), lambda bi, i: (bi, i, 0))
    return pl.pallas_call(
        _out_proj_kernel,
        grid=(b, t // tm),
        in_specs=[row(d), row(y.shape[-1]), pl.BlockSpec(w_b.shape, lambda bi, i: (0, 0))],
        out_specs=row(d),
        out_shape=jax.ShapeDtypeStruct((b, t, d), F32),
        compiler_params=pltpu.CompilerParams(
            dimension_semantics=("parallel", "parallel"), vmem_limit_bytes=VMEM_LIMIT),
        name="out_proj",
    )(x, y, w_b)


def _ffn_kernel(x_ref, xp_ref, xn_ref, g_ref, wa_ref, wg_ref, cwa_ref, cwg_ref, wo_ref,
                o_ref, next_ref, pa_ref, pg_ref, *, tm):
    i = pl.program_id(1)
    j = pl.program_id(2)
    nt = pl.num_programs(1)
    h0 = BF16_ROWS

    @pl.when(j == 0)
    def _():
        g = g_ref[...]
        zeros = jnp.zeros((F32_ROWS, x_ref.shape[-1]), F32)
        prev = jnp.where(i > 0, _rms(xp_ref[...], g), 0.0)
        nxt = jnp.where(i < nt - 1, _rms(xn_ref[...], g), 0.0)
        next_ref[0:h0] = jnp.concatenate([zeros, prev], axis=0).astype(BF16)
        next_ref[h0:h0 + tm] = _rms(x_ref[...], g).astype(BF16)
        next_ref[h0 + tm:2 * h0 + tm] = jnp.concatenate([nxt, zeros], axis=0).astype(BF16)
        o_ref[...] = x_ref[...]

    n2 = next_ref[...]
    pa_ref[...] = jnp.dot(n2, wa_ref[...], preferred_element_type=F32)
    pg_ref[...] = jnp.dot(n2, wg_ref[...], preferred_element_type=F32)

    def conv(ref, cw):
        return (cw[0:1] * ref[h0 - 1:h0 - 1 + tm] + cw[1:2] * ref[h0:h0 + tm]
                + cw[2:3] * ref[h0 + 1:h0 + 1 + tm])

    hidden = (jax.nn.silu(conv(pa_ref, cwa_ref)) * conv(pg_ref, cwg_ref)).astype(BF16)
    o_ref[...] += jnp.dot(hidden, wo_ref[...], preferred_element_type=F32)


def _ffn(x1, g, w_in_b, conv_w, w_out_b, tm, tf):
    b, t, d = x1.shape
    d_ff = w_out_b.shape[0]
    nj = d_ff // tf
    assert d_ff % tf == 0 and t % tm == 0 and tm % F32_ROWS == 0
    per = tm // F32_ROWS
    last = t // F32_ROWS - 1
    return pl.pallas_call(
        functools.partial(_ffn_kernel, tm=tm),
        grid=(b, t // tm, nj),
        in_specs=[
            pl.BlockSpec((None, tm, d), lambda bi, i, j: (bi, i, 0)),
            pl.BlockSpec((None, F32_ROWS, d), lambda bi, i, j: (bi, jnp.maximum(i * per - 1, 0), 0)),
            pl.BlockSpec((None, F32_ROWS, d),
                         lambda bi, i, j: (bi, jnp.minimum((i + 1) * per, last), 0)),
            pl.BlockSpec((1, d), lambda bi, i, j: (0, 0)),
            pl.BlockSpec((d, tf), lambda bi, i, j: (0, j)),
            pl.BlockSpec((d, tf), lambda bi, i, j: (0, nj + j)),
            pl.BlockSpec((3, tf), lambda bi, i, j: (0, j)),
            pl.BlockSpec((3, tf), lambda bi, i, j: (0, nj + j)),
            pl.BlockSpec((tf, d), lambda bi, i, j: (j, 0)),
        ],
        out_specs=pl.BlockSpec((None, tm, d), lambda bi, i, j: (bi, i, 0)),
        out_shape=jax.ShapeDtypeStruct((b, t, d), F32),
        scratch_shapes=[pltpu.VMEM((tm + 2 * BF16_ROWS, d), BF16),
                        pltpu.VMEM((tm + 2 * BF16_ROWS, tf), F32),
                        pltpu.VMEM((tm + 2 * BF16_ROWS, tf), F32)],
        compiler_params=pltpu.CompilerParams(
            dimension_semantics=("parallel", "parallel", "arbitrary"),
            vmem_limit_bytes=VMEM_LIMIT),
        name="ffn",
    )(x1, x1, x1, g, w_in_b, w_in_b, conv_w, conv_w, w_out_b)


_TM_PROJ = 512
_TILE_ROWS = 8
_TM_FFN = 512
_TF_FFN = 512


def _layer(x, mem, p):
    mk, mv = _mem_kv(mem, p["mem_norm_g"], p["w_mem_kv"], p["mem_k_gain"])
    q, k, v, ch, bb, qm = _in_proj(x, p["g_mix"], p["w_in"], p["na_q_gain"], p["na_k_gain"],
                                   p["mem_q_gain"], _TM_PROJ)
    y = _mixers(q, k, v, ch, bb, qm, mk, mv, p["bias_tab"], p["conv_w"], _TILE_ROWS)
    x1 = _out_proj(x, y, p["w_out"], _TM_PROJ)
    return _ffn(x1, p["g_ffn"], p["w_ffn_in"], p["ffn_conv_w"], p["w_ffn_out"], _TM_FFN, _TF_FFN)


def kernel(x_prompt, x_sample, mem_prompt, mem_sample, g_mix, w_in, na_q_gain, na_k_gain,
           na_rel_bias, conv_w, mem_norm_g, w_mem_kv, mem_q_gain, mem_k_gain, w_out,
           g_ffn, w_ffn_in, ffn_conv_w, w_ffn_out):
    y_prompt, y_sample = x_prompt, x_sample
    for l in range(g_mix.shape[0]):
        row = lambda a: a[l].reshape(1, -1).astype(F32)
        q_scale = 1.0 / math.sqrt(HEAD_DIM)
        p = dict(
            g_mix=row(g_mix), g_ffn=row(g_ffn), mem_norm_g=row(mem_norm_g),
            na_q_gain=jnp.tile(row(na_q_gain), (1, 2)) * q_scale,
            na_k_gain=jnp.tile(row(na_k_gain), (1, 2)),
            mem_q_gain=row(mem_q_gain), mem_k_gain=row(mem_k_gain),
            w_in=w_in[l].astype(BF16), w_mem_kv=w_mem_kv[l].astype(BF16),
            w_out=w_out[l].astype(BF16), w_ffn_in=w_ffn_in[l].astype(BF16),
            w_ffn_out=w_ffn_out[l].astype(BF16),
            conv_w=conv_w[l].astype(F32), ffn_conv_w=ffn_conv_w[l].astype(F32),
            bias_tab=_bias_table(na_rel_bias[l]),
        )
        y_prompt = _layer(y_prompt, mem_prompt, p)
        y_sample = _layer(y_sample, mem_sample, p)
    return (y_prompt, y_sample)
```

```python
import functools
import math

import jax
import jax.numpy as jnp
import numpy as np
from jax import lax
from jax.experimental import pallas as pl
from jax.experimental.pallas import tpu as pltpu

F32 = jnp.float32
BF16 = jnp.bfloat16

GRID_W = 64
WIN_H = 8
WIN_W = 16
HEAD_DIM = 64
NA_HEADS = 16
NA_WIDTH = NA_HEADS * HEAD_DIM
CONV_WIDTH = 512
MEM_HEADS = 4
MEM_HEAD_DIM = 128
MEM_WIDTH = MEM_HEADS * MEM_HEAD_DIM
EPS = 1e-6
NEG_INF = -1e30

LANES = 128
BF16_ROWS = 16
F32_ROWS = 8
VMEM_LIMIT = 56 * 1024 * 1024

HEAD_PAIRS = NA_HEADS // 2
KEY_ROWS_HALO = WIN_H // 2
N_DR = 2 * WIN_H - 1


def _rms(x, g):
    return x * lax.rsqrt(jnp.mean(x * x, axis=-1, keepdims=True) + EPS) * g


def _lo_lanes():
    return lax.broadcasted_iota(jnp.int32, (1, LANES), 1) < HEAD_DIM


def _pair_rms(z, g2):
    lo = _lo_lanes()
    sq = z * z
    tot = jnp.sum(sq, axis=-1, keepdims=True)
    s0 = jnp.sum(jnp.where(lo, sq, 0.0), axis=-1, keepdims=True)
    ms = jnp.where(lo, s0, tot - s0) * (1.0 / HEAD_DIM)
    return z * lax.rsqrt(ms + EPS) * g2


def _softmax_pv(s, v):
    m = jnp.max(s, axis=-1, keepdims=True)
    e = jnp.exp(s - m)
    l = jnp.sum(e, axis=-1, keepdims=True)
    o = jnp.dot(e.astype(BF16), v, preferred_element_type=F32)
    return o * (1.0 / l)


def _mem_kv_kernel(mem_ref, g_ref, w_ref, kg_ref, mk_ref, mv_ref):
    n = _rms(mem_ref[...], g_ref[...]).astype(BF16)
    z = jnp.dot(n, w_ref[...], preferred_element_type=F32)
    for h in range(MEM_HEADS):
        c = slice(h * MEM_HEAD_DIM, (h + 1) * MEM_HEAD_DIM)
        mk_ref[:, c] = _rms(z[:, c], kg_ref[...]).astype(BF16)
    mv_ref[...] = z[:, MEM_WIDTH:].astype(BF16)


def _mem_kv(mem, g, w_b, k_gain):
    b, m, d = mem.shape
    out = jax.ShapeDtypeStruct((b, m, MEM_WIDTH), BF16)
    return pl.pallas_call(
        _mem_kv_kernel,
        grid=(b,),
        in_specs=[
            pl.BlockSpec((None, m, d), lambda i: (i, 0, 0)),
            pl.BlockSpec((1, d), lambda i: (0, 0)),
            pl.BlockSpec((d, 2 * MEM_WIDTH), lambda i: (0, 0)),
            pl.BlockSpec((1, MEM_HEAD_DIM), lambda i: (0, 0)),
        ],
        out_specs=[pl.BlockSpec((None, m, MEM_WIDTH), lambda i: (i, 0, 0))] * 2,
        out_shape=[out, out],
        compiler_params=pltpu.CompilerParams(
            dimension_semantics=("parallel",), vmem_limit_bytes=VMEM_LIMIT),
        name="mem_kv",
    )(mem, g, w_b, k_gain)


_Q0, _K0, _V0 = 0, NA_WIDTH, 2 * NA_WIDTH
_H0 = 3 * NA_WIDTH
_B0 = _H0 + CONV_WIDTH
_C0 = _B0 + CONV_WIDTH
_QM0 = _C0 + CONV_WIDTH
_PROJ_CHUNK = 512


def _in_proj_kernel(x_ref, g_ref, w_ref, qg_ref, kg_ref, mqg_ref,
                    q_ref, k_ref, v_ref, p_ref, b_ref, qm_ref):
    n = _rms(x_ref[...], g_ref[...]).astype(BF16)

    def proj(c0):
        return jnp.dot(n, w_ref[:, c0:c0 + _PROJ_CHUNK], preferred_element_type=F32)

    for base, gain_ref, out_ref in ((_Q0, qg_ref, q_ref), (_K0, kg_ref, k_ref)):
        for cc in range(NA_WIDTH // _PROJ_CHUNK):
            z = proj(base + cc * _PROJ_CHUNK)
            for s in range(_PROJ_CHUNK // LANES):
                o = cc * _PROJ_CHUNK + s * LANES
                out_ref[:, o:o + LANES] = _pair_rms(
                    z[:, s * LANES:(s + 1) * LANES], gain_ref[...]).astype(BF16)
    for cc in range(NA_WIDTH // _PROJ_CHUNK):
        o = cc * _PROJ_CHUNK
        v_ref[:, o:o + _PROJ_CHUNK] = proj(_V0 + o).astype(BF16)
    p_ref[...] = (proj(_C0) * proj(_H0)).astype(BF16)
    b_ref[...] = proj(_B0).astype(BF16)
    z = proj(_QM0)
    for h in range(MEM_HEADS):
        c = slice(h * MEM_HEAD_DIM, (h + 1) * MEM_HEAD_DIM)
        qm_ref[:, c] = _rms(z[:, c], mqg_ref[...]).astype(BF16)


def _in_proj(x, g, w_b, q_gain2, k_gain2, mq_gain, tm):
    b, t, d = x.shape
    nt = t // tm
    row = lambda w: pl.BlockSpec((None, tm, w), lambda bi, i: (bi, i, 0))
    const = lambda shape: pl.BlockSpec(shape, lambda bi, i: (0,) * len(shape))
    sds = lambda w: jax.ShapeDtypeStruct((b, t, w), BF16)
    return pl.pallas_call(
        _in_proj_kernel,
        grid=(b, nt),
        in_specs=[row(d), const((1, d)), const(w_b.shape),
                  const((1, LANES)), const((1, LANES)), const((1, MEM_HEAD_DIM))],
        out_specs=[row(NA_WIDTH)] * 3 + [row(CONV_WIDTH)] * 2 + [row(MEM_WIDTH)],
        out_shape=[sds(NA_WIDTH)] * 3 + [sds(CONV_WIDTH)] * 2 + [sds(MEM_WIDTH)],
        compiler_params=pltpu.CompilerParams(
            dimension_semantics=("parallel", "parallel"), vmem_limit_bytes=VMEM_LIMIT),
        name="in_proj",
    )(x, g, w_b, q_gain2, k_gain2, mq_gain)


COL_BLOCKS = GRID_W // WIN_W
Q_BLOCKS = GRID_W // WIN_W
SLAB = 2 * WIN_W
_NEEDED_COL_BLOCKS = tuple(
    tuple(cb for cb in range(COL_BLOCKS)
          if any(max(0, min(q - WIN_W // 2, GRID_W - WIN_W)) < (cb + 1) * WIN_W
                 and max(0, min(q - WIN_W // 2, GRID_W - WIN_W)) + WIN_W > cb * WIN_W
                 for q in range(j * WIN_W, (j + 1) * WIN_W)))
    for j in range(Q_BLOCKS))


def _bias_table(rel_bias):
    h = rel_bias.shape[0]
    n_dc = 2 * WIN_W - 1
    kc = np.arange(GRID_W)[:, None]
    qc = np.arange(GRID_W)[None, :]
    q_cs = np.clip(qc - WIN_W // 2, 0, GRID_W - WIN_W)
    valid = (kc >= q_cs) & (kc < q_cs + WIN_W)
    select = (((kc - qc + WIN_W - 1)[None] == np.arange(n_dc)[:, None, None]) & valid[None])
    select = select.astype(np.float32).reshape(n_dc, COL_BLOCKS, WIN_W, GRID_W)
    mask = np.where(valid, 0.0, NEG_INF).astype(np.float32).reshape(COL_BLOCKS, WIN_W, GRID_W)
    rb = rel_bias.astype(F32).reshape(h // 2, 2, N_DR, n_dc)
    tb = jnp.einsum("pedt,tbcq->pbdceq", rb, select, precision=lax.Precision.HIGHEST)
    tb = tb + mask[None, :, None, :, None, :]
    return tb.reshape(h // 2, COL_BLOCKS, N_DR * WIN_W, LANES).astype(BF16)


def _mixers_kernel(q_ref, k_ref, kp_ref, kn_ref, v_ref, vp_ref, vn_ref,
                   p_ref, pp_ref, pn_ref, b_ref, qm_ref, mk_ref, mv_ref, bt_ref, cw_ref,
                   y_ref, kext, vext, pext, s0_ref, s1_ref, *, rows, tile_rows):
    i = pl.program_id(1)
    nt = pl.num_programs(1)
    tq = tile_rows * GRID_W
    halo = KEY_ROWS_HALO * GRID_W
    lo = _lo_lanes()

    rho = lax.broadcasted_iota(jnp.int32, (2 * GRID_W, LANES), 0)
    lane = lax.broadcasted_iota(jnp.int32, (2 * GRID_W, LANES), 1)
    target = jnp.where((rho & WIN_W) != 0, GRID_W, 0) + (rho >> (SLAB.bit_length() - 1)) * WIN_W + (rho & (WIN_W - 1))
    onehot = (lane == target).astype(BF16)

    for ext, prev, main, nxt in ((kext, kp_ref, k_ref, kn_ref), (vext, vp_ref, v_ref, vn_ref)):
        ext[0:halo] = prev[...]
        ext[halo:halo + tq] = main[...]
        ext[halo + tq:halo + tq + halo] = nxt[...]

    r0 = i * tile_rows

    def window(ext, start, c):
        return jnp.concatenate(
            [ext[pl.ds(pl.multiple_of(start + wr * GRID_W + cb * WIN_W, WIN_W), WIN_W), c]
             for cb in range(COL_BLOCKS) for wr in range(WIN_H)], axis=0)

    def row_geometry(lr):
        r = r0 + lr
        ws = jnp.clip(r - WIN_H // 2, 0, rows - WIN_H)
        start = pl.multiple_of((ws - r0 + KEY_ROWS_HALO) * GRID_W, GRID_W)
        dr0 = ws - r + (WIN_H - 1)
        return start, pl.multiple_of(lr * GRID_W, GRID_W), pl.multiple_of(dr0 * WIN_W, WIN_W)

    def scores(lr, s_ref):
        start, qrow, brow = row_geometry(lr)
        for hp in range(HEAD_PAIRS):
            c = slice(hp * LANES, (hp + 1) * LANES)
            q2 = q_ref[pl.ds(qrow, GRID_W), c]
            zero = jnp.zeros((WIN_W, LANES), BF16)
            pieces = []
            for j in range(Q_BLOCKS):
                qj = q2[j * WIN_W:(j + 1) * WIN_W]
                pieces += [jnp.where(lo, qj, zero), jnp.where(lo, zero, qj)]
            lhs = jnp.concatenate([jnp.concatenate(pieces, axis=0), onehot], axis=1)
            bias = jnp.concatenate(
                [bt_ref[hp, cb, pl.ds(brow, WIN_H * WIN_W), :] for cb in range(COL_BLOCKS)], axis=0)
            rhs = jnp.concatenate([window(kext, start, c), bias], axis=1)
            s_ref[hp] = lax.dot_general(lhs, rhs, (((1,), (1,)), ((), ())),
                                        preferred_element_type=F32)

    def attend(lr, s_ref):
        start, qrow, _ = row_geometry(lr)
        for hp in range(HEAD_PAIRS):
            c = slice(hp * LANES, (hp + 1) * LANES)
            p_rows, inv_l = [], []
            for j in range(Q_BLOCKS):
                rs = slice(j * SLAB, (j + 1) * SLAB)
                need = _NEEDED_COL_BLOCKS[j]
                sb = [s_ref[hp, rs, cb * LANES:(cb + 1) * LANES] for cb in need]
                m = jnp.max(functools.reduce(jnp.maximum, sb), axis=-1, keepdims=True)
                e = [jnp.exp(x - m) for x in sb]
                inv_l.append(1.0 / jnp.sum(functools.reduce(jnp.add, e), axis=-1, keepdims=True))
                zero = jnp.zeros((SLAB, LANES), BF16)
                blocks = [zero] * COL_BLOCKS
                for cb, x in zip(need, e):
                    blocks[cb] = x.astype(BF16)
                p_rows.append(jnp.concatenate(blocks, axis=1))
            o = jnp.dot(jnp.concatenate(p_rows, axis=0), window(vext, start, c),
                        preferred_element_type=F32)
            for j in range(Q_BLOCKS):
                oj = o[j * SLAB:(j + 1) * SLAB] * inv_l[j]
                y_ref[pl.ds(pl.multiple_of(qrow + j * WIN_W, WIN_W), WIN_W), c] = jnp.where(
                    lo, oj[:WIN_W], oj[WIN_W:]).astype(BF16)

    scores(0, s0_ref)

    def row_pair(k, carry):
        lr = 2 * k
        scores(lr + 1, s1_ref)
        attend(lr, s0_ref)
        scores(lr + 2, s0_ref)
        attend(lr + 1, s1_ref)
        return carry

    lax.fori_loop(0, tile_rows // 2 - 1, row_pair, 0)
    scores(tile_rows - 1, s1_ref)
    attend(tile_rows - 2, s0_ref)
    attend(tile_rows - 1, s1_ref)

    pext[0:F32_ROWS] = jnp.where(i > 0, pp_ref[...].astype(F32)[F32_ROWS:], 0.0)
    pext[F32_ROWS:F32_ROWS + tq] = p_ref[...].astype(F32)
    pext[F32_ROWS + tq:2 * F32_ROWS + tq] = jnp.where(
        i < nt - 1, pn_ref[...].astype(F32)[:F32_ROWS], 0.0)
    conv = (cw_ref[0:1] * pext[F32_ROWS - 1:F32_ROWS - 1 + tq]
            + cw_ref[1:2] * pext[F32_ROWS:F32_ROWS + tq]
            + cw_ref[2:3] * pext[F32_ROWS + 1:F32_ROWS + 1 + tq])
    y_ref[:, NA_WIDTH:NA_WIDTH + CONV_WIDTH] = (b_ref[...].astype(F32) * conv).astype(BF16)

    scale = 1.0 / math.sqrt(MEM_HEAD_DIM)
    for h in range(MEM_HEADS):
        c = slice(h * MEM_HEAD_DIM, (h + 1) * MEM_HEAD_DIM)
        s = lax.dot_general(qm_ref[:, c], mk_ref[:, c], (((1,), (1,)), ((), ())),
                            preferred_element_type=F32) * scale
        o0 = NA_WIDTH + CONV_WIDTH + h * MEM_HEAD_DIM
        y_ref[:, o0:o0 + MEM_HEAD_DIM] = _softmax_pv(s, mv_ref[:, c]).astype(BF16)


def _mixers(q, k, v, p, bb, qm, mk, mv, bias_tab, conv_w, tile_rows):
    b, t, _ = q.shape
    rows = t // GRID_W
    tq = tile_rows * GRID_W
    nt = t // tq
    halo = KEY_ROWS_HALO * GRID_W
    assert rows >= WIN_H and rows % tile_rows == 0 and tile_rows >= KEY_ROWS_HALO
    assert tile_rows % 2 == 0
    assert tq % halo == 0 and tq % BF16_ROWS == 0
    n_mem = mk.shape[1]

    def main(w):
        return pl.BlockSpec((None, tq, w), lambda bi, i: (bi, i, 0))

    def prev(blk, w):
        per = tq // blk
        return pl.BlockSpec((None, blk, w), lambda bi, i: (bi, jnp.maximum(i * per - 1, 0), 0))

    def nxt(blk, w):
        per = tq // blk
        last = t // blk - 1
        return pl.BlockSpec((None, blk, w), lambda bi, i: (bi, jnp.minimum((i + 1) * per, last), 0))

    kv_specs = [main(NA_WIDTH), prev(halo, NA_WIDTH), nxt(halo, NA_WIDTH)]
    mem_spec = pl.BlockSpec((None, n_mem, MEM_WIDTH), lambda bi, i: (bi, 0, 0))
    d_mix = NA_WIDTH + CONV_WIDTH + MEM_WIDTH
    return pl.pallas_call(
        functools.partial(_mixers_kernel, rows=rows, tile_rows=tile_rows),
        grid=(b, nt),
        in_specs=[main(NA_WIDTH)] + kv_specs + kv_specs
                 + [main(CONV_WIDTH), prev(BF16_ROWS, CONV_WIDTH), nxt(BF16_ROWS, CONV_WIDTH),
                    main(CONV_WIDTH), main(MEM_WIDTH), mem_spec, mem_spec,
                    pl.BlockSpec(bias_tab.shape, lambda bi, i: (0, 0, 0, 0)),
                    pl.BlockSpec(conv_w.shape, lambda bi, i: (0, 0))],
        out_specs=main(d_mix),
        out_shape=jax.ShapeDtypeStruct((b, t, d_mix), BF16),
        scratch_shapes=[pltpu.VMEM((tq + 2 * halo, NA_WIDTH), BF16),
                        pltpu.VMEM((tq + 2 * halo, NA_WIDTH), BF16),
                        pltpu.VMEM((tq + 2 * F32_ROWS, CONV_WIDTH), F32),
                        pltpu.VMEM((HEAD_PAIRS, 2 * GRID_W, WIN_H * GRID_W), F32),
                        pltpu.VMEM((HEAD_PAIRS, 2 * GRID_W, WIN_H * GRID_W), F32)],
        compiler_params=pltpu.CompilerParams(
            dimension_semantics=("parallel", "parallel"), vmem_limit_bytes=VMEM_LIMIT),
        name="mixers",
    )(q, k, k, k, v, v, v, p, p, p, bb, qm, mk, mv, bias_tab, conv_w)


def _out_proj_kernel(x_ref, y_ref, w_ref, o_ref):
    o_ref[...] = x_ref[...] + jnp.dot(y_ref[...], w_ref[...], preferred_element_type=F32)


def _out_proj(x, y, w_b, tm):
    b, t, d = x.shape
    row = lambda w: pl.BlockSpec((None, tm, w), lambda bi, i: (bi, i, 0))
    return pl.pallas_call(
        _out_proj_kernel,
        grid=(b, t // tm),
        in_specs=[row(d), row(y.shape[-1]), pl.BlockSpec(w_b.shape, lambda bi, i: (0, 0))],
        out_specs=row(d),
        out_shape=jax.ShapeDtypeStruct((b, t, d), F32),
        compiler_params=pltpu.CompilerParams(
            dimension_semantics=("parallel", "parallel"), vmem_limit_bytes=VMEM_LIMIT),
        name="out_proj",
    )(x, y, w_b)


def _ffn_kernel(x_ref, xp_ref, xn_ref, g_ref, wa_ref, wg_ref, cwa_ref, cwg_ref, wo_ref,
                o_ref, next_ref, pa_ref, pg_ref, *, tm):
    i = pl.program_id(1)
    j = pl.program_id(2)
    nt = pl.num_programs(1)
    h0 = BF16_ROWS

    @pl.when(j == 0)
    def _():
        g = g_ref[...]
        zeros = jnp.zeros((F32_ROWS, x_ref.shape[-1]), F32)
        prev = jnp.where(i > 0, _rms(xp_ref[...], g), 0.0)
        nxt = jnp.where(i < nt - 1, _rms(xn_ref[...], g), 0.0)
        next_ref[0:h0] = jnp.concatenate([zeros, prev], axis=0).astype(BF16)
        next_ref[h0:h0 + tm] = _rms(x_ref[...], g).astype(BF16)
        next_ref[h0 + tm:2 * h0 + tm] = jnp.concatenate([nxt, zeros], axis=0).astype(BF16)
        o_ref[...] = x_ref[...]

    n2 = next_ref[...]
    pa_ref[...] = jnp.dot(n2, wa_ref[...], preferred_element_type=F32)
    pg_ref[...] = jnp.dot(n2, wg_ref[...], preferred_element_type=F32)

    def conv(ref, cw):
        return (cw[0:1] * ref[h0 - 1:h0 - 1 + tm] + cw[1:2] * ref[h0:h0 + tm]
                + cw[2:3] * ref[h0 + 1:h0 + 1 + tm])

    hidden = (jax.nn.silu(conv(pa_ref, cwa_ref)) * conv(pg_ref, cwg_ref)).astype(BF16)
    o_ref[...] += jnp.dot(hidden, wo_ref[...], preferred_element_type=F32)


def _ffn(x1, g, w_in_b, conv_w, w_out_b, tm, tf):
    b, t, d = x1.shape
    d_ff = w_out_b.shape[0]
    nj = d_ff // tf
    assert d_ff % tf == 0 and t % tm == 0 and tm % F32_ROWS == 0
    per = tm // F32_ROWS
    last = t // F32_ROWS - 1
    return pl.pallas_call(
        functools.partial(_ffn_kernel, tm=tm),
        grid=(b, t // tm, nj),
        in_specs=[
            pl.BlockSpec((None, tm, d), lambda bi, i, j: (bi, i, 0)),
            pl.BlockSpec((None, F32_ROWS, d), lambda bi, i, j: (bi, jnp.maximum(i * per - 1, 0), 0)),
            pl.BlockSpec((None, F32_ROWS, d),
                         lambda bi, i, j: (bi, jnp.minimum((i + 1) * per, last), 0)),
            pl.BlockSpec((1, d), lambda bi, i, j: (0, 0)),
            pl.BlockSpec((d, tf), lambda bi, i, j: (0, j)),
            pl.BlockSpec((d, tf), lambda bi, i, j: (0, nj + j)),
            pl.BlockSpec((3, tf), lambda bi, i, j: (0, j)),
            pl.BlockSpec((3, tf), lambda bi, i, j: (0, nj + j)),
            pl.BlockSpec((tf, d), lambda bi, i, j: (j, 0)),
        ],
        out_specs=pl.BlockSpec((None, tm, d), lambda bi, i, j: (bi, i, 0)),
        out_shape=jax.ShapeDtypeStruct((b, t, d), F32),
        scratch_shapes=[pltpu.VMEM((tm + 2 * BF16_ROWS, d), BF16),
                        pltpu.VMEM((tm + 2 * BF16_ROWS, tf), F32),
                        pltpu.VMEM((tm + 2 * BF16_ROWS, tf), F32)],
        compiler_params=pltpu.CompilerParams(
            dimension_semantics=("parallel", "parallel", "arbitrary"),
            vmem_limit_bytes=VMEM_LIMIT),
        name="ffn",
    )(x1, x1, x1, g, w_in_b, w_in_b, conv_w, conv_w, w_out_b)


_TM_PROJ = 512
_TILE_ROWS = 8
_TM_FFN = 512
_TF_FFN = 512


def _layer(x, mem, p):
    mk, mv = _mem_kv(mem, p["mem_norm_g"], p["w_mem_kv"], p["mem_k_gain"])
    q, k, v, ch, bb, qm = _in_proj(x, p["g_mix"], p["w_in"], p["na_q_gain"], p["na_k_gain"],
                                   p["mem_q_gain"], _TM_PROJ)
    y = _mixers(q, k, v, ch, bb, qm, mk, mv, p["bias_tab"], p["conv_w"], _TILE_ROWS)
    x1 = _out_proj(x, y, p["w_out"], _TM_PROJ)
    return _ffn(x1, p["g_ffn"], p["w_ffn_in"], p["ffn_conv_w"], p["w_ffn_out"], _TM_FFN, _TF_FFN)


def kernel(x_prompt, x_sample, mem_prompt, mem_sample, g_mix, w_in, na_q_gain, na_k_gain,
           na_rel_bias, conv_w, mem_norm_g, w_mem_kv, mem_q_gain, mem_k_gain, w_out,
           g_ffn, w_ffn_in, ffn_conv_w, w_ffn_out):
    y_prompt, y_sample = x_prompt, x_sample
    for l in range(g_mix.shape[0]):
        row = lambda a: a[l].reshape(1, -1).astype(F32)
        q_scale = 1.0 / math.sqrt(HEAD_DIM)
        p = dict(
            g_mix=row(g_mix), g_ffn=row(g_ffn), mem_norm_g=row(mem_norm_g),
            na_q_gain=jnp.tile(row(na_q_gain), (1, 2)) * q_scale,
            na_k_gain=jnp.tile(row(na_k_gain), (1, 2)),
            mem_q_gain=row(mem_q_gain), mem_k_gain=row(mem_k_gain),
            w_in=w_in[l].astype(BF16), w_mem_kv=w_mem_kv[l].astype(BF16),
            w_out=w_out[l].astype(BF16), w_ffn_in=w_ffn_in[l].astype(BF16),
            w_ffn_out=w_ffn_out[l].astype(BF16),
            conv_w=conv_w[l].astype(F32), ffn_conv_w=ffn_conv_w[l].astype(F32),
            bias_tab=_bias_table(na_rel_bias[l]),
        )
        y_prompt = _layer(y_prompt, mem_prompt, p)
        y_sample = _layer(y_sample, mem_sample, p)
    return (y_prompt, y_sample)
```

```python
import functools
import math

import jax
import jax.numpy as jnp
import numpy as np
from jax import lax
from jax.experimental import pallas as pl
from jax.experimental.pallas import tpu as pltpu

F32 = jnp.float32
BF16 = jnp.bfloat16

GRID_W = 64
WIN_H = 8
WIN_W = 16
HEAD_DIM = 64
NA_HEADS = 16
NA_WIDTH = NA_HEADS * HEAD_DIM
CONV_WIDTH = 512
MEM_HEADS = 4
MEM_HEAD_DIM = 128
MEM_WIDTH = MEM_HEADS * MEM_HEAD_DIM
EPS = 1e-6
NEG_INF = -1e30

LANES = 128
BF16_ROWS = 16
F32_ROWS = 8
VMEM_LIMIT = 56 * 1024 * 1024

HEAD_PAIRS = NA_HEADS // 2
KEY_ROWS_HALO = WIN_H // 2
N_DR = 2 * WIN_H - 1


def _rms(x, g):
    return x * lax.rsqrt(jnp.mean(x * x, axis=-1, keepdims=True) + EPS) * g


def _lo_lanes():
    return lax.broadcasted_iota(jnp.int32, (1, LANES), 1) < HEAD_DIM


def _pair_rms(z, g2):
    lo = _lo_lanes()
    sq = z * z
    tot = jnp.sum(sq, axis=-1, keepdims=True)
    s0 = jnp.sum(jnp.where(lo, sq, 0.0), axis=-1, keepdims=True)
    ms = jnp.where(lo, s0, tot - s0) * (1.0 / HEAD_DIM)
    return z * lax.rsqrt(ms + EPS) * g2


def _softmax_pv(s, v):
    m = jnp.max(s, axis=-1, keepdims=True)
    e = jnp.exp(s - m)
    l = jnp.sum(e, axis=-1, keepdims=True)
    o = jnp.dot(e.astype(BF16), v, preferred_element_type=F32)
    return o * (1.0 / l)


def _mem_kv_kernel(mem_ref, g_ref, w_ref, kg_ref, mk_ref, mv_ref):
    n = _rms(mem_ref[...], g_ref[...]).astype(BF16)
    z = jnp.dot(n, w_ref[...], preferred_element_type=F32)
    for h in range(MEM_HEADS):
        c = slice(h * MEM_HEAD_DIM, (h + 1) * MEM_HEAD_DIM)
        mk_ref[:, c] = _rms(z[:, c], kg_ref[...]).astype(BF16)
    mv_ref[...] = z[:, MEM_WIDTH:].astype(BF16)


def _mem_kv(mem, g, w_b, k_gain):
    b, m, d = mem.shape
    out = jax.ShapeDtypeStruct((b, m, MEM_WIDTH), BF16)
    return pl.pallas_call(
        _mem_kv_kernel,
        grid=(b,),
        in_specs=[
            pl.BlockSpec((None, m, d), lambda i: (i, 0, 0)),
            pl.BlockSpec((1, d), lambda i: (0, 0)),
            pl.BlockSpec((d, 2 * MEM_WIDTH), lambda i: (0, 0)),
            pl.BlockSpec((1, MEM_HEAD_DIM), lambda i: (0, 0)),
        ],
        out_specs=[pl.BlockSpec((None, m, MEM_WIDTH), lambda i: (i, 0, 0))] * 2,
        out_shape=[out, out],
        compiler_params=pltpu.CompilerParams(
            dimension_semantics=("parallel",), vmem_limit_bytes=VMEM_LIMIT),
        name="mem_kv",
    )(mem, g, w_b, k_gain)


_Q0, _K0, _V0 = 0, NA_WIDTH, 2 * NA_WIDTH
_H0 = 3 * NA_WIDTH
_B0 = _H0 + CONV_WIDTH
_C0 = _B0 + CONV_WIDTH
_QM0 = _C0 + CONV_WIDTH
_PROJ_CHUNK = 512


def _in_proj_kernel(x_ref, g_ref, w_ref, qg_ref, kg_ref, mqg_ref,
                    q_ref, k_ref, v_ref, p_ref, b_ref, qm_ref):
    n = _rms(x_ref[...], g_ref[...]).astype(BF16)

    def proj(c0):
        return jnp.dot(n, w_ref[:, c0:c0 + _PROJ_CHUNK], preferred_element_type=F32)

    for base, gain_ref, out_ref in ((_Q0, qg_ref, q_ref), (_K0, kg_ref, k_ref)):
        for cc in range(NA_WIDTH // _PROJ_CHUNK):
            z = proj(base + cc * _PROJ_CHUNK)
            for s in range(_PROJ_CHUNK // LANES):
                o = cc * _PROJ_CHUNK + s * LANES
                out_ref[:, o:o + LANES] = _pair_rms(
                    z[:, s * LANES:(s + 1) * LANES], gain_ref[...]).astype(BF16)
    for cc in range(NA_WIDTH // _PROJ_CHUNK):
        o = cc * _PROJ_CHUNK
        v_ref[:, o:o + _PROJ_CHUNK] = proj(_V0 + o).astype(BF16)
    p_ref[...] = (proj(_C0) * proj(_H0)).astype(BF16)
    b_ref[...] = proj(_B0).astype(BF16)
    z = proj(_QM0)
    for h in range(MEM_HEADS):
        c = slice(h * MEM_HEAD_DIM, (h + 1) * MEM_HEAD_DIM)
        qm_ref[:, c] = _rms(z[:, c], mqg_ref[...]).astype(BF16)


def _in_proj(x, g, w_b, q_gain2, k_gain2, mq_gain, tm):
    b, t, d = x.shape
    nt = t // tm
    row = lambda w: pl.BlockSpec((None, tm, w), lambda bi, i: (bi, i, 0))
    const = lambda shape: pl.BlockSpec(shape, lambda bi, i: (0,) * len(shape))
    sds = lambda w: jax.ShapeDtypeStruct((b, t, w), BF16)
    return pl.pallas_call(
        _in_proj_kernel,
        grid=(b, nt),
        in_specs=[row(d), const((1, d)), const(w_b.shape),
                  const((1, LANES)), const((1, LANES)), const((1, MEM_HEAD_DIM))],
        out_specs=[row(NA_WIDTH)] * 3 + [row(CONV_WIDTH)] * 2 + [row(MEM_WIDTH)],
        out_shape=[sds(NA_WIDTH)] * 3 + [sds(CONV_WIDTH)] * 2 + [sds(MEM_WIDTH)],
        compiler_params=pltpu.CompilerParams(
            dimension_semantics=("parallel", "parallel"), vmem_limit_bytes=VMEM_LIMIT),
        name="in_proj",
    )(x, g, w_b, q_gain2, k_gain2, mq_gain)


COL_BLOCKS = GRID_W // WIN_W
Q_BLOCKS = GRID_W // WIN_W
SLAB = 2 * WIN_W
_NEEDED_COL_BLOCKS = tuple(
    tuple(cb for cb in range(COL_BLOCKS)
          if any(max(0, min(q - WIN_W // 2, GRID_W - WIN_W)) < (cb + 1) * WIN_W
                 and max(0, min(q - WIN_W // 2, GRID_W - WIN_W)) + WIN_W > cb * WIN_W
                 for q in range(j * WIN_W, (j + 1) * WIN_W)))
    for j in range(Q_BLOCKS))


_SCORE_BLOCKS = tuple((j, cb) for j in range(Q_BLOCKS) for cb in _NEEDED_COL_BLOCKS[j])


def _bias_table(rel_bias):
    h = rel_bias.shape[0]
    n_dc = 2 * WIN_W - 1
    cr = np.arange(WIN_W)[:, None]
    cl = np.arange(WIN_W)[None, :]
    select = np.zeros((len(_SCORE_BLOCKS), WIN_W, WIN_W, n_dc), np.float32)
    mask = np.zeros((len(_SCORE_BLOCKS), WIN_W, WIN_W), np.float32)
    for n, (j, cb) in enumerate(_SCORE_BLOCKS):
        q, k = j * WIN_W + cr, cb * WIN_W + cl
        q_cs = np.clip(q - WIN_W // 2, 0, GRID_W - WIN_W)
        valid = (k >= q_cs) & (k < q_cs + WIN_W)
        select[n] = ((k - q + WIN_W - 1)[..., None] == np.arange(n_dc)) & valid[..., None]
        mask[n] = np.where(valid, 0.0, NEG_INF)
    offs = np.arange(WIN_H)[:, None] + np.arange(WIN_H)[None, :]
    rows_sel = (offs[..., None] == np.arange(N_DR)).astype(np.float32)
    rb = rel_bias.astype(F32).reshape(h // 2, 2, N_DR, n_dc)
    win = jnp.einsum("owd,pedt->opewt", rows_sel, rb, precision=lax.Precision.HIGHEST)
    tb = jnp.einsum("opewt,brlt->opberwl", win, select, precision=lax.Precision.HIGHEST)
    tb = tb + mask[None, None, :, None, :, None, :]
    return tb.reshape(WIN_H, h // 2, len(_SCORE_BLOCKS), SLAB, LANES)


def _mixers_kernel(q_ref, k_ref, kp_ref, kn_ref, v_ref, vp_ref, vn_ref,
                   p_ref, pp_ref, pn_ref, b_ref, qm_ref, mk_ref, mv_ref, bt_ref, cw_ref,
                   y_ref, kext, vext, pext, s0_ref, s1_ref, *, rows, tile_rows):
    i = pl.program_id(1)
    nt = pl.num_programs(1)
    tq = tile_rows * GRID_W
    halo = KEY_ROWS_HALO * GRID_W
    lo = _lo_lanes()
    n_hp = 2 * GRID_W

    for ext, prev, main, nxt in ((kext, kp_ref, k_ref, kn_ref), (vext, vp_ref, v_ref, vn_ref)):
        ext[0:halo] = prev[...]
        ext[halo:halo + tq] = main[...]
        ext[halo + tq:halo + tq + halo] = nxt[...]

    r0 = i * tile_rows

    def window(ext, start, c):
        return jnp.concatenate(
            [ext[pl.ds(pl.multiple_of(start + wr * GRID_W + cb * WIN_W, WIN_W), WIN_W), c]
             for cb in range(COL_BLOCKS) for wr in range(WIN_H)], axis=0)

    def row_geometry(lr):
        r = r0 + lr
        ws = jnp.clip(r - WIN_H // 2, 0, rows - WIN_H)
        start = pl.multiple_of((ws - r0 + KEY_ROWS_HALO) * GRID_W, GRID_W)
        dr0 = ws - r + (WIN_H - 1)
        return start, pl.multiple_of(lr * GRID_W, GRID_W), dr0

    def scores(lr, s_ref):
        start, qrow, dr0 = row_geometry(lr)
        zero = jnp.zeros((WIN_W, LANES), BF16)
        for quad in range(HEAD_PAIRS // 2):
            c2 = slice(2 * quad * LANES, (2 * quad + 2) * LANES)
            q4 = q_ref[pl.ds(qrow, GRID_W), c2]
            pieces = []
            for half in range(2):
                for j in range(Q_BLOCKS):
                    qj = q4[j * WIN_W:(j + 1) * WIN_W, half * LANES:(half + 1) * LANES]
                    for piece in (jnp.where(lo, qj, zero), jnp.where(lo, zero, qj)):
                        pieces.append(jnp.concatenate(
                            [piece, zero] if half == 0 else [zero, piece], axis=1))
            s = lax.dot_general(jnp.concatenate(pieces, axis=0), window(kext, start, c2),
                                (((1,), (1,)), ((), ())), preferred_element_type=F32)
            for half in range(2):
                hp = 2 * quad + half
                for n, (j, cb) in enumerate(_SCORE_BLOCKS):
                    rs = slice(j * SLAB, (j + 1) * SLAB)
                    cs = slice(cb * LANES, (cb + 1) * LANES)
                    s_ref[hp, rs, cs] = (s[half * n_hp + j * SLAB:half * n_hp + (j + 1) * SLAB, cs]
                                         + bt_ref[dr0, hp, n])

    def attend(lr, s_ref):
        start, qrow, _ = row_geometry(lr)
        for quad in range(HEAD_PAIRS // 2):
            c2 = slice(2 * quad * LANES, (2 * quad + 2) * LANES)
            p_rows, inv_l = [], []
            for half in range(2):
                hp = 2 * quad + half
                for j in range(Q_BLOCKS):
                    rs = slice(j * SLAB, (j + 1) * SLAB)
                    need = _NEEDED_COL_BLOCKS[j]
                    sb = [s_ref[hp, rs, cb * LANES:(cb + 1) * LANES] for cb in need]
                    m = jnp.max(functools.reduce(jnp.maximum, sb), axis=-1, keepdims=True)
                    e = [jnp.exp(x - m) for x in sb]
                    inv_l.append(
                        1.0 / jnp.sum(functools.reduce(jnp.add, e), axis=-1, keepdims=True))
                    zero = jnp.zeros((SLAB, LANES), BF16)
                    blocks = [zero] * COL_BLOCKS
                    for cb, x in zip(need, e):
                        blocks[cb] = x.astype(BF16)
                    p_rows.append(jnp.concatenate(blocks, axis=1))
            o = jnp.dot(jnp.concatenate(p_rows, axis=0), window(vext, start, c2),
                        preferred_element_type=F32)
            for half in range(2):
                c = slice((2 * quad + half) * LANES, (2 * quad + half + 1) * LANES)
                for j in range(Q_BLOCKS):
                    r_lo = half * n_hp + j * SLAB
                    oj = o[r_lo:r_lo + SLAB, half * LANES:(half + 1) * LANES]
                    oj = oj * inv_l[half * Q_BLOCKS + j]
                    y_ref[pl.ds(pl.multiple_of(qrow + j * WIN_W, WIN_W), WIN_W), c] = jnp.where(
                        lo, oj[:WIN_W], oj[WIN_W:]).astype(BF16)

    scores(0, s0_ref)

    def row_pair(k, carry):
        lr = 2 * k
        scores(lr + 1, s1_ref)
        attend(lr, s0_ref)
        scores(lr + 2, s0_ref)
        attend(lr + 1, s1_ref)
        return carry

    lax.fori_loop(0, tile_rows // 2 - 1, row_pair, 0)
    scores(tile_rows - 1, s1_ref)
    attend(tile_rows - 2, s0_ref)
    attend(tile_rows - 1, s1_ref)

    pext[0:F32_ROWS] = jnp.where(i > 0, pp_ref[...].astype(F32)[F32_ROWS:], 0.0)
    pext[F32_ROWS:F32_ROWS + tq] = p_ref[...].astype(F32)
    pext[F32_ROWS + tq:2 * F32_ROWS + tq] = jnp.where(
        i < nt - 1, pn_ref[...].astype(F32)[:F32_ROWS], 0.0)
    conv = (cw_ref[0:1] * pext[F32_ROWS - 1:F32_ROWS - 1 + tq]
            + cw_ref[1:2] * pext[F32_ROWS:F32_ROWS + tq]
            + cw_ref[2:3] * pext[F32_ROWS + 1:F32_ROWS + 1 + tq])
    y_ref[:, NA_WIDTH:NA_WIDTH + CONV_WIDTH] = (b_ref[...].astype(F32) * conv).astype(BF16)

    scale = 1.0 / math.sqrt(MEM_HEAD_DIM)
    for h in range(MEM_HEADS):
        c = slice(h * MEM_HEAD_DIM, (h + 1) * MEM_HEAD_DIM)
        s = lax.dot_general(qm_ref[:, c], mk_ref[:, c], (((1,), (1,)), ((), ())),
                            preferred_element_type=F32) * scale
        o0 = NA_WIDTH + CONV_WIDTH + h * MEM_HEAD_DIM
        y_ref[:, o0:o0 + MEM_HEAD_DIM] = _softmax_pv(s, mv_ref[:, c]).astype(BF16)


def _mixers(q, k, v, p, bb, qm, mk, mv, bias_tab, conv_w, tile_rows):
    b, t, _ = q.shape
    rows = t // GRID_W
    tq = tile_rows * GRID_W
    nt = t // tq
    halo = KEY_ROWS_HALO * GRID_W
    assert rows >= WIN_H and rows % tile_rows == 0 and tile_rows >= KEY_ROWS_HALO
    assert tile_rows % 2 == 0
    assert tq % halo == 0 and tq % BF16_ROWS == 0
    n_mem = mk.shape[1]

    def main(w):
        return pl.BlockSpec((None, tq, w), lambda bi, i: (bi, i, 0))

    def prev(blk, w):
        per = tq // blk
        return pl.BlockSpec((None, blk, w), lambda bi, i: (bi, jnp.maximum(i * per - 1, 0), 0))

    def nxt(blk, w):
        per = tq // blk
        last = t // blk - 1
        return pl.BlockSpec((None, blk, w), lambda bi, i: (bi, jnp.minimum((i + 1) * per, last), 0))

    kv_specs = [main(NA_WIDTH), prev(halo, NA_WIDTH), nxt(halo, NA_WIDTH)]
    mem_spec = pl.BlockSpec((None, n_mem, MEM_WIDTH), lambda bi, i: (bi, 0, 0))
    d_mix = NA_WIDTH + CONV_WIDTH + MEM_WIDTH
    return pl.pallas_call(
        functools.partial(_mixers_kernel, rows=rows, tile_rows=tile_rows),
        grid=(b, nt),
        in_specs=[main(NA_WIDTH)] + kv_specs + kv_specs
                 + [main(CONV_WIDTH), prev(BF16_ROWS, CONV_WIDTH), nxt(BF16_ROWS, CONV_WIDTH),
                    main(CONV_WIDTH), main(MEM_WIDTH), mem_spec, mem_spec,
                    pl.BlockSpec(bias_tab.shape, lambda bi, i: (0,) * bias_tab.ndim,
                                 pipeline_mode=pl.Buffered(1)),
                    pl.BlockSpec(conv_w.shape, lambda bi, i: (0, 0))],
        out_specs=main(d_mix),
        out_shape=jax.ShapeDtypeStruct((b, t, d_mix), BF16),
        scratch_shapes=[pltpu.VMEM((tq + 2 * halo, NA_WIDTH), BF16),
                        pltpu.VMEM((tq + 2 * halo, NA_WIDTH), BF16),
                        pltpu.VMEM((tq + 2 * F32_ROWS, CONV_WIDTH), F32),
                        pltpu.VMEM((HEAD_PAIRS, 2 * GRID_W, WIN_H * GRID_W), F32),
                        pltpu.VMEM((HEAD_PAIRS, 2 * GRID_W, WIN_H * GRID_W), F32)],
        compiler_params=pltpu.CompilerParams(
            dimension_semantics=("parallel", "parallel"), vmem_limit_bytes=VMEM_LIMIT),
        name="mixers",
    )(q, k, k, k, v, v, v, p, p, p, bb, qm, mk, mv, bias_tab, conv_w)


def _out_proj_kernel(x_ref, y_ref, w_ref, o_ref):
    o_ref[...] = x_ref[...] + jnp.dot(y_ref[...], w_ref[...], preferred_element_type=F32)


def _out_proj(x, y, w_b, tm):
    b, t, d = x.shape
    row = lambda w: pl.BlockSpec((None, tm, w), lambda bi, i: (bi, i, 0))
    return pl.pallas_call(
        _out_proj_kernel,
        grid=(b, t // tm),
        in_specs=[row(d), row(y.shape[-1]), pl.BlockSpec(w_b.shape, lambda bi, i: (0, 0))],
        out_specs=row(d),
        out_shape=jax.ShapeDtypeStruct((b, t, d), F32),
        compiler_params=pltpu.CompilerParams(
            dimension_semantics=("parallel", "parallel"), vmem_limit_bytes=VMEM_LIMIT),
        name="out_proj",
    )(x, y, w_b)


def _ffn_kernel(x_ref, xp_ref, xn_ref, g_ref, wa_ref, wg_ref, cwa_ref, cwg_ref, wo_ref,
                o_ref, next_ref, pa_ref, pg_ref, *, tm):
    i = pl.program_id(1)
    j = pl.program_id(2)
    nt = pl.num_programs(1)
    h0 = BF16_ROWS

    @pl.when(j == 0)
    def _():
        g = g_ref[...]
        zeros = jnp.zeros((F32_ROWS, x_ref.shape[-1]), F32)
        prev = jnp.where(i > 0, _rms(xp_ref[...], g), 0.0)
        nxt = jnp.where(i < nt - 1, _rms(xn_ref[...], g), 0.0)
        next_ref[0:h0] = jnp.concatenate([zeros, prev], axis=0).astype(BF16)
        next_ref[h0:h0 + tm] = _rms(x_ref[...], g).astype(BF16)
        next_ref[h0 + tm:2 * h0 + tm] = jnp.concatenate([nxt, zeros], axis=0).astype(BF16)
        o_ref[...] = x_ref[...]

    n2 = next_ref[...]
    pa_ref[...] = jnp.dot(n2, wa_ref[...], preferred_element_type=F32)
    pg_ref[...] = jnp.dot(n2, wg_ref[...], preferred_element_type=F32)

    def conv(ref, cw):
        return (cw[0:1] * ref[h0 - 1:h0 - 1 + tm] + cw[1:2] * ref[h0:h0 + tm]
                + cw[2:3] * ref[h0 + 1:h0 + 1 + tm])

    hidden = (jax.nn.silu(conv(pa_ref, cwa_ref)) * conv(pg_ref, cwg_ref)).astype(BF16)
    o_ref[...] += jnp.dot(hidden, wo_ref[...], preferred_element_type=F32)


def _ffn(x1, g, w_in_b, conv_w, w_out_b, tm, tf):
    b, t, d = x1.shape
    d_ff = w_out_b.shape[0]
    nj = d_ff // tf
    assert d_ff % tf == 0 and t % tm == 0 and tm % F32_ROWS == 0
    per = tm // F32_ROWS
    last = t // F32_ROWS - 1
    return pl.pallas_call(
        functools.partial(_ffn_kernel, tm=tm),
        grid=(b, t // tm, nj),
        in_specs=[
            pl.BlockSpec((None, tm, d), lambda bi, i, j: (bi, i, 0)),
            pl.BlockSpec((None, F32_ROWS, d), lambda bi, i, j: (bi, jnp.maximum(i * per - 1, 0), 0)),
            pl.BlockSpec((None, F32_ROWS, d),
                         lambda bi, i, j: (bi, jnp.minimum((i + 1) * per, last), 0)),
            pl.BlockSpec((1, d), lambda bi, i, j: (0, 0)),
            pl.BlockSpec((d, tf), lambda bi, i, j: (0, j)),
            pl.BlockSpec((d, tf), lambda bi, i, j: (0, nj + j)),
            pl.BlockSpec((3, tf), lambda bi, i, j: (0, j)),
            pl.BlockSpec((3, tf), lambda bi, i, j: (0, nj + j)),
            pl.BlockSpec((tf, d), lambda bi, i, j: (j, 0)),
        ],
        out_specs=pl.BlockSpec((None, tm, d), lambda bi, i, j: (bi, i, 0)),
        out_shape=jax.ShapeDtypeStruct((b, t, d), F32),
        scratch_shapes=[pltpu.VMEM((tm + 2 * BF16_ROWS, d), BF16),
                        pltpu.VMEM((tm + 2 * BF16_ROWS, tf), F32),
                        pltpu.VMEM((tm + 2 * BF16_ROWS, tf), F32)],
        compiler_params=pltpu.CompilerParams(
            dimension_semantics=("parallel", "parallel", "arbitrary"),
            vmem_limit_bytes=VMEM_LIMIT),
        name="ffn",
    )(x1, x1, x1, g, w_in_b, w_in_b, conv_w, conv_w, w_out_b)


_TM_PROJ = 512
_TILE_ROWS = 8
_TM_FFN = 512
_TF_FFN = 512


def _layer(x, mem, p):
    mk, mv = _mem_kv(mem, p["mem_norm_g"], p["w_mem_kv"], p["mem_k_gain"])
    q, k, v, ch, bb, qm = _in_proj(x, p["g_mix"], p["w_in"], p["na_q_gain"], p["na_k_gain"],
                                   p["mem_q_gain"], _TM_PROJ)
    y = _mixers(q, k, v, ch, bb, qm, mk, mv, p["bias_tab"], p["conv_w"], _TILE_ROWS)
    x1 = _out_proj(x, y, p["w_out"], _TM_PROJ)
    return _ffn(x1, p["g_ffn"], p["w_ffn_in"], p["ffn_conv_w"], p["w_ffn_out"], _TM_FFN, _TF_FFN)


def kernel(x_prompt, x_sample, mem_prompt, mem_sample, g_mix, w_in, na_q_gain, na_k_gain,
           na_rel_bias, conv_w, mem_norm_g, w_mem_kv, mem_q_gain, mem_k_gain, w_out,
           g_ffn, w_ffn_in, ffn_conv_w, w_ffn_out):
    y_prompt, y_sample = x_prompt, x_sample
    for l in range(g_mix.shape[0]):
        row = lambda a: a[l].reshape(1, -1).astype(F32)
        q_scale = 1.0 / math.sqrt(HEAD_DIM)
        p = dict(
            g_mix=row(g_mix), g_ffn=row(g_ffn), mem_norm_g=row(mem_norm_g),
            na_q_gain=jnp.tile(row(na_q_gain), (1, 2)) * q_scale,
            na_k_gain=jnp.tile(row(na_k_gain), (1, 2)),
            mem_q_gain=row(mem_q_gain), mem_k_gain=row(mem_k_gain),
            w_in=w_in[l].astype(BF16), w_mem_kv=w_mem_kv[l].astype(BF16),
            w_out=w_out[l].astype(BF16), w_ffn_in=w_ffn_in[l].astype(BF16),
            w_ffn_out=w_ffn_out[l].astype(BF16),
            conv_w=conv_w[l].astype(F32), ffn_conv_w=ffn_conv_w[l].astype(F32),
            bias_tab=_bias_table(na_rel_bias[l]),
        )
        y_prompt = _layer(y_prompt, mem_prompt, p)
        y_sample = _layer(y_sample, mem_sample, p)
    return (y_prompt, y_sample)
```

```python
import functools
import math

import jax
import jax.numpy as jnp
import numpy as np
from jax import lax
from jax.experimental import pallas as pl
from jax.experimental.pallas import tpu as pltpu

F32 = jnp.float32
BF16 = jnp.bfloat16

GRID_W = 64
WIN_H = 8
WIN_W = 16
HEAD_DIM = 64
NA_HEADS = 16
NA_WIDTH = NA_HEADS * HEAD_DIM
CONV_WIDTH = 512
MEM_HEADS = 4
MEM_HEAD_DIM = 128
MEM_WIDTH = MEM_HEADS * MEM_HEAD_DIM
EPS = 1e-6
NEG_INF = -1e30

LANES = 128
BF16_ROWS = 16
F32_ROWS = 8
VMEM_LIMIT = 56 * 1024 * 1024

HEAD_PAIRS = NA_HEADS // 2
KEY_ROWS_HALO = WIN_H // 2
N_DR = 2 * WIN_H - 1


def _rms(x, g):
    return x * lax.rsqrt(jnp.mean(x * x, axis=-1, keepdims=True) + EPS) * g


def _lo_lanes():
    return lax.broadcasted_iota(jnp.int32, (1, LANES), 1) < HEAD_DIM


def _pair_rms(z, g2):
    lo = _lo_lanes()
    sq = z * z
    tot = jnp.sum(sq, axis=-1, keepdims=True)
    s0 = jnp.sum(jnp.where(lo, sq, 0.0), axis=-1, keepdims=True)
    ms = jnp.where(lo, s0, tot - s0) * (1.0 / HEAD_DIM)
    return z * lax.rsqrt(ms + EPS) * g2


def _softmax_pv(s, v):
    m = jnp.max(s, axis=-1, keepdims=True)
    e = jnp.exp(s - m)
    l = jnp.sum(e, axis=-1, keepdims=True)
    o = jnp.dot(e.astype(BF16), v, preferred_element_type=F32)
    return o * (1.0 / l)


def _mem_kv_kernel(mem_ref, g_ref, w_ref, kg_ref, mk_ref, mv_ref):
    n = _rms(mem_ref[...], g_ref[...]).astype(BF16)
    z = jnp.dot(n, w_ref[...], preferred_element_type=F32)
    for h in range(MEM_HEADS):
        c = slice(h * MEM_HEAD_DIM, (h + 1) * MEM_HEAD_DIM)
        mk_ref[:, c] = _rms(z[:, c], kg_ref[...]).astype(BF16)
    mv_ref[...] = z[:, MEM_WIDTH:].astype(BF16)


def _mem_kv(mem, g, w_b, k_gain):
    b, m, d = mem.shape
    out = jax.ShapeDtypeStruct((b, m, MEM_WIDTH), BF16)
    return pl.pallas_call(
        _mem_kv_kernel,
        grid=(b,),
        in_specs=[
            pl.BlockSpec((None, m, d), lambda i: (i, 0, 0)),
            pl.BlockSpec((1, d), lambda i: (0, 0)),
            pl.BlockSpec((d, 2 * MEM_WIDTH), lambda i: (0, 0)),
            pl.BlockSpec((1, MEM_HEAD_DIM), lambda i: (0, 0)),
        ],
        out_specs=[pl.BlockSpec((None, m, MEM_WIDTH), lambda i: (i, 0, 0))] * 2,
        out_shape=[out, out],
        compiler_params=pltpu.CompilerParams(
            dimension_semantics=("parallel",), vmem_limit_bytes=VMEM_LIMIT),
        name="mem_kv",
    )(mem, g, w_b, k_gain)


_Q0, _K0, _V0 = 0, NA_WIDTH, 2 * NA_WIDTH
_H0 = 3 * NA_WIDTH
_B0 = _H0 + CONV_WIDTH
_C0 = _B0 + CONV_WIDTH
_QM0 = _C0 + CONV_WIDTH
_PROJ_CHUNK = 512


def _in_proj_kernel(x_ref, g_ref, w_ref, qg_ref, kg_ref, mqg_ref,
                    q_ref, k_ref, v_ref, p_ref, b_ref, qm_ref):
    n = _rms(x_ref[...], g_ref[...]).astype(BF16)

    def proj(c0):
        return jnp.dot(n, w_ref[:, c0:c0 + _PROJ_CHUNK], preferred_element_type=F32)

    for base, gain_ref, out_ref in ((_Q0, qg_ref, q_ref), (_K0, kg_ref, k_ref)):
        for cc in range(NA_WIDTH // _PROJ_CHUNK):
            z = proj(base + cc * _PROJ_CHUNK)
            for s in range(_PROJ_CHUNK // LANES):
                o = cc * _PROJ_CHUNK + s * LANES
                out_ref[:, o:o + LANES] = _pair_rms(
                    z[:, s * LANES:(s + 1) * LANES], gain_ref[...]).astype(BF16)
    for cc in range(NA_WIDTH // _PROJ_CHUNK):
        o = cc * _PROJ_CHUNK
        v_ref[:, o:o + _PROJ_CHUNK] = proj(_V0 + o).astype(BF16)
    p_ref[...] = (proj(_C0) * proj(_H0)).astype(BF16)
    b_ref[...] = proj(_B0).astype(BF16)
    z = proj(_QM0)
    for h in range(MEM_HEADS):
        c = slice(h * MEM_HEAD_DIM, (h + 1) * MEM_HEAD_DIM)
        qm_ref[:, c] = _rms(z[:, c], mqg_ref[...]).astype(BF16)


def _in_proj(x, g, w_b, q_gain2, k_gain2, mq_gain, tm):
    b, t, d = x.shape
    nt = t // tm
    row = lambda w: pl.BlockSpec((None, tm, w), lambda bi, i: (bi, i, 0))
    const = lambda shape: pl.BlockSpec(shape, lambda bi, i: (0,) * len(shape))
    sds = lambda w: jax.ShapeDtypeStruct((b, t, w), BF16)
    return pl.pallas_call(
        _in_proj_kernel,
        grid=(b, nt),
        in_specs=[row(d), const((1, d)), const(w_b.shape),
                  const((1, LANES)), const((1, LANES)), const((1, MEM_HEAD_DIM))],
        out_specs=[row(NA_WIDTH)] * 3 + [row(CONV_WIDTH)] * 2 + [row(MEM_WIDTH)],
        out_shape=[sds(NA_WIDTH)] * 3 + [sds(CONV_WIDTH)] * 2 + [sds(MEM_WIDTH)],
        compiler_params=pltpu.CompilerParams(
            dimension_semantics=("parallel", "parallel"), vmem_limit_bytes=VMEM_LIMIT),
        name="in_proj",
    )(x, g, w_b, q_gain2, k_gain2, mq_gain)


COL_BLOCKS = GRID_W // WIN_W
Q_BLOCKS = GRID_W // WIN_W
SLAB = 2 * WIN_W
_NEEDED_COL_BLOCKS = tuple(
    tuple(cb for cb in range(COL_BLOCKS)
          if any(max(0, min(q - WIN_W // 2, GRID_W - WIN_W)) < (cb + 1) * WIN_W
                 and max(0, min(q - WIN_W // 2, GRID_W - WIN_W)) + WIN_W > cb * WIN_W
                 for q in range(j * WIN_W, (j + 1) * WIN_W)))
    for j in range(Q_BLOCKS))


_SCORE_BLOCKS = tuple((j, cb) for j in range(Q_BLOCKS) for cb in _NEEDED_COL_BLOCKS[j])


def _bias_table(rel_bias):
    h = rel_bias.shape[0]
    n_dc = 2 * WIN_W - 1
    cr = np.arange(WIN_W)[:, None]
    cl = np.arange(WIN_W)[None, :]
    select = np.zeros((len(_SCORE_BLOCKS), WIN_W, WIN_W, n_dc), np.float32)
    mask = np.zeros((len(_SCORE_BLOCKS), WIN_W, WIN_W), np.float32)
    for n, (j, cb) in enumerate(_SCORE_BLOCKS):
        q, k = j * WIN_W + cr, cb * WIN_W + cl
        q_cs = np.clip(q - WIN_W // 2, 0, GRID_W - WIN_W)
        valid = (k >= q_cs) & (k < q_cs + WIN_W)
        select[n] = ((k - q + WIN_W - 1)[..., None] == np.arange(n_dc)) & valid[..., None]
        mask[n] = np.where(valid, 0.0, NEG_INF)
    rb = rel_bias.astype(F32).reshape(h // 2, 2, N_DR, n_dc)
    base = jnp.einsum("pedt,brlt->pberdl", rb, select, precision=lax.Precision.HIGHEST)
    base = base + mask[None, :, None, :, None, :]
    base = base.reshape(h // 2, len(_SCORE_BLOCKS), SLAB, N_DR * WIN_W)
    base = jnp.pad(base, ((0, 0), (0, 0), (0, 0), (0, 2 * LANES - N_DR * WIN_W)))
    return pl.pallas_call(
        _bias_expand_kernel,
        grid=(h // 2,),
        in_specs=[pl.BlockSpec((None,) + base.shape[1:], lambda p: (p, 0, 0, 0))],
        out_specs=pl.BlockSpec((WIN_H, None, len(_SCORE_BLOCKS), SLAB, LANES),
                               lambda p: (0, p, 0, 0, 0)),
        out_shape=jax.ShapeDtypeStruct((WIN_H, h // 2, len(_SCORE_BLOCKS), SLAB, LANES), F32),
        name="bias_expand",
    )(base)


def _bias_expand_kernel(base_ref, o_ref):
    for n in range(base_ref.shape[0]):
        blk = base_ref[n]
        for d0 in range(WIN_H):
            o_ref[d0, n] = blk[:, d0 * WIN_W:d0 * WIN_W + LANES]


def _mixers_kernel(q_ref, k_ref, kp_ref, kn_ref, v_ref, vp_ref, vn_ref,
                   p_ref, pp_ref, pn_ref, b_ref, qm_ref, mk_ref, mv_ref, bt_ref, cw_ref,
                   y_ref, kext, vext, pext, s0_ref, s1_ref, *, rows, tile_rows):
    i = pl.program_id(1)
    nt = pl.num_programs(1)
    tq = tile_rows * GRID_W
    halo = KEY_ROWS_HALO * GRID_W
    lo = _lo_lanes()
    n_hp = 2 * GRID_W

    for ext, prev, main, nxt in ((kext, kp_ref, k_ref, kn_ref), (vext, vp_ref, v_ref, vn_ref)):
        ext[0:halo] = prev[...]
        ext[halo:halo + tq] = main[...]
        ext[halo + tq:halo + tq + halo] = nxt[...]

    r0 = i * tile_rows

    def window(ext, start, c):
        return jnp.concatenate(
            [ext[pl.ds(pl.multiple_of(start + wr * GRID_W + cb * WIN_W, WIN_W), WIN_W), c]
             for cb in range(COL_BLOCKS) for wr in range(WIN_H)], axis=0)

    def row_geometry(lr):
        r = r0 + lr
        ws = jnp.clip(r - WIN_H // 2, 0, rows - WIN_H)
        start = pl.multiple_of((ws - r0 + KEY_ROWS_HALO) * GRID_W, GRID_W)
        dr0 = ws - r + (WIN_H - 1)
        return start, pl.multiple_of(lr * GRID_W, GRID_W), dr0

    def scores(lr, s_ref):
        start, qrow, dr0 = row_geometry(lr)
        zero = jnp.zeros((WIN_W, LANES), BF16)
        for quad in range(HEAD_PAIRS // 2):
            c2 = slice(2 * quad * LANES, (2 * quad + 2) * LANES)
            q4 = q_ref[pl.ds(qrow, GRID_W), c2]
            pieces = []
            for half in range(2):
                for j in range(Q_BLOCKS):
                    qj = q4[j * WIN_W:(j + 1) * WIN_W, half * LANES:(half + 1) * LANES]
                    for piece in (jnp.where(lo, qj, zero), jnp.where(lo, zero, qj)):
                        pieces.append(jnp.concatenate(
                            [piece, zero] if half == 0 else [zero, piece], axis=1))
            s = lax.dot_general(jnp.concatenate(pieces, axis=0), window(kext, start, c2),
                                (((1,), (1,)), ((), ())), preferred_element_type=F32)
            for half in range(2):
                hp = 2 * quad + half
                for n, (j, cb) in enumerate(_SCORE_BLOCKS):
                    rs = slice(j * SLAB, (j + 1) * SLAB)
                    cs = slice(cb * LANES, (cb + 1) * LANES)
                    s_ref[hp, rs, cs] = (s[half * n_hp + j * SLAB:half * n_hp + (j + 1) * SLAB, cs]
                                         + bt_ref[dr0, hp, n])

    def attend(lr, s_ref):
        start, qrow, _ = row_geometry(lr)
        for quad in range(HEAD_PAIRS // 2):
            c2 = slice(2 * quad * LANES, (2 * quad + 2) * LANES)
            p_rows, inv_l = [], []
            for half in range(2):
                hp = 2 * quad + half
                for j in range(Q_BLOCKS):
                    rs = slice(j * SLAB, (j + 1) * SLAB)
                    need = _NEEDED_COL_BLOCKS[j]
                    sb = [s_ref[hp, rs, cb * LANES:(cb + 1) * LANES] for cb in need]
                    m = jnp.max(functools.reduce(jnp.maximum, sb), axis=-1, keepdims=True)
                    e = [jnp.exp(x - m) for x in sb]
                    inv_l.append(
                        1.0 / jnp.sum(functools.reduce(jnp.add, e), axis=-1, keepdims=True))
                    zero = jnp.zeros((SLAB, LANES), BF16)
                    blocks = [zero] * COL_BLOCKS
                    for cb, x in zip(need, e):
                        blocks[cb] = x.astype(BF16)
                    p_rows.append(jnp.concatenate(blocks, axis=1))
            o = jnp.dot(jnp.concatenate(p_rows, axis=0), window(vext, start, c2),
                        preferred_element_type=F32)
            for half in range(2):
                c = slice((2 * quad + half) * LANES, (2 * quad + half + 1) * LANES)
                for j in range(Q_BLOCKS):
                    r_lo = half * n_hp + j * SLAB
                    oj = o[r_lo:r_lo + SLAB, half * LANES:(half + 1) * LANES]
                    oj = oj * inv_l[half * Q_BLOCKS + j]
                    y_ref[pl.ds(pl.multiple_of(qrow + j * WIN_W, WIN_W), WIN_W), c] = jnp.where(
                        lo, oj[:WIN_W], oj[WIN_W:]).astype(BF16)

    scores(0, s0_ref)

    def row_pair(k, carry):
        lr = 2 * k
        scores(lr + 1, s1_ref)
        attend(lr, s0_ref)
        scores(lr + 2, s0_ref)
        attend(lr + 1, s1_ref)
        return carry

    lax.fori_loop(0, tile_rows // 2 - 1, row_pair, 0)
    scores(tile_rows - 1, s1_ref)
    attend(tile_rows - 2, s0_ref)
    attend(tile_rows - 1, s1_ref)

    pext[0:F32_ROWS] = jnp.where(i > 0, pp_ref[...].astype(F32)[F32_ROWS:], 0.0)
    pext[F32_ROWS:F32_ROWS + tq] = p_ref[...].astype(F32)
    pext[F32_ROWS + tq:2 * F32_ROWS + tq] = jnp.where(
        i < nt - 1, pn_ref[...].astype(F32)[:F32_ROWS], 0.0)
    conv = (cw_ref[0:1] * pext[F32_ROWS - 1:F32_ROWS - 1 + tq]
            + cw_ref[1:2] * pext[F32_ROWS:F32_ROWS + tq]
            + cw_ref[2:3] * pext[F32_ROWS + 1:F32_ROWS + 1 + tq])
    y_ref[:, NA_WIDTH:NA_WIDTH + CONV_WIDTH] = (b_ref[...].astype(F32) * conv).astype(BF16)

    scale = 1.0 / math.sqrt(MEM_HEAD_DIM)
    for h in range(MEM_HEADS):
        c = slice(h * MEM_HEAD_DIM, (h + 1) * MEM_HEAD_DIM)
        s = lax.dot_general(qm_ref[:, c], mk_ref[:, c], (((1,), (1,)), ((), ())),
                            preferred_element_type=F32) * scale
        o0 = NA_WIDTH + CONV_WIDTH + h * MEM_HEAD_DIM
        y_ref[:, o0:o0 + MEM_HEAD_DIM] = _softmax_pv(s, mv_ref[:, c]).astype(BF16)


def _mixers(q, k, v, p, bb, qm, mk, mv, bias_tab, conv_w, tile_rows):
    b, t, _ = q.shape
    rows = t // GRID_W
    tq = tile_rows * GRID_W
    nt = t // tq
    halo = KEY_ROWS_HALO * GRID_W
    assert rows >= WIN_H and rows % tile_rows == 0 and tile_rows >= KEY_ROWS_HALO
    assert tile_rows % 2 == 0
    assert tq % halo == 0 and tq % BF16_ROWS == 0
    n_mem = mk.shape[1]

    def main(w):
        return pl.BlockSpec((None, tq, w), lambda bi, i: (bi, i, 0))

    def prev(blk, w):
        per = tq // blk
        return pl.BlockSpec((None, blk, w), lambda bi, i: (bi, jnp.maximum(i * per - 1, 0), 0))

    def nxt(blk, w):
        per = tq // blk
        last = t // blk - 1
        return pl.BlockSpec((None, blk, w), lambda bi, i: (bi, jnp.minimum((i + 1) * per, last), 0))

    kv_specs = [main(NA_WIDTH), prev(halo, NA_WIDTH), nxt(halo, NA_WIDTH)]
    mem_spec = pl.BlockSpec((None, n_mem, MEM_WIDTH), lambda bi, i: (bi, 0, 0))
    d_mix = NA_WIDTH + CONV_WIDTH + MEM_WIDTH
    return pl.pallas_call(
        functools.partial(_mixers_kernel, rows=rows, tile_rows=tile_rows),
        grid=(b, nt),
        in_specs=[main(NA_WIDTH)] + kv_specs + kv_specs
                 + [main(CONV_WIDTH), prev(BF16_ROWS, CONV_WIDTH), nxt(BF16_ROWS, CONV_WIDTH),
                    main(CONV_WIDTH), main(MEM_WIDTH), mem_spec, mem_spec,
                    pl.BlockSpec(bias_tab.shape, lambda bi, i: (0,) * bias_tab.ndim,
                                 pipeline_mode=pl.Buffered(1)),
                    pl.BlockSpec(conv_w.shape, lambda bi, i: (0, 0))],
        out_specs=main(d_mix),
        out_shape=jax.ShapeDtypeStruct((b, t, d_mix), BF16),
        scratch_shapes=[pltpu.VMEM((tq + 2 * halo, NA_WIDTH), BF16),
                        pltpu.VMEM((tq + 2 * halo, NA_WIDTH), BF16),
                        pltpu.VMEM((tq + 2 * F32_ROWS, CONV_WIDTH), F32),
                        pltpu.VMEM((HEAD_PAIRS, 2 * GRID_W, WIN_H * GRID_W), F32),
                        pltpu.VMEM((HEAD_PAIRS, 2 * GRID_W, WIN_H * GRID_W), F32)],
        compiler_params=pltpu.CompilerParams(
            dimension_semantics=("parallel", "parallel"), vmem_limit_bytes=VMEM_LIMIT),
        name="mixers",
    )(q, k, k, k, v, v, v, p, p, p, bb, qm, mk, mv, bias_tab, conv_w)


def _out_proj_kernel(x_ref, y_ref, w_ref, o_ref):
    o_ref[...] = x_ref[...] + jnp.dot(y_ref[...], w_ref[...], preferred_element_type=F32)


def _out_proj(x, y, w_b, tm):
    b, t, d = x.shape
    row = lambda w: pl.BlockSpec((None, tm, w), lambda bi, i: (bi, i, 0))
    return pl.pallas_call(
        _out_proj_kernel,
        grid=(b, t // tm),
        in_specs=[row(d), row(y.shape[-1]), pl.BlockSpec(w_b.shape, lambda bi, i: (0, 0))],
        out_specs=row(d),
        out_shape=jax.ShapeDtypeStruct((b, t, d), F32),
        compiler_params=pltpu.CompilerParams(
            dimension_semantics=("parallel", "parallel"), vmem_limit_bytes=VMEM_LIMIT),
        name="out_proj",
    )(x, y, w_b)


def _ffn_kernel(x_ref, xp_ref, xn_ref, g_ref, wa_ref, wg_ref, cwa_ref, cwg_ref, wo_ref,
                o_ref, next_ref, pa_ref, pg_ref, *, tm):
    i = pl.program_id(1)
    j = pl.program_id(2)
    nt = pl.num_programs(1)
    h0 = BF16_ROWS

    @pl.when(j == 0)
    def _():
        g = g_ref[...]
        zeros = jnp.zeros((F32_ROWS, x_ref.shape[-1]), F32)
        prev = jnp.where(i > 0, _rms(xp_ref[...], g), 0.0)
        nxt = jnp.where(i < nt - 1, _rms(xn_ref[...], g), 0.0)
        next_ref[0:h0] = jnp.concatenate([zeros, prev], axis=0).astype(BF16)
        next_ref[h0:h0 + tm] = _rms(x_ref[...], g).astype(BF16)
        next_ref[h0 + tm:2 * h0 + tm] = jnp.concatenate([nxt, zeros], axis=0).astype(BF16)
        o_ref[...] = x_ref[...]

    n2 = next_ref[...]
    pa_ref[...] = jnp.dot(n2, wa_ref[...], preferred_element_type=F32)
    pg_ref[...] = jnp.dot(n2, wg_ref[...], preferred_element_type=F32)

    def conv(ref, cw):
        return (cw[0:1] * ref[h0 - 1:h0 - 1 + tm] + cw[1:2] * ref[h0:h0 + tm]
                + cw[2:3] * ref[h0 + 1:h0 + 1 + tm])

    hidden = (jax.nn.silu(conv(pa_ref, cwa_ref)) * conv(pg_ref, cwg_ref)).astype(BF16)
    o_ref[...] += jnp.dot(hidden, wo_ref[...], preferred_element_type=F32)


def _ffn(x1, g, w_in_b, conv_w, w_out_b, tm, tf):
    b, t, d = x1.shape
    d_ff = w_out_b.shape[0]
    nj = d_ff // tf
    assert d_ff % tf == 0 and t % tm == 0 and tm % F32_ROWS == 0
    per = tm // F32_ROWS
    last = t // F32_ROWS - 1
    return pl.pallas_call(
        functools.partial(_ffn_kernel, tm=tm),
        grid=(b, t // tm, nj),
        in_specs=[
            pl.BlockSpec((None, tm, d), lambda bi, i, j: (bi, i, 0)),
            pl.BlockSpec((None, F32_ROWS, d), lambda bi, i, j: (bi, jnp.maximum(i * per - 1, 0), 0)),
            pl.BlockSpec((None, F32_ROWS, d),
                         lambda bi, i, j: (bi, jnp.minimum((i + 1) * per, last), 0)),
            pl.BlockSpec((1, d), lambda bi, i, j: (0, 0)),
            pl.BlockSpec((d, tf), lambda bi, i, j: (0, j)),
            pl.BlockSpec((d, tf), lambda bi, i, j: (0, nj + j)),
            pl.BlockSpec((3, tf), lambda bi, i, j: (0, j)),
            pl.BlockSpec((3, tf), lambda bi, i, j: (0, nj + j)),
            pl.BlockSpec((tf, d), lambda bi, i, j: (j, 0)),
        ],
        out_specs=pl.BlockSpec((None, tm, d), lambda bi, i, j: (bi, i, 0)),
        out_shape=jax.ShapeDtypeStruct((b, t, d), F32),
        scratch_shapes=[pltpu.VMEM((tm + 2 * BF16_ROWS, d), BF16),
                        pltpu.VMEM((tm + 2 * BF16_ROWS, tf), F32),
                        pltpu.VMEM((tm + 2 * BF16_ROWS, tf), F32)],
        compiler_params=pltpu.CompilerParams(
            dimension_semantics=("parallel", "parallel", "arbitrary"),
            vmem_limit_bytes=VMEM_LIMIT),
        name="ffn",
    )(x1, x1, x1, g, w_in_b, w_in_b, conv_w, conv_w, w_out_b)


_TM_PROJ = 512
_TILE_ROWS = 8
_TM_FFN = 512
_TF_FFN = 512


def _layer(x, mem, p):
    mk, mv = _mem_kv(mem, p["mem_norm_g"], p["w_mem_kv"], p["mem_k_gain"])
    q, k, v, ch, bb, qm = _in_proj(x, p["g_mix"], p["w_in"], p["na_q_gain"], p["na_k_gain"],
                                   p["mem_q_gain"], _TM_PROJ)
    y = _mixers(q, k, v, ch, bb, qm, mk, mv, p["bias_tab"], p["conv_w"], _TILE_ROWS)
    x1 = _out_proj(x, y, p["w_out"], _TM_PROJ)
    return _ffn(x1, p["g_ffn"], p["w_ffn_in"], p["ffn_conv_w"], p["w_ffn_out"], _TM_FFN, _TF_FFN)


def kernel(x_prompt, x_sample, mem_prompt, mem_sample, g_mix, w_in, na_q_gain, na_k_gain,
           na_rel_bias, conv_w, mem_norm_g, w_mem_kv, mem_q_gain, mem_k_gain, w_out,
           g_ffn, w_ffn_in, ffn_conv_w, w_ffn_out):
    y_prompt, y_sample = x_prompt, x_sample
    for l in range(g_mix.shape[0]):
        row = lambda a: a[l].reshape(1, -1).astype(F32)
        q_scale = 1.0 / math.sqrt(HEAD_DIM)
        p = dict(
            g_mix=row(g_mix), g_ffn=row(g_ffn), mem_norm_g=row(mem_norm_g),
            na_q_gain=jnp.tile(row(na_q_gain), (1, 2)) * q_scale,
            na_k_gain=jnp.tile(row(na_k_gain), (1, 2)),
            mem_q_gain=row(mem_q_gain), mem_k_gain=row(mem_k_gain),
            w_in=w_in[l].astype(BF16), w_mem_kv=w_mem_kv[l].astype(BF16),
            w_out=w_out[l].astype(BF16), w_ffn_in=w_ffn_in[l].astype(BF16),
            w_ffn_out=w_ffn_out[l].astype(BF16),
            conv_w=conv_w[l].astype(F32), ffn_conv_w=ffn_conv_w[l].astype(F32),
            bias_tab=_bias_table(na_rel_bias[l]),
        )
        y_prompt = _layer(y_prompt, mem_prompt, p)
        y_sample = _layer(y_sample, mem_sample, p)
    return (y_prompt, y_sample)
```

```python
import functools
import math

import jax
import jax.numpy as jnp
import numpy as np
from jax import lax
from jax.experimental import pallas as pl
from jax.experimental.pallas import tpu as pltpu

F32 = jnp.float32
BF16 = jnp.bfloat16

GRID_W = 64
WIN_H = 8
WIN_W = 16
HEAD_DIM = 64
NA_HEADS = 16
NA_WIDTH = NA_HEADS * HEAD_DIM
CONV_WIDTH = 512
MEM_HEADS = 4
MEM_HEAD_DIM = 128
MEM_WIDTH = MEM_HEADS * MEM_HEAD_DIM
EPS = 1e-6
NEG_INF = -1e30

LANES = 128
BF16_ROWS = 16
F32_ROWS = 8
VMEM_LIMIT = 56 * 1024 * 1024

HEAD_PAIRS = NA_HEADS // 2
KEY_ROWS_HALO = WIN_H // 2
N_DR = 2 * WIN_H - 1


def _rms(x, g):
    return x * lax.rsqrt(jnp.mean(x * x, axis=-1, keepdims=True) + EPS) * g


def _lo_lanes():
    return lax.broadcasted_iota(jnp.int32, (1, LANES), 1) < HEAD_DIM


def _pair_rms(z, g2):
    lo = _lo_lanes()
    sq = z * z
    tot = jnp.sum(sq, axis=-1, keepdims=True)
    s0 = jnp.sum(jnp.where(lo, sq, 0.0), axis=-1, keepdims=True)
    ms = jnp.where(lo, s0, tot - s0) * (1.0 / HEAD_DIM)
    return z * lax.rsqrt(ms + EPS) * g2


def _softmax_pv(s, v):
    m = jnp.max(s, axis=-1, keepdims=True)
    e = jnp.exp(s - m)
    l = jnp.sum(e, axis=-1, keepdims=True)
    o = jnp.dot(e.astype(BF16), v, preferred_element_type=F32)
    return o * (1.0 / l)


def _mem_kv_kernel(mem_ref, g_ref, w_ref, kg_ref, mk_ref, mv_ref):
    n = _rms(mem_ref[...], g_ref[...]).astype(BF16)
    z = jnp.dot(n, w_ref[...].astype(BF16), preferred_element_type=F32)
    for h in range(MEM_HEADS):
        c = slice(h * MEM_HEAD_DIM, (h + 1) * MEM_HEAD_DIM)
        mk_ref[:, c] = _rms(z[:, c], kg_ref[...]).astype(BF16)
    mv_ref[...] = z[:, MEM_WIDTH:].astype(BF16)


def _mem_kv(mem, g, w_b, k_gain):
    b, m, d = mem.shape
    out = jax.ShapeDtypeStruct((b, m, MEM_WIDTH), BF16)
    return pl.pallas_call(
        _mem_kv_kernel,
        grid=(b,),
        in_specs=[
            pl.BlockSpec((None, m, d), lambda i: (i, 0, 0)),
            pl.BlockSpec((1, d), lambda i: (0, 0)),
            pl.BlockSpec((d, 2 * MEM_WIDTH), lambda i: (0, 0)),
            pl.BlockSpec((1, MEM_HEAD_DIM), lambda i: (0, 0)),
        ],
        out_specs=[pl.BlockSpec((None, m, MEM_WIDTH), lambda i: (i, 0, 0))] * 2,
        out_shape=[out, out],
        compiler_params=pltpu.CompilerParams(
            dimension_semantics=("parallel",), vmem_limit_bytes=VMEM_LIMIT),
        name="mem_kv",
    )(mem, g, w_b, k_gain)


_Q0, _K0, _V0 = 0, NA_WIDTH, 2 * NA_WIDTH
_H0 = 3 * NA_WIDTH
_B0 = _H0 + CONV_WIDTH
_C0 = _B0 + CONV_WIDTH
_QM0 = _C0 + CONV_WIDTH
_PROJ_CHUNK = 512


def _in_proj_kernel(x_ref, g_ref, w_ref, qg_ref, kg_ref, mqg_ref,
                    q_ref, k_ref, v_ref, p_ref, b_ref, qm_ref):
    n = _rms(x_ref[...], g_ref[...]).astype(BF16)

    def proj(c0):
        return jnp.dot(n, w_ref[:, c0:c0 + _PROJ_CHUNK], preferred_element_type=F32)

    for base, gain_ref, out_ref in ((_Q0, qg_ref, q_ref), (_K0, kg_ref, k_ref)):
        for cc in range(NA_WIDTH // _PROJ_CHUNK):
            z = proj(base + cc * _PROJ_CHUNK)
            for s in range(_PROJ_CHUNK // LANES):
                o = cc * _PROJ_CHUNK + s * LANES
                out_ref[:, o:o + LANES] = _pair_rms(
                    z[:, s * LANES:(s + 1) * LANES], gain_ref[...]).astype(BF16)
    for cc in range(NA_WIDTH // _PROJ_CHUNK):
        o = cc * _PROJ_CHUNK
        v_ref[:, o:o + _PROJ_CHUNK] = proj(_V0 + o).astype(BF16)
    p_ref[...] = (proj(_C0) * proj(_H0)).astype(BF16)
    b_ref[...] = proj(_B0).astype(BF16)
    z = proj(_QM0)
    for h in range(MEM_HEADS):
        c = slice(h * MEM_HEAD_DIM, (h + 1) * MEM_HEAD_DIM)
        qm_ref[:, c] = _rms(z[:, c], mqg_ref[...]).astype(BF16)


def _in_proj(x, g, w_b, q_gain2, k_gain2, mq_gain, tm):
    b, t, d = x.shape
    nt = t // tm
    row = lambda w: pl.BlockSpec((None, tm, w), lambda bi, i: (bi, i, 0))
    const = lambda shape: pl.BlockSpec(shape, lambda bi, i: (0,) * len(shape))
    sds = lambda w: jax.ShapeDtypeStruct((b, t, w), BF16)
    return pl.pallas_call(
        _in_proj_kernel,
        grid=(b, nt),
        in_specs=[row(d), const((1, d)), const(w_b.shape),
                  const((1, LANES)), const((1, LANES)), const((1, MEM_HEAD_DIM))],
        out_specs=[row(NA_WIDTH)] * 3 + [row(CONV_WIDTH)] * 2 + [row(MEM_WIDTH)],
        out_shape=[sds(NA_WIDTH)] * 3 + [sds(CONV_WIDTH)] * 2 + [sds(MEM_WIDTH)],
        compiler_params=pltpu.CompilerParams(
            dimension_semantics=("parallel", "parallel"), vmem_limit_bytes=VMEM_LIMIT),
        name="in_proj",
    )(x, g, w_b, q_gain2, k_gain2, mq_gain)


COL_BLOCKS = GRID_W // WIN_W
Q_BLOCKS = GRID_W // WIN_W
SLAB = 2 * WIN_W
_NEEDED_COL_BLOCKS = tuple(
    tuple(cb for cb in range(COL_BLOCKS)
          if any(max(0, min(q - WIN_W // 2, GRID_W - WIN_W)) < (cb + 1) * WIN_W
                 and max(0, min(q - WIN_W // 2, GRID_W - WIN_W)) + WIN_W > cb * WIN_W
                 for q in range(j * WIN_W, (j + 1) * WIN_W)))
    for j in range(Q_BLOCKS))


_SCORE_BLOCKS = tuple((j, cb) for j in range(Q_BLOCKS) for cb in _NEEDED_COL_BLOCKS[j])


def _bias_table(rel_bias):
    h = rel_bias.shape[0]
    n_dc = 2 * WIN_W - 1
    n_blk = len(_SCORE_BLOCKS)
    cr = np.arange(WIN_W)[:, None]
    cl = np.arange(WIN_W)[None, :]
    select = np.stack([(WIN_W * (x - 1) + cl - cr + WIN_W - 1)[..., None] == np.arange(n_dc)
                       for x in range(3)]).astype(np.float32)
    mask = np.zeros((n_blk, 2, WIN_W, WIN_H, WIN_W), np.float32)
    for n, (j, cb) in enumerate(_SCORE_BLOCKS):
        q, k = j * WIN_W + cr, cb * WIN_W + cl
        q_cs = np.clip(q - WIN_W // 2, 0, GRID_W - WIN_W)
        mask[n] = np.where((k >= q_cs) & (k < q_cs + WIN_W), 0.0, NEG_INF)[None, :, None, :]
    rb = rel_bias.astype(F32).reshape(h // 2, 2, N_DR, n_dc)
    pieces = jnp.einsum("pedt,xrlt->pedxrl", rb, select, precision=lax.Precision.HIGHEST)
    pieces = jnp.pad(pieces, ((0, 0),) * 5 + ((0, LANES - WIN_W),))
    return pl.pallas_call(
        _bias_expand_kernel,
        grid=(h // 2,),
        in_specs=[pl.BlockSpec((None,) + pieces.shape[1:], lambda p: (p, 0, 0, 0, 0, 0)),
                  pl.BlockSpec((n_blk, SLAB, LANES), lambda p: (0, 0, 0))],
        out_specs=pl.BlockSpec((WIN_H, None, n_blk, SLAB, LANES), lambda p: (0, p, 0, 0, 0)),
        out_shape=jax.ShapeDtypeStruct((WIN_H, h // 2, n_blk, SLAB, LANES), F32),
        name="bias_expand",
    )(pieces, mask.reshape(n_blk, SLAB, LANES))


def _bias_expand_kernel(piece_ref, mask_ref, o_ref):
    for n, (j, cb) in enumerate(_SCORE_BLOCKS):
        for d0 in range(WIN_H):
            halves = []
            for hd in range(2):
                acc = piece_ref[hd, d0, cb - j + 1]
                for wr in range(1, WIN_H):
                    acc = acc + pltpu.roll(piece_ref[hd, d0 + wr, cb - j + 1], wr * WIN_W, axis=1)
                halves.append(acc)
            o_ref[d0, n] = jnp.concatenate(halves, axis=0) + mask_ref[n]


def _mixers_kernel(q_ref, k_ref, kp_ref, kn_ref, v_ref, vp_ref, vn_ref,
                   p_ref, pp_ref, pn_ref, b_ref, qm_ref, mk_ref, mv_ref, bt_ref, cw_ref,
                   y_ref, kext, vext, pext, s0_ref, s1_ref, *, rows, tile_rows):
    i = pl.program_id(1)
    nt = pl.num_programs(1)
    tq = tile_rows * GRID_W
    halo = KEY_ROWS_HALO * GRID_W
    lo = _lo_lanes()
    n_hp = 2 * GRID_W

    for ext, prev, main, nxt in ((kext, kp_ref, k_ref, kn_ref), (vext, vp_ref, v_ref, vn_ref)):
        ext[0:halo] = prev[...]
        ext[halo:halo + tq] = main[...]
        ext[halo + tq:halo + tq + halo] = nxt[...]

    r0 = i * tile_rows

    def window(ext, start, c):
        return jnp.concatenate(
            [ext[pl.ds(pl.multiple_of(start + wr * GRID_W + cb * WIN_W, WIN_W), WIN_W), c]
             for cb in range(COL_BLOCKS) for wr in range(WIN_H)], axis=0)

    def row_geometry(lr):
        r = r0 + lr
        ws = jnp.clip(r - WIN_H // 2, 0, rows - WIN_H)
        start = pl.multiple_of((ws - r0 + KEY_ROWS_HALO) * GRID_W, GRID_W)
        dr0 = ws - r + (WIN_H - 1)
        return start, pl.multiple_of(lr * GRID_W, GRID_W), dr0

    def scores(lr, s_ref):
        start, qrow, dr0 = row_geometry(lr)
        zero = jnp.zeros((WIN_W, LANES), BF16)
        for quad in range(HEAD_PAIRS // 2):
            c2 = slice(2 * quad * LANES, (2 * quad + 2) * LANES)
            q4 = q_ref[pl.ds(qrow, GRID_W), c2]
            pieces = []
            for half in range(2):
                for j in range(Q_BLOCKS):
                    qj = q4[j * WIN_W:(j + 1) * WIN_W, half * LANES:(half + 1) * LANES]
                    for piece in (jnp.where(lo, qj, zero), jnp.where(lo, zero, qj)):
                        pieces.append(jnp.concatenate(
                            [piece, zero] if half == 0 else [zero, piece], axis=1))
            s = lax.dot_general(jnp.concatenate(pieces, axis=0), window(kext, start, c2),
                                (((1,), (1,)), ((), ())), preferred_element_type=F32)
            for half in range(2):
                hp = 2 * quad + half
                for n, (j, cb) in enumerate(_SCORE_BLOCKS):
                    rs = slice(j * SLAB, (j + 1) * SLAB)
                    cs = slice(cb * LANES, (cb + 1) * LANES)
                    s_ref[hp, rs, cs] = (s[half * n_hp + j * SLAB:half * n_hp + (j + 1) * SLAB, cs]
                                         + bt_ref[dr0, hp, n])

    def attend(lr, s_ref):
        start, qrow, _ = row_geometry(lr)
        for quad in range(HEAD_PAIRS // 2):
            c2 = slice(2 * quad * LANES, (2 * quad + 2) * LANES)
            p_rows, inv_l = [], []
            for half in range(2):
                hp = 2 * quad + half
                for j in range(Q_BLOCKS):
                    rs = slice(j * SLAB, (j + 1) * SLAB)
                    need = _NEEDED_COL_BLOCKS[j]
                    sb = [s_ref[hp, rs, cb * LANES:(cb + 1) * LANES] for cb in need]
                    m = jnp.max(functools.reduce(jnp.maximum, sb), axis=-1, keepdims=True)
                    e = [jnp.exp(x - m) for x in sb]
                    inv_l.append(
                        1.0 / jnp.sum(functools.reduce(jnp.add, e), axis=-1, keepdims=True))
                    zero = jnp.zeros((SLAB, LANES), BF16)
                    blocks = [zero] * COL_BLOCKS
                    for cb, x in zip(need, e):
                        blocks[cb] = x.astype(BF16)
                    p_rows.append(jnp.concatenate(blocks, axis=1))
            o = jnp.dot(jnp.concatenate(p_rows, axis=0), window(vext, start, c2),
                        preferred_element_type=F32)
            for half in range(2):
                c = slice((2 * quad + half) * LANES, (2 * quad + half + 1) * LANES)
                for j in range(Q_BLOCKS):
                    r_lo = half * n_hp + j * SLAB
                    oj = o[r_lo:r_lo + SLAB, half * LANES:(half + 1) * LANES]
                    oj = oj * inv_l[half * Q_BLOCKS + j]
                    y_ref[pl.ds(pl.multiple_of(qrow + j * WIN_W, WIN_W), WIN_W), c] = jnp.where(
                        lo, oj[:WIN_W], oj[WIN_W:]).astype(BF16)

    def gated_conv():
        pext[0:F32_ROWS] = jnp.where(i > 0, pp_ref[...].astype(F32)[F32_ROWS:], 0.0)
        pext[F32_ROWS:F32_ROWS + tq] = p_ref[...].astype(F32)
        pext[F32_ROWS + tq:2 * F32_ROWS + tq] = jnp.where(
            i < nt - 1, pn_ref[...].astype(F32)[:F32_ROWS], 0.0)
        conv = (cw_ref[0:1] * pext[F32_ROWS - 1:F32_ROWS - 1 + tq]
                + cw_ref[1:2] * pext[F32_ROWS:F32_ROWS + tq]
                + cw_ref[2:3] * pext[F32_ROWS + 1:F32_ROWS + 1 + tq])
        y_ref[:, NA_WIDTH:NA_WIDTH + CONV_WIDTH] = (b_ref[...].astype(F32) * conv).astype(BF16)

    def mem_attention(h):
        c = slice(h * MEM_HEAD_DIM, (h + 1) * MEM_HEAD_DIM)
        s = lax.dot_general(qm_ref[:, c], mk_ref[:, c], (((1,), (1,)), ((), ())),
                            preferred_element_type=F32) * (1.0 / math.sqrt(MEM_HEAD_DIM))
        o0 = NA_WIDTH + CONV_WIDTH + h * MEM_HEAD_DIM
        y_ref[:, o0:o0 + MEM_HEAD_DIM] = _softmax_pv(s, mv_ref[:, c]).astype(BF16)

    scores(0, s0_ref)

    def row_pair(k, carry):
        lr = 2 * k
        scores(lr + 1, s1_ref)
        attend(lr, s0_ref)
        scores(lr + 2, s0_ref)
        attend(lr + 1, s1_ref)
        return carry

    lax.fori_loop(0, tile_rows // 2 - 1, row_pair, 0)
    scores(tile_rows - 1, s1_ref)
    attend(tile_rows - 2, s0_ref)
    attend(tile_rows - 1, s1_ref)
    gated_conv()
    for h in range(MEM_HEADS):
        mem_attention(h)


def _mixers(q, k, v, p, bb, qm, mk, mv, bias_tab, conv_w, tile_rows):
    b, t, _ = q.shape
    rows = t // GRID_W
    tq = tile_rows * GRID_W
    nt = t // tq
    halo = KEY_ROWS_HALO * GRID_W
    assert rows >= WIN_H and rows % tile_rows == 0 and tile_rows >= KEY_ROWS_HALO
    assert tile_rows % 2 == 0
    assert tq % halo == 0 and tq % BF16_ROWS == 0
    n_mem = mk.shape[1]

    def main(w):
        return pl.BlockSpec((None, tq, w), lambda bi, i: (bi, i, 0))

    def prev(blk, w):
        per = tq // blk
        return pl.BlockSpec((None, blk, w), lambda bi, i: (bi, jnp.maximum(i * per - 1, 0), 0))

    def nxt(blk, w):
        per = tq // blk
        last = t // blk - 1
        return pl.BlockSpec((None, blk, w), lambda bi, i: (bi, jnp.minimum((i + 1) * per, last), 0))

    kv_specs = [main(NA_WIDTH), prev(halo, NA_WIDTH), nxt(halo, NA_WIDTH)]
    mem_spec = pl.BlockSpec((None, n_mem, MEM_WIDTH), lambda bi, i: (bi, 0, 0))
    d_mix = NA_WIDTH + CONV_WIDTH + MEM_WIDTH
    return pl.pallas_call(
        functools.partial(_mixers_kernel, rows=rows, tile_rows=tile_rows),
        grid=(b, nt),
        in_specs=[main(NA_WIDTH)] + kv_specs + kv_specs
                 + [main(CONV_WIDTH), prev(BF16_ROWS, CONV_WIDTH), nxt(BF16_ROWS, CONV_WIDTH),
                    main(CONV_WIDTH), main(MEM_WIDTH), mem_spec, mem_spec,
                    pl.BlockSpec(bias_tab.shape, lambda bi, i: (0,) * bias_tab.ndim,
                                 pipeline_mode=pl.Buffered(1)),
                    pl.BlockSpec(conv_w.shape, lambda bi, i: (0, 0))],
        out_specs=main(d_mix),
        out_shape=jax.ShapeDtypeStruct((b, t, d_mix), BF16),
        scratch_shapes=[pltpu.VMEM((tq + 2 * halo, NA_WIDTH), BF16),
                        pltpu.VMEM((tq + 2 * halo, NA_WIDTH), BF16),
                        pltpu.VMEM((tq + 2 * F32_ROWS, CONV_WIDTH), F32),
                        pltpu.VMEM((HEAD_PAIRS, 2 * GRID_W, WIN_H * GRID_W), F32),
                        pltpu.VMEM((HEAD_PAIRS, 2 * GRID_W, WIN_H * GRID_W), F32)],
        compiler_params=pltpu.CompilerParams(
            dimension_semantics=("parallel", "parallel"), vmem_limit_bytes=VMEM_LIMIT),
        name="mixers",
    )(q, k, k, k, v, v, v, p, p, p, bb, qm, mk, mv, bias_tab, conv_w)


def _out_proj_kernel(x_ref, y_ref, w_ref, o_ref):
    o_ref[...] = x_ref[...] + jnp.dot(y_ref[...], w_ref[...], preferred_element_type=F32)


def _out_proj(x, y, w_b, tm):
    b, t, d = x.shape
    row = lambda w: pl.BlockSpec((None, tm, w), lambda bi, i: (bi, i, 0))
    return pl.pallas_call(
        _out_proj_kernel,
        grid=(b, t // tm),
        in_specs=[row(d), row(y.shape[-1]), pl.BlockSpec(w_b.shape, lambda bi, i: (0, 0))],
        out_specs=row(d),
        out_shape=jax.ShapeDtypeStruct((b, t, d), F32),
        compiler_params=pltpu.CompilerParams(
            dimension_semantics=("parallel", "parallel"), vmem_limit_bytes=VMEM_LIMIT),
        name="out_proj",
    )(x, y, w_b)


def _ffn_kernel(x_ref, xp_ref, xn_ref, g_ref, wi_ref, cw_ref, wo_ref,
                o_ref, next_ref, pre_ref, *, tm):
    i = pl.program_id(1)
    j = pl.program_id(2)
    nt = pl.num_programs(1)
    h0 = F32_ROWS

    @pl.when(j == 0)
    def _():
        g = g_ref[...]
        prev = jnp.where(i > 0, _rms(xp_ref[...], g), 0.0)
        nxt = jnp.where(i < nt - 1, _rms(xn_ref[...], g), 0.0)
        next_ref[0:tm] = _rms(x_ref[...], g).astype(BF16)
        next_ref[tm:tm + 2 * h0] = jnp.concatenate([prev, nxt], axis=0).astype(BF16)
        o_ref[...] = x_ref[...]

    pre = jnp.dot(next_ref[...], wi_ref[...], preferred_element_type=F32)
    pre_ref[0:h0] = pre[tm:tm + h0]
    pre_ref[h0:h0 + tm] = pre[0:tm]
    pre_ref[h0 + tm:2 * h0 + tm] = pre[tm + h0:tm + 2 * h0]
    tf = wo_ref.shape[0]

    def conv(c):
        return (cw_ref[0:1, c] * pre_ref[h0 - 1:h0 - 1 + tm, c] + cw_ref[1:2, c] * pre_ref[h0:h0 + tm, c]
                + cw_ref[2:3, c] * pre_ref[h0 + 1:h0 + 1 + tm, c])

    hidden = (jax.nn.silu(conv(slice(0, tf))) * conv(slice(tf, 2 * tf))).astype(BF16)
    o_ref[...] += jnp.dot(hidden, wo_ref[...], preferred_element_type=F32)


def _chunk_interleave(w, tf):
    lead = w.shape[:-1]
    nj = w.shape[-1] // (2 * tf)
    return jnp.swapaxes(w.reshape(lead + (2, nj, tf)), -3, -2).reshape(lead + (2 * nj * tf,))


def _ffn(x1, g, w_in_b, conv_w, w_out_b, tm, tf):
    b, t, d = x1.shape
    d_ff = w_out_b.shape[0]
    nj = d_ff // tf
    assert d_ff % tf == 0 and t % tm == 0 and tm % F32_ROWS == 0
    per = tm // F32_ROWS
    last = t // F32_ROWS - 1
    return pl.pallas_call(
        functools.partial(_ffn_kernel, tm=tm),
        grid=(b, t // tm, nj),
        in_specs=[
            pl.BlockSpec((None, tm, d), lambda bi, i, j: (bi, i, 0)),
            pl.BlockSpec((None, F32_ROWS, d), lambda bi, i, j: (bi, jnp.maximum(i * per - 1, 0), 0)),
            pl.BlockSpec((None, F32_ROWS, d),
                         lambda bi, i, j: (bi, jnp.minimum((i + 1) * per, last), 0)),
            pl.BlockSpec((1, d), lambda bi, i, j: (0, 0)),
            pl.BlockSpec((d, 2 * tf), lambda bi, i, j: (0, j)),
            pl.BlockSpec((3, 2 * tf), lambda bi, i, j: (0, j)),
            pl.BlockSpec((tf, d), lambda bi, i, j: (j, 0)),
        ],
        out_specs=pl.BlockSpec((None, tm, d), lambda bi, i, j: (bi, i, 0)),
        out_shape=jax.ShapeDtypeStruct((b, t, d), F32),
        scratch_shapes=[pltpu.VMEM((tm + 2 * F32_ROWS, d), BF16),
                        pltpu.VMEM((tm + 2 * F32_ROWS, 2 * tf), F32)],
        compiler_params=pltpu.CompilerParams(
            dimension_semantics=("parallel", "parallel", "arbitrary"),
            vmem_limit_bytes=VMEM_LIMIT),
        name="ffn",
    )(x1, x1, x1, g, w_in_b, conv_w, w_out_b)


_TM_PROJ = 512
_TILE_ROWS = 8
_TM_FFN = 512
_TF_FFN = 512


def _layer(x, mem, p):
    mk, mv = _mem_kv(mem, p["mem_norm_g"], p["w_mem_kv"], p["mem_k_gain"])
    q, k, v, ch, bb, qm = _in_proj(x, p["g_mix"], p["w_in"], p["na_q_gain"], p["na_k_gain"],
                                   p["mem_q_gain"], _TM_PROJ)
    y = _mixers(q, k, v, ch, bb, qm, mk, mv, p["bias_tab"], p["conv_w"], _TILE_ROWS)
    x1 = _out_proj(x, y, p["w_out"], _TM_PROJ)
    return _ffn(x1, p["g_ffn"], p["w_ffn_in"], p["ffn_conv_w"], p["w_ffn_out"], _TM_FFN, _TF_FFN)


def kernel(x_prompt, x_sample, mem_prompt, mem_sample, g_mix, w_in, na_q_gain, na_k_gain,
           na_rel_bias, conv_w, mem_norm_g, w_mem_kv, mem_q_gain, mem_k_gain, w_out,
           g_ffn, w_ffn_in, ffn_conv_w, w_ffn_out):
    y_prompt, y_sample = x_prompt, x_sample
    for l in range(g_mix.shape[0]):
        row = lambda a: a[l].reshape(1, -1).astype(F32)
        q_scale = 1.0 / math.sqrt(HEAD_DIM)
        p = dict(
            g_mix=row(g_mix), g_ffn=row(g_ffn), mem_norm_g=row(mem_norm_g),
            na_q_gain=jnp.tile(row(na_q_gain), (1, 2)) * q_scale,
            na_k_gain=jnp.tile(row(na_k_gain), (1, 2)),
            mem_q_gain=row(mem_q_gain), mem_k_gain=row(mem_k_gain),
            w_in=w_in[l].astype(BF16), w_mem_kv=w_mem_kv[l].astype(F32),
            w_out=w_out[l].astype(BF16),
            w_ffn_in=_chunk_interleave(w_ffn_in[l].astype(BF16), _TF_FFN),
            w_ffn_out=w_ffn_out[l].astype(BF16),
            conv_w=conv_w[l].astype(F32),
            ffn_conv_w=_chunk_interleave(ffn_conv_w[l].astype(F32), _TF_FFN),
            bias_tab=_bias_table(na_rel_bias[l]),
        )
        y_prompt = _layer(y_prompt, mem_prompt, p)
        y_sample = _layer(y_sample, mem_sample, p)
    return (y_prompt, y_sample)
```

```python
import functools
import math

import jax
import jax.numpy as jnp
import numpy as np
from jax import lax
from jax.experimental import pallas as pl
from jax.experimental.pallas import tpu as pltpu

F32 = jnp.float32
BF16 = jnp.bfloat16

GRID_W = 64
WIN_H = 8
WIN_W = 16
HEAD_DIM = 64
NA_HEADS = 16
NA_WIDTH = NA_HEADS * HEAD_DIM
CONV_WIDTH = 512
MEM_HEADS = 4
MEM_HEAD_DIM = 128
MEM_WIDTH = MEM_HEADS * MEM_HEAD_DIM
EPS = 1e-6
NEG_INF = -1e30

LANES = 128
BF16_ROWS = 16
F32_ROWS = 8
VMEM_LIMIT = 56 * 1024 * 1024

HEAD_PAIRS = NA_HEADS // 2
KEY_ROWS_HALO = WIN_H // 2
N_DR = 2 * WIN_H - 1


def _rms(x, g):
    return x * lax.rsqrt(jnp.mean(x * x, axis=-1, keepdims=True) + EPS) * g


def _lo_lanes():
    return lax.broadcasted_iota(jnp.int32, (1, LANES), 1) < HEAD_DIM


def _pair_rms(z, g2):
    lo = _lo_lanes()
    sq = z * z
    tot = jnp.sum(sq, axis=-1, keepdims=True)
    s0 = jnp.sum(jnp.where(lo, sq, 0.0), axis=-1, keepdims=True)
    ms = jnp.where(lo, s0, tot - s0) * (1.0 / HEAD_DIM)
    return z * lax.rsqrt(ms + EPS) * g2


def _softmax_pv(s, v):
    m = jnp.max(s, axis=-1, keepdims=True)
    e = jnp.exp(s - m)
    l = jnp.sum(e, axis=-1, keepdims=True)
    o = jnp.dot(e.astype(BF16), v, preferred_element_type=F32)
    return o * (1.0 / l)


def _mem_kv_kernel(mem_ref, g_ref, w_ref, kg_ref, mk_ref, mv_ref):
    n = _rms(mem_ref[...], g_ref[...]).astype(BF16)
    z = jnp.dot(n, w_ref[...], preferred_element_type=F32)
    for h in range(MEM_HEADS):
        c = slice(h * MEM_HEAD_DIM, (h + 1) * MEM_HEAD_DIM)
        mk_ref[:, c] = _rms(z[:, c], kg_ref[...]).astype(BF16)
    mv_ref[...] = z[:, MEM_WIDTH:].astype(BF16)


def _mem_kv(mem, g, w_b, k_gain):
    b, m, d = mem.shape
    out = jax.ShapeDtypeStruct((b, m, MEM_WIDTH), BF16)
    return pl.pallas_call(
        _mem_kv_kernel,
        grid=(b,),
        in_specs=[
            pl.BlockSpec((None, m, d), lambda i: (i, 0, 0)),
            pl.BlockSpec((1, d), lambda i: (0, 0)),
            pl.BlockSpec((d, 2 * MEM_WIDTH), lambda i: (0, 0)),
            pl.BlockSpec((1, MEM_HEAD_DIM), lambda i: (0, 0)),
        ],
        out_specs=[pl.BlockSpec((None, m, MEM_WIDTH), lambda i: (i, 0, 0))] * 2,
        out_shape=[out, out],
        compiler_params=pltpu.CompilerParams(
            dimension_semantics=("parallel",), vmem_limit_bytes=VMEM_LIMIT),
        name="mem_kv",
    )(mem, g, w_b, k_gain)


_Q0, _K0, _V0 = 0, NA_WIDTH, 2 * NA_WIDTH
_H0 = 3 * NA_WIDTH
_B0 = _H0 + CONV_WIDTH
_C0 = _B0 + CONV_WIDTH
_QM0 = _C0 + CONV_WIDTH
_PROJ_CHUNK = 512


def _in_proj_kernel(x_ref, g_ref, w_ref, qg_ref, kg_ref, mqg_ref,
                    q_ref, k_ref, v_ref, p_ref, b_ref, qm_ref):
    n = _rms(x_ref[...], g_ref[...]).astype(BF16)

    def proj(c0):
        return jnp.dot(n, w_ref[:, c0:c0 + _PROJ_CHUNK], preferred_element_type=F32)

    for base, gain_ref, out_ref in ((_Q0, qg_ref, q_ref), (_K0, kg_ref, k_ref)):
        for cc in range(NA_WIDTH // _PROJ_CHUNK):
            z = proj(base + cc * _PROJ_CHUNK)
            for s in range(_PROJ_CHUNK // LANES):
                o = cc * _PROJ_CHUNK + s * LANES
                out_ref[:, o:o + LANES] = _pair_rms(
                    z[:, s * LANES:(s + 1) * LANES], gain_ref[...]).astype(BF16)
    for cc in range(NA_WIDTH // _PROJ_CHUNK):
        o = cc * _PROJ_CHUNK
        v_ref[:, o:o + _PROJ_CHUNK] = proj(_V0 + o).astype(BF16)
    p_ref[...] = (proj(_C0) * proj(_H0)).astype(BF16)
    b_ref[...] = proj(_B0).astype(BF16)
    z = proj(_QM0)
    for h in range(MEM_HEADS):
        c = slice(h * MEM_HEAD_DIM, (h + 1) * MEM_HEAD_DIM)
        qm_ref[:, c] = _rms(z[:, c], mqg_ref[...]).astype(BF16)


def _in_proj(x, g, w_b, q_gain2, k_gain2, mq_gain, tm):
    b, t, d = x.shape
    nt = t // tm
    row = lambda w: pl.BlockSpec((None, tm, w), lambda bi, i: (bi, i, 0))
    const = lambda shape: pl.BlockSpec(shape, lambda bi, i: (0,) * len(shape))
    sds = lambda w: jax.ShapeDtypeStruct((b, t, w), BF16)
    return pl.pallas_call(
        _in_proj_kernel,
        grid=(b, nt),
        in_specs=[row(d), const((1, d)), const(w_b.shape),
                  const((1, LANES)), const((1, LANES)), const((1, MEM_HEAD_DIM))],
        out_specs=[row(NA_WIDTH)] * 3 + [row(CONV_WIDTH)] * 2 + [row(MEM_WIDTH)],
        out_shape=[sds(NA_WIDTH)] * 3 + [sds(CONV_WIDTH)] * 2 + [sds(MEM_WIDTH)],
        compiler_params=pltpu.CompilerParams(
            dimension_semantics=("parallel", "parallel"), vmem_limit_bytes=VMEM_LIMIT),
        name="in_proj",
    )(x, g, w_b, q_gain2, k_gain2, mq_gain)


COL_BLOCKS = GRID_W // WIN_W
Q_BLOCKS = GRID_W // WIN_W
SLAB = 2 * WIN_W
_NEEDED_COL_BLOCKS = tuple(
    tuple(cb for cb in range(COL_BLOCKS)
          if any(max(0, min(q - WIN_W // 2, GRID_W - WIN_W)) < (cb + 1) * WIN_W
                 and max(0, min(q - WIN_W // 2, GRID_W - WIN_W)) + WIN_W > cb * WIN_W
                 for q in range(j * WIN_W, (j + 1) * WIN_W)))
    for j in range(Q_BLOCKS))


_SCORE_BLOCKS = tuple((j, cb) for j in range(Q_BLOCKS) for cb in _NEEDED_COL_BLOCKS[j])


def _bias_table(rel_bias):
    h = rel_bias.shape[0]
    n_dc = 2 * WIN_W - 1
    n_blk = len(_SCORE_BLOCKS)
    cr = np.arange(WIN_W)[:, None]
    cl = np.arange(WIN_W)[None, :]
    select = np.stack([(WIN_W * (x - 1) + cl - cr + WIN_W - 1)[..., None] == np.arange(n_dc)
                       for x in range(3)]).astype(np.float32)
    mask = np.zeros((n_blk, 2, WIN_W, WIN_H, WIN_W), np.float32)
    for n, (j, cb) in enumerate(_SCORE_BLOCKS):
        q, k = j * WIN_W + cr, cb * WIN_W + cl
        q_cs = np.clip(q - WIN_W // 2, 0, GRID_W - WIN_W)
        mask[n] = np.where((k >= q_cs) & (k < q_cs + WIN_W), 0.0, NEG_INF)[None, :, None, :]
    rb = rel_bias.astype(F32).reshape(h // 2, 2, N_DR, n_dc)
    pieces = jnp.einsum("pedt,xrlt->pedxrl", rb, select, precision=lax.Precision.HIGHEST)
    pieces = jnp.pad(pieces, ((0, 0),) * 5 + ((0, LANES - WIN_W),))
    return pl.pallas_call(
        _bias_expand_kernel,
        grid=(h // 2,),
        in_specs=[pl.BlockSpec((None,) + pieces.shape[1:], lambda p: (p, 0, 0, 0, 0, 0)),
                  pl.BlockSpec((n_blk, SLAB, LANES), lambda p: (0, 0, 0))],
        out_specs=pl.BlockSpec((WIN_H, None, n_blk, SLAB, LANES), lambda p: (0, p, 0, 0, 0)),
        out_shape=jax.ShapeDtypeStruct((WIN_H, h // 2, n_blk, SLAB, LANES), F32),
        name="bias_expand",
    )(pieces, mask.reshape(n_blk, SLAB, LANES))


def _bias_expand_kernel(piece_ref, mask_ref, o_ref):
    for n, (j, cb) in enumerate(_SCORE_BLOCKS):
        for d0 in range(WIN_H):
            halves = []
            for hd in range(2):
                acc = piece_ref[hd, d0, cb - j + 1]
                for wr in range(1, WIN_H):
                    acc = acc + pltpu.roll(piece_ref[hd, d0 + wr, cb - j + 1], wr * WIN_W, axis=1)
                halves.append(acc)
            o_ref[d0, n] = jnp.concatenate(halves, axis=0) + mask_ref[n]


def _mixers_kernel(q_ref, k_ref, kp_ref, kn_ref, v_ref, vp_ref, vn_ref,
                   p_ref, pp_ref, pn_ref, b_ref, qm_ref, mk_ref, mv_ref, bt_ref, cw_ref,
                   y_ref, kext, vext, pext, s0_ref, s1_ref, *, rows, tile_rows):
    i = pl.program_id(1)
    nt = pl.num_programs(1)
    tq = tile_rows * GRID_W
    halo = KEY_ROWS_HALO * GRID_W
    lo = _lo_lanes()
    n_hp = 2 * GRID_W

    for ext, prev, main, nxt in ((kext, kp_ref, k_ref, kn_ref), (vext, vp_ref, v_ref, vn_ref)):
        ext[0:halo] = prev[...]
        ext[halo:halo + tq] = main[...]
        ext[halo + tq:halo + tq + halo] = nxt[...]

    r0 = i * tile_rows

    def window(ext, start, c):
        return jnp.concatenate(
            [ext[pl.ds(pl.multiple_of(start + wr * GRID_W + cb * WIN_W, WIN_W), WIN_W), c]
             for cb in range(COL_BLOCKS) for wr in range(WIN_H)], axis=0)

    def row_geometry(lr):
        r = r0 + lr
        ws = jnp.clip(r - WIN_H // 2, 0, rows - WIN_H)
        start = pl.multiple_of((ws - r0 + KEY_ROWS_HALO) * GRID_W, GRID_W)
        dr0 = ws - r + (WIN_H - 1)
        return start, pl.multiple_of(lr * GRID_W, GRID_W), dr0

    def scores(lr, s_ref):
        start, qrow, dr0 = row_geometry(lr)
        zero = jnp.zeros((WIN_W, LANES), BF16)
        for quad in range(HEAD_PAIRS // 2):
            c2 = slice(2 * quad * LANES, (2 * quad + 2) * LANES)
            q4 = q_ref[pl.ds(qrow, GRID_W), c2]
            pieces = []
            for half in range(2):
                for j in range(Q_BLOCKS):
                    qj = q4[j * WIN_W:(j + 1) * WIN_W, half * LANES:(half + 1) * LANES]
                    for piece in (jnp.where(lo, qj, zero), jnp.where(lo, zero, qj)):
                        pieces.append(jnp.concatenate(
                            [piece, zero] if half == 0 else [zero, piece], axis=1))
            s = lax.dot_general(jnp.concatenate(pieces, axis=0), window(kext, start, c2),
                                (((1,), (1,)), ((), ())), preferred_element_type=F32)
            for half in range(2):
                hp = 2 * quad + half
                for n, (j, cb) in enumerate(_SCORE_BLOCKS):
                    rs = slice(j * SLAB, (j + 1) * SLAB)
                    cs = slice(cb * LANES, (cb + 1) * LANES)
                    s_ref[hp, rs, cs] = (s[half * n_hp + j * SLAB:half * n_hp + (j + 1) * SLAB, cs]
                                         + bt_ref[dr0, hp, n])

    def attend(lr, s_ref):
        start, qrow, _ = row_geometry(lr)
        for quad in range(HEAD_PAIRS // 2):
            c2 = slice(2 * quad * LANES, (2 * quad + 2) * LANES)
            p_rows, inv_l = [], []
            for half in range(2):
                hp = 2 * quad + half
                for j in range(Q_BLOCKS):
                    rs = slice(j * SLAB, (j + 1) * SLAB)
                    need = _NEEDED_COL_BLOCKS[j]
                    sb = [s_ref[hp, rs, cb * LANES:(cb + 1) * LANES] for cb in need]
                    m = jnp.max(functools.reduce(jnp.maximum, sb), axis=-1, keepdims=True)
                    e = [jnp.exp(x - m) for x in sb]
                    inv_l.append(
                        1.0 / jnp.sum(functools.reduce(jnp.add, e), axis=-1, keepdims=True))
                    zero = jnp.zeros((SLAB, LANES), BF16)
                    blocks = [zero] * COL_BLOCKS
                    for cb, x in zip(need, e):
                        blocks[cb] = x.astype(BF16)
                    p_rows.append(jnp.concatenate(blocks, axis=1))
            o = jnp.dot(jnp.concatenate(p_rows, axis=0), window(vext, start, c2),
                        preferred_element_type=F32)
            for half in range(2):
                c = slice((2 * quad + half) * LANES, (2 * quad + half + 1) * LANES)
                for j in range(Q_BLOCKS):
                    r_lo = half * n_hp + j * SLAB
                    oj = o[r_lo:r_lo + SLAB, half * LANES:(half + 1) * LANES]
                    oj = oj * inv_l[half * Q_BLOCKS + j]
                    y_ref[pl.ds(pl.multiple_of(qrow + j * WIN_W, WIN_W), WIN_W), c] = jnp.where(
                        lo, oj[:WIN_W], oj[WIN_W:]).astype(BF16)

    def gated_conv():
        pext[0:F32_ROWS] = jnp.where(i > 0, pp_ref[...].astype(F32)[F32_ROWS:], 0.0)
        pext[F32_ROWS:F32_ROWS + tq] = p_ref[...].astype(F32)
        pext[F32_ROWS + tq:2 * F32_ROWS + tq] = jnp.where(
            i < nt - 1, pn_ref[...].astype(F32)[:F32_ROWS], 0.0)
        conv = (cw_ref[0:1] * pext[F32_ROWS - 1:F32_ROWS - 1 + tq]
                + cw_ref[1:2] * pext[F32_ROWS:F32_ROWS + tq]
                + cw_ref[2:3] * pext[F32_ROWS + 1:F32_ROWS + 1 + tq])
        y_ref[:, NA_WIDTH:NA_WIDTH + CONV_WIDTH] = (b_ref[...].astype(F32) * conv).astype(BF16)

    def mem_attention(h):
        c = slice(h * MEM_HEAD_DIM, (h + 1) * MEM_HEAD_DIM)
        s = lax.dot_general(qm_ref[:, c], mk_ref[:, c], (((1,), (1,)), ((), ())),
                            preferred_element_type=F32) * (1.0 / math.sqrt(MEM_HEAD_DIM))
        o0 = NA_WIDTH + CONV_WIDTH + h * MEM_HEAD_DIM
        y_ref[:, o0:o0 + MEM_HEAD_DIM] = _softmax_pv(s, mv_ref[:, c]).astype(BF16)

    scores(0, s0_ref)

    def row_pair(k, carry):
        lr = 2 * k
        scores(lr + 1, s1_ref)
        attend(lr, s0_ref)
        scores(lr + 2, s0_ref)
        attend(lr + 1, s1_ref)
        return carry

    lax.fori_loop(0, tile_rows // 2 - 1, row_pair, 0)
    scores(tile_rows - 1, s1_ref)
    attend(tile_rows - 2, s0_ref)
    attend(tile_rows - 1, s1_ref)
    gated_conv()
    for h in range(MEM_HEADS):
        mem_attention(h)


def _mixers(q, k, v, p, bb, qm, mk, mv, bias_tab, conv_w, tile_rows):
    b, t, _ = q.shape
    rows = t // GRID_W
    tq = tile_rows * GRID_W
    nt = t // tq
    halo = KEY_ROWS_HALO * GRID_W
    assert rows >= WIN_H and rows % tile_rows == 0 and tile_rows >= KEY_ROWS_HALO
    assert tile_rows % 2 == 0
    assert tq % halo == 0 and tq % BF16_ROWS == 0
    n_mem = mk.shape[1]

    def main(w):
        return pl.BlockSpec((None, tq, w), lambda bi, i: (bi, i, 0))

    def prev(blk, w):
        per = tq // blk
        return pl.BlockSpec((None, blk, w), lambda bi, i: (bi, jnp.maximum(i * per - 1, 0), 0))

    def nxt(blk, w):
        per = tq // blk
        last = t // blk - 1
        return pl.BlockSpec((None, blk, w), lambda bi, i: (bi, jnp.minimum((i + 1) * per, last), 0))

    kv_specs = [main(NA_WIDTH), prev(halo, NA_WIDTH), nxt(halo, NA_WIDTH)]
    mem_spec = pl.BlockSpec((None, n_mem, MEM_WIDTH), lambda bi, i: (bi, 0, 0))
    d_mix = NA_WIDTH + CONV_WIDTH + MEM_WIDTH
    return pl.pallas_call(
        functools.partial(_mixers_kernel, rows=rows, tile_rows=tile_rows),
        grid=(b, nt),
        in_specs=[main(NA_WIDTH)] + kv_specs + kv_specs
                 + [main(CONV_WIDTH), prev(BF16_ROWS, CONV_WIDTH), nxt(BF16_ROWS, CONV_WIDTH),
                    main(CONV_WIDTH), main(MEM_WIDTH), mem_spec, mem_spec,
                    pl.BlockSpec(bias_tab.shape, lambda bi, i: (0,) * bias_tab.ndim,
                                 pipeline_mode=pl.Buffered(1)),
                    pl.BlockSpec(conv_w.shape, lambda bi, i: (0, 0))],
        out_specs=main(d_mix),
        out_shape=jax.ShapeDtypeStruct((b, t, d_mix), BF16),
        scratch_shapes=[pltpu.VMEM((tq + 2 * halo, NA_WIDTH), BF16),
                        pltpu.VMEM((tq + 2 * halo, NA_WIDTH), BF16),
                        pltpu.VMEM((tq + 2 * F32_ROWS, CONV_WIDTH), F32),
                        pltpu.VMEM((HEAD_PAIRS, 2 * GRID_W, WIN_H * GRID_W), F32),
                        pltpu.VMEM((HEAD_PAIRS, 2 * GRID_W, WIN_H * GRID_W), F32)],
        compiler_params=pltpu.CompilerParams(
            dimension_semantics=("parallel", "parallel"), vmem_limit_bytes=VMEM_LIMIT),
        name="mixers",
    )(q, k, k, k, v, v, v, p, p, p, bb, qm, mk, mv, bias_tab, conv_w)


def _out_proj_kernel(x_ref, y_ref, w_ref, o_ref):
    o_ref[...] = x_ref[...] + jnp.dot(y_ref[...], w_ref[...], preferred_element_type=F32)


def _out_proj(x, y, w_b, tm):
    b, t, d = x.shape
    row = lambda w: pl.BlockSpec((None, tm, w), lambda bi, i: (bi, i, 0))
    return pl.pallas_call(
        _out_proj_kernel,
        grid=(b, t // tm),
        in_specs=[row(d), row(y.shape[-1]), pl.BlockSpec(w_b.shape, lambda bi, i: (0, 0))],
        out_specs=row(d),
        out_shape=jax.ShapeDtypeStruct((b, t, d), F32),
        compiler_params=pltpu.CompilerParams(
            dimension_semantics=("parallel", "parallel"), vmem_limit_bytes=VMEM_LIMIT),
        name="out_proj",
    )(x, y, w_b)


def _ffn_kernel(x_ref, xp_ref, xn_ref, g_ref, wa_ref, wg_ref, cwa_ref, cwg_ref, wo_ref,
                o_ref, next_ref, pa_ref, pg_ref, *, tm):
    i = pl.program_id(1)
    j = pl.program_id(2)
    nt = pl.num_programs(1)
    h0 = F32_ROWS

    @pl.when(j == 0)
    def _():
        g = g_ref[...]
        prev = jnp.where(i > 0, _rms(xp_ref[...], g), 0.0)
        nxt = jnp.where(i < nt - 1, _rms(xn_ref[...], g), 0.0)
        next_ref[0:tm] = _rms(x_ref[...], g).astype(BF16)
        next_ref[tm:tm + 2 * h0] = jnp.concatenate([prev, nxt], axis=0).astype(BF16)
        o_ref[...] = x_ref[...]

    n2 = next_ref[...]

    def up(w_ref, dst_ref):
        pre = jnp.dot(n2, w_ref[...], preferred_element_type=F32)
        dst_ref[0:h0] = pre[tm:tm + h0]
        dst_ref[h0:h0 + tm] = pre[0:tm]
        dst_ref[h0 + tm:2 * h0 + tm] = pre[tm + h0:tm + 2 * h0]

    up(wa_ref, pa_ref)
    up(wg_ref, pg_ref)

    def conv(ref, cw):
        return (cw[0:1] * ref[h0 - 1:h0 - 1 + tm] + cw[1:2] * ref[h0:h0 + tm]
                + cw[2:3] * ref[h0 + 1:h0 + 1 + tm])

    hidden = (jax.nn.silu(conv(pa_ref, cwa_ref)) * conv(pg_ref, cwg_ref)).astype(BF16)
    o_ref[...] += jnp.dot(hidden, wo_ref[...], preferred_element_type=F32)


def _ffn(x1, g, w_in_b, conv_w, w_out_b, tm, tf):
    b, t, d = x1.shape
    d_ff = w_out_b.shape[0]
    nj = d_ff // tf
    assert d_ff % tf == 0 and t % tm == 0 and tm % F32_ROWS == 0
    per = tm // F32_ROWS
    last = t // F32_ROWS - 1
    return pl.pallas_call(
        functools.partial(_ffn_kernel, tm=tm),
        grid=(b, t // tm, nj),
        in_specs=[
            pl.BlockSpec((None, tm, d), lambda bi, i, j: (bi, i, 0)),
            pl.BlockSpec((None, F32_ROWS, d), lambda bi, i, j: (bi, jnp.maximum(i * per - 1, 0), 0)),
            pl.BlockSpec((None, F32_ROWS, d),
                         lambda bi, i, j: (bi, jnp.minimum((i + 1) * per, last), 0)),
            pl.BlockSpec((1, d), lambda bi, i, j: (0, 0)),
            pl.BlockSpec((d, tf), lambda bi, i, j: (0, j)),
            pl.BlockSpec((d, tf), lambda bi, i, j: (0, nj + j)),
            pl.BlockSpec((3, tf), lambda bi, i, j: (0, j)),
            pl.BlockSpec((3, tf), lambda bi, i, j: (0, nj + j)),
            pl.BlockSpec((tf, d), lambda bi, i, j: (j, 0)),
        ],
        out_specs=pl.BlockSpec((None, tm, d), lambda bi, i, j: (bi, i, 0)),
        out_shape=jax.ShapeDtypeStruct((b, t, d), F32),
        scratch_shapes=[pltpu.VMEM((tm + 2 * F32_ROWS, d), BF16),
                        pltpu.VMEM((tm + 2 * F32_ROWS, tf), F32),
                        pltpu.VMEM((tm + 2 * F32_ROWS, tf), F32)],
        compiler_params=pltpu.CompilerParams(
            dimension_semantics=("parallel", "parallel", "arbitrary"),
            vmem_limit_bytes=VMEM_LIMIT),
        name="ffn",
    )(x1, x1, x1, g, w_in_b, w_in_b, conv_w, conv_w, w_out_b)


_TM_PROJ = 512
_TILE_ROWS = 8
_TM_FFN = 512
_TF_FFN = 512


def _layer(x, mem, p):
    mk, mv = _mem_kv(mem, p["mem_norm_g"], p["w_mem_kv"], p["mem_k_gain"])
    q, k, v, ch, bb, qm = _in_proj(x, p["g_mix"], p["w_in"], p["na_q_gain"], p["na_k_gain"],
                                   p["mem_q_gain"], _TM_PROJ)
    y = _mixers(q, k, v, ch, bb, qm, mk, mv, p["bias_tab"], p["conv_w"], _TILE_ROWS)
    x1 = _out_proj(x, y, p["w_out"], _TM_PROJ)
    return _ffn(x1, p["g_ffn"], p["w_ffn_in"], p["ffn_conv_w"], p["w_ffn_out"], _TM_FFN, _TF_FFN)


def kernel(x_prompt, x_sample, mem_prompt, mem_sample, g_mix, w_in, na_q_gain, na_k_gain,
           na_rel_bias, conv_w, mem_norm_g, w_mem_kv, mem_q_gain, mem_k_gain, w_out,
           g_ffn, w_ffn_in, ffn_conv_w, w_ffn_out):
    y_prompt, y_sample = x_prompt, x_sample
    for l in range(g_mix.shape[0]):
        row = lambda a: a[l].reshape(1, -1).astype(F32)
        q_scale = 1.0 / math.sqrt(HEAD_DIM)
        p = dict(
            g_mix=row(g_mix), g_ffn=row(g_ffn), mem_norm_g=row(mem_norm_g),
            na_q_gain=jnp.tile(row(na_q_gain), (1, 2)) * q_scale,
            na_k_gain=jnp.tile(row(na_k_gain), (1, 2)),
            mem_q_gain=row(mem_q_gain), mem_k_gain=row(mem_k_gain),
            w_in=w_in[l].astype(BF16), w_mem_kv=w_mem_kv[l].astype(BF16),
            w_out=w_out[l].astype(BF16), w_ffn_in=w_ffn_in[l].astype(BF16),
            w_ffn_out=w_ffn_out[l].astype(BF16),
            conv_w=conv_w[l].astype(F32), ffn_conv_w=ffn_conv_w[l].astype(F32),
            bias_tab=_bias_table(na_rel_bias[l]),
        )
        y_prompt = _layer(y_prompt, mem_prompt, p)
        y_sample = _layer(y_sample, mem_sample, p)
    return (y_prompt, y_sample)
```

```python
import functools
import math

import jax
import jax.numpy as jnp
import numpy as np
from jax import lax
from jax.experimental import pallas as pl
from jax.experimental.pallas import tpu as pltpu

F32 = jnp.float32
BF16 = jnp.bfloat16

GRID_W = 64
WIN_H = 8
WIN_W = 16
HEAD_DIM = 64
NA_HEADS = 16
NA_WIDTH = NA_HEADS * HEAD_DIM
CONV_WIDTH = 512
MEM_HEADS = 4
MEM_HEAD_DIM = 128
MEM_WIDTH = MEM_HEADS * MEM_HEAD_DIM
EPS = 1e-6
NEG_INF = -1e30

LANES = 128
BF16_ROWS = 16
F32_ROWS = 8
VMEM_LIMIT = 56 * 1024 * 1024

HEAD_PAIRS = NA_HEADS // 2
KEY_ROWS_HALO = WIN_H // 2
N_DR = 2 * WIN_H - 1


def _rms(x, g):
    return x * lax.rsqrt(jnp.mean(x * x, axis=-1, keepdims=True) + EPS) * g


def _lo_lanes():
    return lax.broadcasted_iota(jnp.int32, (1, LANES), 1) < HEAD_DIM


def _pair_rms(z, g2):
    lo = _lo_lanes()
    sq = z * z
    tot = jnp.sum(sq, axis=-1, keepdims=True)
    s0 = jnp.sum(jnp.where(lo, sq, 0.0), axis=-1, keepdims=True)
    ms = jnp.where(lo, s0, tot - s0) * (1.0 / HEAD_DIM)
    return z * lax.rsqrt(ms + EPS) * g2


def _softmax_pv(s, v):
    m = jnp.max(s, axis=-1, keepdims=True)
    e = jnp.exp(s - m)
    l = jnp.sum(e, axis=-1, keepdims=True)
    o = jnp.dot(e.astype(BF16), v, preferred_element_type=F32)
    return o * (1.0 / l)


def _mem_kv_kernel(mem_ref, g_ref, w_ref, kg_ref, mk_ref, mv_ref):
    n = _rms(mem_ref[...], g_ref[...]).astype(BF16)
    z = jnp.dot(n, w_ref[...], preferred_element_type=F32)
    for h in range(MEM_HEADS):
        c = slice(h * MEM_HEAD_DIM, (h + 1) * MEM_HEAD_DIM)
        mk_ref[:, c] = _rms(z[:, c], kg_ref[...]).astype(BF16)
    mv_ref[...] = z[:, MEM_WIDTH:].astype(BF16)


def _mem_kv(mem, g, w_b, k_gain):
    b, m, d = mem.shape
    out = jax.ShapeDtypeStruct((b, m, MEM_WIDTH), BF16)
    return pl.pallas_call(
        _mem_kv_kernel,
        grid=(b,),
        in_specs=[
            pl.BlockSpec((None, m, d), lambda i: (i, 0, 0)),
            pl.BlockSpec((1, d), lambda i: (0, 0)),
            pl.BlockSpec((d, 2 * MEM_WIDTH), lambda i: (0, 0)),
            pl.BlockSpec((1, MEM_HEAD_DIM), lambda i: (0, 0)),
        ],
        out_specs=[pl.BlockSpec((None, m, MEM_WIDTH), lambda i: (i, 0, 0))] * 2,
        out_shape=[out, out],
        compiler_params=pltpu.CompilerParams(
            dimension_semantics=("parallel",), vmem_limit_bytes=VMEM_LIMIT),
        name="mem_kv",
    )(mem, g, w_b, k_gain)


_Q0, _K0, _V0 = 0, NA_WIDTH, 2 * NA_WIDTH
_H0 = 3 * NA_WIDTH
_B0 = _H0 + CONV_WIDTH
_C0 = _B0 + CONV_WIDTH
_QM0 = _C0 + CONV_WIDTH
_PROJ_CHUNK = 512


def _in_proj_kernel(x_ref, g_ref, w_ref, qg_ref, kg_ref, mqg_ref,
                    q_ref, k_ref, v_ref, p_ref, b_ref, qm_ref):
    n = _rms(x_ref[...], g_ref[...]).astype(BF16)

    def proj(c0):
        return jnp.dot(n, w_ref[:, c0:c0 + _PROJ_CHUNK], preferred_element_type=F32)

    for base, gain_ref, out_ref in ((_Q0, qg_ref, q_ref), (_K0, kg_ref, k_ref)):
        for cc in range(NA_WIDTH // _PROJ_CHUNK):
            z = proj(base + cc * _PROJ_CHUNK)
            for s in range(_PROJ_CHUNK // LANES):
                o = cc * _PROJ_CHUNK + s * LANES
                out_ref[:, o:o + LANES] = _pair_rms(
                    z[:, s * LANES:(s + 1) * LANES], gain_ref[...]).astype(BF16)
    for cc in range(NA_WIDTH // _PROJ_CHUNK):
        o = cc * _PROJ_CHUNK
        v_ref[:, o:o + _PROJ_CHUNK] = proj(_V0 + o).astype(BF16)
    p_ref[...] = (proj(_C0) * proj(_H0)).astype(BF16)
    b_ref[...] = proj(_B0).astype(BF16)
    z = proj(_QM0)
    for h in range(MEM_HEADS):
        c = slice(h * MEM_HEAD_DIM, (h + 1) * MEM_HEAD_DIM)
        qm_ref[:, c] = _rms(z[:, c], mqg_ref[...]).astype(BF16)


def _in_proj(x, g, w_b, q_gain2, k_gain2, mq_gain, tm):
    b, t, d = x.shape
    nt = t // tm
    row = lambda w: pl.BlockSpec((None, tm, w), lambda bi, i: (bi, i, 0))
    const = lambda shape: pl.BlockSpec(shape, lambda bi, i: (0,) * len(shape))
    sds = lambda w: jax.ShapeDtypeStruct((b, t, w), BF16)
    return pl.pallas_call(
        _in_proj_kernel,
        grid=(b, nt),
        in_specs=[row(d), const((1, d)), const(w_b.shape),
                  const((1, LANES)), const((1, LANES)), const((1, MEM_HEAD_DIM))],
        out_specs=[row(NA_WIDTH)] * 3 + [row(CONV_WIDTH)] * 2 + [row(MEM_WIDTH)],
        out_shape=[sds(NA_WIDTH)] * 3 + [sds(CONV_WIDTH)] * 2 + [sds(MEM_WIDTH)],
        compiler_params=pltpu.CompilerParams(
            dimension_semantics=("parallel", "parallel"), vmem_limit_bytes=VMEM_LIMIT),
        name="in_proj",
    )(x, g, w_b, q_gain2, k_gain2, mq_gain)


COL_BLOCKS = GRID_W // WIN_W
Q_BLOCKS = GRID_W // WIN_W
SLAB = 2 * WIN_W
_NEEDED_COL_BLOCKS = tuple(
    tuple(cb for cb in range(COL_BLOCKS)
          if any(max(0, min(q - WIN_W // 2, GRID_W - WIN_W)) < (cb + 1) * WIN_W
                 and max(0, min(q - WIN_W // 2, GRID_W - WIN_W)) + WIN_W > cb * WIN_W
                 for q in range(j * WIN_W, (j + 1) * WIN_W)))
    for j in range(Q_BLOCKS))


_SCORE_BLOCKS = tuple((j, cb) for j in range(Q_BLOCKS) for cb in _NEEDED_COL_BLOCKS[j])


def _bias_table(rel_bias):
    h = rel_bias.shape[0]
    n_dc = 2 * WIN_W - 1
    n_blk = len(_SCORE_BLOCKS)
    cr = np.arange(WIN_W)[:, None]
    cl = np.arange(WIN_W)[None, :]
    select = np.stack([(WIN_W * (x - 1) + cl - cr + WIN_W - 1)[..., None] == np.arange(n_dc)
                       for x in range(3)]).astype(np.float32)
    mask = np.zeros((n_blk, 2, WIN_W, WIN_H, WIN_W), np.float32)
    for n, (j, cb) in enumerate(_SCORE_BLOCKS):
        q, k = j * WIN_W + cr, cb * WIN_W + cl
        q_cs = np.clip(q - WIN_W // 2, 0, GRID_W - WIN_W)
        mask[n] = np.where((k >= q_cs) & (k < q_cs + WIN_W), 0.0, NEG_INF)[None, :, None, :]
    rb = rel_bias.astype(F32).reshape(h // 2, 2, N_DR, n_dc)
    pieces = jnp.einsum("pedt,xrlt->pedxrl", rb, select, precision=lax.Precision.HIGHEST)
    pieces = jnp.pad(pieces, ((0, 0),) * 5 + ((0, LANES - WIN_W),))
    return pl.pallas_call(
        _bias_expand_kernel,
        grid=(h // 2,),
        in_specs=[pl.BlockSpec((None,) + pieces.shape[1:], lambda p: (p, 0, 0, 0, 0, 0)),
                  pl.BlockSpec((n_blk, SLAB, LANES), lambda p: (0, 0, 0))],
        out_specs=pl.BlockSpec((WIN_H, None, n_blk, SLAB, LANES), lambda p: (0, p, 0, 0, 0)),
        out_shape=jax.ShapeDtypeStruct((WIN_H, h // 2, n_blk, SLAB, LANES), F32),
        name="bias_expand",
    )(pieces, mask.reshape(n_blk, SLAB, LANES))


def _bias_expand_kernel(piece_ref, mask_ref, o_ref):
    for n, (j, cb) in enumerate(_SCORE_BLOCKS):
        for d0 in range(WIN_H):
            halves = []
            for hd in range(2):
                acc = piece_ref[hd, d0, cb - j + 1]
                for wr in range(1, WIN_H):
                    acc = acc + pltpu.roll(piece_ref[hd, d0 + wr, cb - j + 1], wr * WIN_W, axis=1)
                halves.append(acc)
            o_ref[d0, n] = jnp.concatenate(halves, axis=0) + mask_ref[n]


def _mixers_kernel(q_ref, k_ref, kp_ref, kn_ref, v_ref, vp_ref, vn_ref,
                   p_ref, pp_ref, pn_ref, b_ref, qm_ref, mk_ref, mv_ref, bt_ref, cw_ref,
                   y_ref, kext, vext, pext, s0_ref, s1_ref, *, rows, tile_rows):
    i = pl.program_id(1)
    nt = pl.num_programs(1)
    tq = tile_rows * GRID_W
    halo = KEY_ROWS_HALO * GRID_W
    lo = _lo_lanes()
    n_hp = 2 * GRID_W

    for ext, prev, main, nxt in ((kext, kp_ref, k_ref, kn_ref), (vext, vp_ref, v_ref, vn_ref)):
        ext[0:halo] = prev[...]
        ext[halo:halo + tq] = main[...]
        ext[halo + tq:halo + tq + halo] = nxt[...]

    r0 = i * tile_rows

    def window(ext, start, c):
        return jnp.concatenate(
            [ext[pl.ds(pl.multiple_of(start + wr * GRID_W + cb * WIN_W, WIN_W), WIN_W), c]
             for cb in range(COL_BLOCKS) for wr in range(WIN_H)], axis=0)

    def row_geometry(lr):
        r = r0 + lr
        ws = jnp.clip(r - WIN_H // 2, 0, rows - WIN_H)
        start = pl.multiple_of((ws - r0 + KEY_ROWS_HALO) * GRID_W, GRID_W)
        dr0 = ws - r + (WIN_H - 1)
        return start, pl.multiple_of(lr * GRID_W, GRID_W), dr0

    def scores(lr, s_ref):
        start, qrow, dr0 = row_geometry(lr)
        zero = jnp.zeros((WIN_W, LANES), BF16)
        for quad in range(HEAD_PAIRS // 2):
            c2 = slice(2 * quad * LANES, (2 * quad + 2) * LANES)
            q4 = q_ref[pl.ds(qrow, GRID_W), c2]
            pieces = []
            for half in range(2):
                for j in range(Q_BLOCKS):
                    qj = q4[j * WIN_W:(j + 1) * WIN_W, half * LANES:(half + 1) * LANES]
                    for piece in (jnp.where(lo, qj, zero), jnp.where(lo, zero, qj)):
                        pieces.append(jnp.concatenate(
                            [piece, zero] if half == 0 else [zero, piece], axis=1))
            s = lax.dot_general(jnp.concatenate(pieces, axis=0), window(kext, start, c2),
                                (((1,), (1,)), ((), ())), preferred_element_type=F32)
            for half in range(2):
                hp = 2 * quad + half
                for n, (j, cb) in enumerate(_SCORE_BLOCKS):
                    rs = slice(j * SLAB, (j + 1) * SLAB)
                    cs = slice(cb * LANES, (cb + 1) * LANES)
                    s_ref[hp, rs, cs] = (s[half * n_hp + j * SLAB:half * n_hp + (j + 1) * SLAB, cs]
                                         + bt_ref[dr0, hp, n])

    def attend(lr, s_ref):
        start, qrow, _ = row_geometry(lr)
        for quad in range(HEAD_PAIRS // 2):
            c2 = slice(2 * quad * LANES, (2 * quad + 2) * LANES)
            p_rows, inv_l = [], []
            for half in range(2):
                hp = 2 * quad + half
                for j in range(Q_BLOCKS):
                    rs = slice(j * SLAB, (j + 1) * SLAB)
                    need = _NEEDED_COL_BLOCKS[j]
                    sb = [s_ref[hp, rs, cb * LANES:(cb + 1) * LANES] for cb in need]
                    m = jnp.max(functools.reduce(jnp.maximum, sb), axis=-1, keepdims=True)
                    e = [jnp.exp(x - m) for x in sb]
                    inv_l.append(
                        1.0 / jnp.sum(functools.reduce(jnp.add, e), axis=-1, keepdims=True))
                    zero = jnp.zeros((SLAB, LANES), BF16)
                    blocks = [zero] * COL_BLOCKS
                    for cb, x in zip(need, e):
                        blocks[cb] = x.astype(BF16)
                    p_rows.append(jnp.concatenate(blocks, axis=1))
            o = jnp.dot(jnp.concatenate(p_rows, axis=0), window(vext, start, c2),
                        preferred_element_type=F32)
            for half in range(2):
                c = slice((2 * quad + half) * LANES, (2 * quad + half + 1) * LANES)
                for j in range(Q_BLOCKS):
                    r_lo = half * n_hp + j * SLAB
                    oj = o[r_lo:r_lo + SLAB, half * LANES:(half + 1) * LANES]
                    oj = oj * inv_l[half * Q_BLOCKS + j]
                    y_ref[pl.ds(pl.multiple_of(qrow + j * WIN_W, WIN_W), WIN_W), c] = jnp.where(
                        lo, oj[:WIN_W], oj[WIN_W:]).astype(BF16)

    def gated_conv():
        pext[0:F32_ROWS] = jnp.where(i > 0, pp_ref[...].astype(F32)[F32_ROWS:], 0.0)
        pext[F32_ROWS:F32_ROWS + tq] = p_ref[...].astype(F32)
        pext[F32_ROWS + tq:2 * F32_ROWS + tq] = jnp.where(
            i < nt - 1, pn_ref[...].astype(F32)[:F32_ROWS], 0.0)
        conv = (cw_ref[0:1] * pext[F32_ROWS - 1:F32_ROWS - 1 + tq]
                + cw_ref[1:2] * pext[F32_ROWS:F32_ROWS + tq]
                + cw_ref[2:3] * pext[F32_ROWS + 1:F32_ROWS + 1 + tq])
        y_ref[:, NA_WIDTH:NA_WIDTH + CONV_WIDTH] = (b_ref[...].astype(F32) * conv).astype(BF16)

    def mem_attention(h):
        c = slice(h * MEM_HEAD_DIM, (h + 1) * MEM_HEAD_DIM)
        s = lax.dot_general(qm_ref[:, c], mk_ref[:, c], (((1,), (1,)), ((), ())),
                            preferred_element_type=F32) * (1.0 / math.sqrt(MEM_HEAD_DIM))
        o0 = NA_WIDTH + CONV_WIDTH + h * MEM_HEAD_DIM
        y_ref[:, o0:o0 + MEM_HEAD_DIM] = _softmax_pv(s, mv_ref[:, c]).astype(BF16)

    scores(0, s0_ref)

    def row_pair(k, carry):
        lr = 2 * k
        scores(lr + 1, s1_ref)
        attend(lr, s0_ref)
        scores(lr + 2, s0_ref)
        attend(lr + 1, s1_ref)
        return carry

    lax.fori_loop(0, tile_rows // 2 - 1, row_pair, 0)
    scores(tile_rows - 1, s1_ref)
    attend(tile_rows - 2, s0_ref)
    attend(tile_rows - 1, s1_ref)
    gated_conv()
    for h in range(MEM_HEADS):
        mem_attention(h)


def _mixers(q, k, v, p, bb, qm, mk, mv, bias_tab, conv_w, tile_rows):
    b, t, _ = q.shape
    rows = t // GRID_W
    tq = tile_rows * GRID_W
    nt = t // tq
    halo = KEY_ROWS_HALO * GRID_W
    assert rows >= WIN_H and rows % tile_rows == 0 and tile_rows >= KEY_ROWS_HALO
    assert tile_rows % 2 == 0
    assert tq % halo == 0 and tq % BF16_ROWS == 0
    n_mem = mk.shape[1]

    def main(w):
        return pl.BlockSpec((None, tq, w), lambda bi, i: (bi, i, 0))

    def prev(blk, w):
        per = tq // blk
        return pl.BlockSpec((None, blk, w), lambda bi, i: (bi, jnp.maximum(i * per - 1, 0), 0))

    def nxt(blk, w):
        per = tq // blk
        last = t // blk - 1
        return pl.BlockSpec((None, blk, w), lambda bi, i: (bi, jnp.minimum((i + 1) * per, last), 0))

    kv_specs = [main(NA_WIDTH), prev(halo, NA_WIDTH), nxt(halo, NA_WIDTH)]
    mem_spec = pl.BlockSpec((None, n_mem, MEM_WIDTH), lambda bi, i: (bi, 0, 0))
    d_mix = NA_WIDTH + CONV_WIDTH + MEM_WIDTH
    return pl.pallas_call(
        functools.partial(_mixers_kernel, rows=rows, tile_rows=tile_rows),
        grid=(b, nt),
        in_specs=[main(NA_WIDTH)] + kv_specs + kv_specs
                 + [main(CONV_WIDTH), prev(BF16_ROWS, CONV_WIDTH), nxt(BF16_ROWS, CONV_WIDTH),
                    main(CONV_WIDTH), main(MEM_WIDTH), mem_spec, mem_spec,
                    pl.BlockSpec(bias_tab.shape, lambda bi, i: (0,) * bias_tab.ndim,
                                 pipeline_mode=pl.Buffered(1)),
                    pl.BlockSpec(conv_w.shape, lambda bi, i: (0, 0))],
        out_specs=main(d_mix),
        out_shape=jax.ShapeDtypeStruct((b, t, d_mix), BF16),
        scratch_shapes=[pltpu.VMEM((tq + 2 * halo, NA_WIDTH), BF16),
                        pltpu.VMEM((tq + 2 * halo, NA_WIDTH), BF16),
                        pltpu.VMEM((tq + 2 * F32_ROWS, CONV_WIDTH), F32),
                        pltpu.VMEM((HEAD_PAIRS, 2 * GRID_W, WIN_H * GRID_W), F32),
                        pltpu.VMEM((HEAD_PAIRS, 2 * GRID_W, WIN_H * GRID_W), F32)],
        compiler_params=pltpu.CompilerParams(
            dimension_semantics=("parallel", "parallel"), vmem_limit_bytes=VMEM_LIMIT),
        name="mixers",
    )(q, k, k, k, v, v, v, p, p, p, bb, qm, mk, mv, bias_tab, conv_w)


def _out_proj_kernel(x_ref, y_ref, w_ref, o_ref):
    o_ref[...] = x_ref[...] + jnp.dot(y_ref[...], w_ref[...], preferred_element_type=F32)


def _out_proj(x, y, w_b, tm):
    b, t, d = x.shape
    row = lambda w: pl.BlockSpec((None, tm, w), lambda bi, i: (bi, i, 0))
    return pl.pallas_call(
        _out_proj_kernel,
        grid=(b, t // tm),
        in_specs=[row(d), row(y.shape[-1]), pl.BlockSpec(w_b.shape, lambda bi, i: (0, 0))],
        out_specs=row(d),
        out_shape=jax.ShapeDtypeStruct((b, t, d), F32),
        compiler_params=pltpu.CompilerParams(
            dimension_semantics=("parallel", "parallel"), vmem_limit_bytes=VMEM_LIMIT),
        name="out_proj",
    )(x, y, w_b)


def _ffn_kernel(x_ref, xp_ref, xn_ref, g_ref, wa_ref, wg_ref, cw_ref, wo_ref,
                o_ref, next_ref, pa_ref, pg_ref, *, tm):
    i = pl.program_id(1)
    j = pl.program_id(2)
    nt = pl.num_programs(1)
    nj = pl.num_programs(2)
    h0 = BF16_ROWS

    @pl.when(j == 0)
    def _():
        g = g_ref[...]
        zeros = jnp.zeros((F32_ROWS, x_ref.shape[-1]), F32)
        prev = jnp.where(i > 0, _rms(xp_ref[...], g), 0.0)
        nxt = jnp.where(i < nt - 1, _rms(xn_ref[...], g), 0.0)
        next_ref[0:h0] = jnp.concatenate([zeros, prev], axis=0).astype(BF16)
        next_ref[h0:h0 + tm] = _rms(x_ref[...], g).astype(BF16)
        next_ref[h0 + tm:2 * h0 + tm] = jnp.concatenate([nxt, zeros], axis=0).astype(BF16)
        o_ref[...] = x_ref[...]

    n2 = next_ref[...]
    pa_ref[...] = jnp.dot(n2, wa_ref[...], preferred_element_type=F32)
    pg_ref[...] = jnp.dot(n2, wg_ref[...], preferred_element_type=F32)

    def conv(ref, cw):
        return (cw[0:1] * ref[h0 - 1:h0 - 1 + tm] + cw[1:2] * ref[h0:h0 + tm]
                + cw[2:3] * ref[h0 + 1:h0 + 1 + tm])

    hidden = (jax.nn.silu(conv(pa_ref, cw_ref[j])) * conv(pg_ref, cw_ref[nj + j])).astype(BF16)
    o_ref[...] += jnp.dot(hidden, wo_ref[...], preferred_element_type=F32)


def _ffn(x1, g, w_in_b, conv_w, w_out_b, tm, tf):
    b, t, d = x1.shape
    d_ff = w_out_b.shape[0]
    nj = d_ff // tf
    assert d_ff % tf == 0 and t % tm == 0 and tm % F32_ROWS == 0
    per = tm // F32_ROWS
    last = t // F32_ROWS - 1
    taps = conv_w.shape[0]
    conv_chunks = jnp.swapaxes(conv_w.reshape(taps, 2 * nj, tf), 0, 1)
    return pl.pallas_call(
        functools.partial(_ffn_kernel, tm=tm),
        grid=(b, t // tm, nj),
        in_specs=[
            pl.BlockSpec((None, tm, d), lambda bi, i, j: (bi, i, 0)),
            pl.BlockSpec((None, F32_ROWS, d), lambda bi, i, j: (bi, jnp.maximum(i * per - 1, 0), 0)),
            pl.BlockSpec((None, F32_ROWS, d),
                         lambda bi, i, j: (bi, jnp.minimum((i + 1) * per, last), 0)),
            pl.BlockSpec((1, d), lambda bi, i, j: (0, 0)),
            pl.BlockSpec((d, tf), lambda bi, i, j: (0, j)),
            pl.BlockSpec((d, tf), lambda bi, i, j: (0, nj + j)),
            pl.BlockSpec((2 * nj, taps, tf), lambda bi, i, j: (0, 0, 0)),
            pl.BlockSpec((tf, d), lambda bi, i, j: (j, 0)),
        ],
        out_specs=pl.BlockSpec((None, tm, d), lambda bi, i, j: (bi, i, 0)),
        out_shape=jax.ShapeDtypeStruct((b, t, d), F32),
        scratch_shapes=[pltpu.VMEM((tm + 2 * BF16_ROWS, d), BF16),
                        pltpu.VMEM((tm + 2 * BF16_ROWS, tf), F32),
                        pltpu.VMEM((tm + 2 * BF16_ROWS, tf), F32)],
        compiler_params=pltpu.CompilerParams(
            dimension_semantics=("parallel", "parallel", "arbitrary"),
            vmem_limit_bytes=VMEM_LIMIT),
        name="ffn",
    )(x1, x1, x1, g, w_in_b, w_in_b, conv_chunks, w_out_b)


_TM_PROJ = 512
_TILE_ROWS = 8
_TM_FFN = 512
_TF_FFN = 512


def _layer(x, mem, p):
    mk, mv = _mem_kv(mem, p["mem_norm_g"], p["w_mem_kv"], p["mem_k_gain"])
    q, k, v, ch, bb, qm = _in_proj(x, p["g_mix"], p["w_in"], p["na_q_gain"], p["na_k_gain"],
                                   p["mem_q_gain"], _TM_PROJ)
    y = _mixers(q, k, v, ch, bb, qm, mk, mv, p["bias_tab"], p["conv_w"], _TILE_ROWS)
    x1 = _out_proj(x, y, p["w_out"], _TM_PROJ)
    return _ffn(x1, p["g_ffn"], p["w_ffn_in"], p["ffn_conv_w"], p["w_ffn_out"], _TM_FFN, _TF_FFN)


def kernel(x_prompt, x_sample, mem_prompt, mem_sample, g_mix, w_in, na_q_gain, na_k_gain,
           na_rel_bias, conv_w, mem_norm_g, w_mem_kv, mem_q_gain, mem_k_gain, w_out,
           g_ffn, w_ffn_in, ffn_conv_w, w_ffn_out):
    y_prompt, y_sample = x_prompt, x_sample
    for l in range(g_mix.shape[0]):
        row = lambda a: a[l].reshape(1, -1).astype(F32)
        q_scale = 1.0 / math.sqrt(HEAD_DIM)
        p = dict(
            g_mix=row(g_mix), g_ffn=row(g_ffn), mem_norm_g=row(mem_norm_g),
            na_q_gain=jnp.tile(row(na_q_gain), (1, 2)) * q_scale,
            na_k_gain=jnp.tile(row(na_k_gain), (1, 2)),
            mem_q_gain=row(mem_q_gain), mem_k_gain=row(mem_k_gain),
            w_in=w_in[l].astype(BF16), w_mem_kv=w_mem_kv[l].astype(BF16),
            w_out=w_out[l].astype(BF16), w_ffn_in=w_ffn_in[l].astype(BF16),
            w_ffn_out=w_ffn_out[l].astype(BF16),
            conv_w=conv_w[l].astype(F32), ffn_conv_w=ffn_conv_w[l].astype(F32),
            bias_tab=_bias_table(na_rel_bias[l]),
        )
        y_prompt = _layer(y_prompt, mem_prompt, p)
        y_sample = _layer(y_sample, mem_sample, p)
    return (y_prompt, y_sample)
```

```python
import functools
import math

import jax
import jax.numpy as jnp
import numpy as np
from jax import lax
from jax.experimental import pallas as pl
from jax.experimental.pallas import tpu as pltpu

F32 = jnp.float32
BF16 = jnp.bfloat16

GRID_W = 64
WIN_H = 8
WIN_W = 16
HEAD_DIM = 64
NA_HEADS = 16
NA_WIDTH = NA_HEADS * HEAD_DIM
CONV_WIDTH = 512
MEM_HEADS = 4
MEM_HEAD_DIM = 128
MEM_WIDTH = MEM_HEADS * MEM_HEAD_DIM
EPS = 1e-6
NEG_INF = -1e30

LANES = 128
BF16_ROWS = 16
F32_ROWS = 8
VMEM_LIMIT = 56 * 1024 * 1024

HEAD_PAIRS = NA_HEADS // 2
KEY_ROWS_HALO = WIN_H // 2
N_DR = 2 * WIN_H - 1


def _rms(x, g):
    return x * lax.rsqrt(jnp.mean(x * x, axis=-1, keepdims=True) + EPS) * g


def _lo_lanes():
    return lax.broadcasted_iota(jnp.int32, (1, LANES), 1) < HEAD_DIM


def _pair_rms(z, g2):
    lo = _lo_lanes()
    sq = z * z
    s0 = jnp.sum(jnp.where(lo, sq, 0.0), axis=-1, keepdims=True)
    s1 = jnp.sum(jnp.where(lo, 0.0, sq), axis=-1, keepdims=True)
    ms = jnp.where(lo, s0, s1) * (1.0 / HEAD_DIM)
    return z * lax.rsqrt(ms + EPS) * g2


def _softmax_pv(s, v):
    m = jnp.max(s, axis=-1, keepdims=True)
    e = jnp.exp(s - m)
    l = jnp.sum(e, axis=-1, keepdims=True)
    o = jnp.dot(e.astype(BF16), v, preferred_element_type=F32)
    return o * (1.0 / l)


def _mem_kv_kernel(mem_ref, g_ref, w_ref, kg_ref, mk_ref, mv_ref):
    n = _rms(mem_ref[...], g_ref[...]).astype(BF16)
    z = jnp.dot(n, w_ref[...], preferred_element_type=F32)
    for h in range(MEM_HEADS):
        c = slice(h * MEM_HEAD_DIM, (h + 1) * MEM_HEAD_DIM)
        mk_ref[:, c] = _rms(z[:, c], kg_ref[...]).astype(BF16)
    mv_ref[...] = z[:, MEM_WIDTH:].astype(BF16)


def _mem_kv(mem, g, w_b, k_gain):
    b, m, d = mem.shape
    out = jax.ShapeDtypeStruct((b, m, MEM_WIDTH), BF16)
    return pl.pallas_call(
        _mem_kv_kernel,
        grid=(b,),
        in_specs=[
            pl.BlockSpec((None, m, d), lambda i: (i, 0, 0)),
            pl.BlockSpec((1, d), lambda i: (0, 0)),
            pl.BlockSpec((d, 2 * MEM_WIDTH), lambda i: (0, 0)),
            pl.BlockSpec((1, MEM_HEAD_DIM), lambda i: (0, 0)),
        ],
        out_specs=[pl.BlockSpec((None, m, MEM_WIDTH), lambda i: (i, 0, 0))] * 2,
        out_shape=[out, out],
        compiler_params=pltpu.CompilerParams(
            dimension_semantics=("parallel",), vmem_limit_bytes=VMEM_LIMIT),
        name="mem_kv",
    )(mem, g, w_b, k_gain)


_Q0, _K0, _V0 = 0, NA_WIDTH, 2 * NA_WIDTH
_H0 = 3 * NA_WIDTH
_B0 = _H0 + CONV_WIDTH
_C0 = _B0 + CONV_WIDTH
_QM0 = _C0 + CONV_WIDTH
_PROJ_CHUNK = 512


def _in_proj_kernel(x_ref, g_ref, w_ref, qg_ref, kg_ref, mqg_ref,
                    q_ref, k_ref, v_ref, p_ref, b_ref, qm_ref):
    n = _rms(x_ref[...], g_ref[...]).astype(BF16)

    def proj(c0):
        return jnp.dot(n, w_ref[:, c0:c0 + _PROJ_CHUNK], preferred_element_type=F32)

    for base, gain_ref, out_ref in ((_Q0, qg_ref, q_ref), (_K0, kg_ref, k_ref)):
        for cc in range(NA_WIDTH // _PROJ_CHUNK):
            z = proj(base + cc * _PROJ_CHUNK)
            for s in range(_PROJ_CHUNK // LANES):
                o = cc * _PROJ_CHUNK + s * LANES
                out_ref[:, o:o + LANES] = _pair_rms(
                    z[:, s * LANES:(s + 1) * LANES], gain_ref[...]).astype(BF16)
    for cc in range(NA_WIDTH // _PROJ_CHUNK):
        o = cc * _PROJ_CHUNK
        v_ref[:, o:o + _PROJ_CHUNK] = proj(_V0 + o).astype(BF16)
    p_ref[...] = (proj(_C0) * proj(_H0)).astype(BF16)
    b_ref[...] = proj(_B0).astype(BF16)
    z = proj(_QM0)
    for h in range(MEM_HEADS):
        c = slice(h * MEM_HEAD_DIM, (h + 1) * MEM_HEAD_DIM)
        qm_ref[:, c] = _rms(z[:, c], mqg_ref[...]).astype(BF16)


def _in_proj(x, g, w_b, q_gain2, k_gain2, mq_gain, tm):
    b, t, d = x.shape
    nt = t // tm
    row = lambda w: pl.BlockSpec((None, tm, w), lambda bi, i: (bi, i, 0))
    const = lambda shape: pl.BlockSpec(shape, lambda bi, i: (0,) * len(shape))
    sds = lambda w: jax.ShapeDtypeStruct((b, t, w), BF16)
    return pl.pallas_call(
        _in_proj_kernel,
        grid=(b, nt),
        in_specs=[row(d), const((1, d)), const(w_b.shape),
                  const((1, LANES)), const((1, LANES)), const((1, MEM_HEAD_DIM))],
        out_specs=[row(NA_WIDTH)] * 3 + [row(CONV_WIDTH)] * 2 + [row(MEM_WIDTH)],
        out_shape=[sds(NA_WIDTH)] * 3 + [sds(CONV_WIDTH)] * 2 + [sds(MEM_WIDTH)],
        compiler_params=pltpu.CompilerParams(
            dimension_semantics=("parallel", "parallel"), vmem_limit_bytes=VMEM_LIMIT),
        name="in_proj",
    )(x, g, w_b, q_gain2, k_gain2, mq_gain)


COL_BLOCKS = GRID_W // WIN_W
Q_BLOCKS = GRID_W // WIN_W
SLAB = 2 * WIN_W
_NEEDED_COL_BLOCKS = tuple(
    tuple(cb for cb in range(COL_BLOCKS)
          if any(max(0, min(q - WIN_W // 2, GRID_W - WIN_W)) < (cb + 1) * WIN_W
                 and max(0, min(q - WIN_W // 2, GRID_W - WIN_W)) + WIN_W > cb * WIN_W
                 for q in range(j * WIN_W, (j + 1) * WIN_W)))
    for j in range(Q_BLOCKS))


_SCORE_BLOCKS = tuple((j, cb) for j in range(Q_BLOCKS) for cb in _NEEDED_COL_BLOCKS[j])


def _bias_table(rel_bias):
    h = rel_bias.shape[0]
    n_dc = 2 * WIN_W - 1
    n_blk = len(_SCORE_BLOCKS)
    cr = np.arange(WIN_W)[:, None]
    cl = np.arange(WIN_W)[None, :]
    select = np.stack([(WIN_W * (x - 1) + cl - cr + WIN_W - 1)[..., None] == np.arange(n_dc)
                       for x in range(3)]).astype(np.float32)
    mask = np.zeros((n_blk, 2, WIN_W, WIN_H, WIN_W), np.float32)
    for n, (j, cb) in enumerate(_SCORE_BLOCKS):
        q, k = j * WIN_W + cr, cb * WIN_W + cl
        q_cs = np.clip(q - WIN_W // 2, 0, GRID_W - WIN_W)
        mask[n] = np.where((k >= q_cs) & (k < q_cs + WIN_W), 0.0, NEG_INF)[None, :, None, :]
    rb = rel_bias.astype(F32).reshape(h // 2, 2, N_DR, n_dc)
    pieces = jnp.einsum("pedt,xrlt->pedxrl", rb, select, precision=lax.Precision.HIGHEST)
    pieces = jnp.pad(pieces, ((0, 0),) * 5 + ((0, LANES - WIN_W),))
    return pl.pallas_call(
        _bias_expand_kernel,
        grid=(h // 2,),
        in_specs=[pl.BlockSpec((None,) + pieces.shape[1:], lambda p: (p, 0, 0, 0, 0, 0)),
                  pl.BlockSpec((n_blk, SLAB, LANES), lambda p: (0, 0, 0))],
        out_specs=pl.BlockSpec((WIN_H, None, n_blk, SLAB, LANES), lambda p: (0, p, 0, 0, 0)),
        out_shape=jax.ShapeDtypeStruct((WIN_H, h // 2, n_blk, SLAB, LANES), F32),
        name="bias_expand",
    )(pieces, mask.reshape(n_blk, SLAB, LANES))


def _bias_expand_kernel(piece_ref, mask_ref, o_ref):
    for n, (j, cb) in enumerate(_SCORE_BLOCKS):
        for d0 in range(WIN_H):
            halves = []
            for hd in range(2):
                acc = piece_ref[hd, d0, cb - j + 1]
                for wr in range(1, WIN_H):
                    acc = acc + pltpu.roll(piece_ref[hd, d0 + wr, cb - j + 1], wr * WIN_W, axis=1)
                halves.append(acc)
            o_ref[d0, n] = jnp.concatenate(halves, axis=0) + mask_ref[n]


def _mixers_kernel(q_ref, k_ref, kp_ref, kn_ref, v_ref, vp_ref, vn_ref,
                   p_ref, pp_ref, pn_ref, b_ref, qm_ref, mk_ref, mv_ref, bt_ref, cw_ref,
                   y_ref, kext, vext, pext, s0_ref, s1_ref, *, rows, tile_rows):
    i = pl.program_id(1)
    nt = pl.num_programs(1)
    tq = tile_rows * GRID_W
    halo = KEY_ROWS_HALO * GRID_W
    lo = _lo_lanes()
    n_hp = 2 * GRID_W

    for ext, prev, main, nxt in ((kext, kp_ref, k_ref, kn_ref), (vext, vp_ref, v_ref, vn_ref)):
        ext[0:halo] = prev[...]
        ext[halo:halo + tq] = main[...]
        ext[halo + tq:halo + tq + halo] = nxt[...]

    r0 = i * tile_rows

    def window(ext, start, c):
        return jnp.concatenate(
            [ext[pl.ds(pl.multiple_of(start + wr * GRID_W + cb * WIN_W, WIN_W), WIN_W), c]
             for cb in range(COL_BLOCKS) for wr in range(WIN_H)], axis=0)

    def row_geometry(lr):
        r = r0 + lr
        ws = jnp.clip(r - WIN_H // 2, 0, rows - WIN_H)
        start = pl.multiple_of((ws - r0 + KEY_ROWS_HALO) * GRID_W, GRID_W)
        dr0 = ws - r + (WIN_H - 1)
        return start, pl.multiple_of(lr * GRID_W, GRID_W), dr0

    def scores(lr, s_ref):
        start, qrow, dr0 = row_geometry(lr)
        zero = jnp.zeros((WIN_W, LANES), BF16)
        for quad in range(HEAD_PAIRS // 2):
            c2 = slice(2 * quad * LANES, (2 * quad + 2) * LANES)
            q4 = q_ref[pl.ds(qrow, GRID_W), c2]
            pieces = []
            for half in range(2):
                for j in range(Q_BLOCKS):
                    qj = q4[j * WIN_W:(j + 1) * WIN_W, half * LANES:(half + 1) * LANES]
                    for piece in (jnp.where(lo, qj, zero), jnp.where(lo, zero, qj)):
                        pieces.append(jnp.concatenate(
                            [piece, zero] if half == 0 else [zero, piece], axis=1))
            s = lax.dot_general(jnp.concatenate(pieces, axis=0), window(kext, start, c2),
                                (((1,), (1,)), ((), ())), preferred_element_type=F32)
            for half in range(2):
                hp = 2 * quad + half
                for n, (j, cb) in enumerate(_SCORE_BLOCKS):
                    rs = slice(j * SLAB, (j + 1) * SLAB)
                    cs = slice(cb * LANES, (cb + 1) * LANES)
                    s_ref[hp, rs, cs] = (s[half * n_hp + j * SLAB:half * n_hp + (j + 1) * SLAB, cs]
                                         + bt_ref[dr0, hp, n])

    def attend(lr, s_ref):
        start, qrow, _ = row_geometry(lr)
        for quad in range(HEAD_PAIRS // 2):
            c2 = slice(2 * quad * LANES, (2 * quad + 2) * LANES)
            p_rows, inv_l = [], []
            for half in range(2):
                hp = 2 * quad + half
                for j in range(Q_BLOCKS):
                    rs = slice(j * SLAB, (j + 1) * SLAB)
                    need = _NEEDED_COL_BLOCKS[j]
                    sb = [s_ref[hp, rs, cb * LANES:(cb + 1) * LANES] for cb in need]
                    m = jnp.max(functools.reduce(jnp.maximum, sb), axis=-1, keepdims=True)
                    e = [jnp.exp(x - m) for x in sb]
                    inv_l.append(
                        1.0 / jnp.sum(functools.reduce(jnp.add, e), axis=-1, keepdims=True))
                    zero = jnp.zeros((SLAB, LANES), BF16)
                    blocks = [zero] * COL_BLOCKS
                    for cb, x in zip(need, e):
                        blocks[cb] = x.astype(BF16)
                    p_rows.append(jnp.concatenate(blocks, axis=1))
            o = jnp.dot(jnp.concatenate(p_rows, axis=0), window(vext, start, c2),
                        preferred_element_type=F32)
            for half in range(2):
                c = slice((2 * quad + half) * LANES, (2 * quad + half + 1) * LANES)
                for j in range(Q_BLOCKS):
                    r_lo = half * n_hp + j * SLAB
                    oj = o[r_lo:r_lo + SLAB, half * LANES:(half + 1) * LANES]
                    oj = oj * inv_l[half * Q_BLOCKS + j]
                    y_ref[pl.ds(pl.multiple_of(qrow + j * WIN_W, WIN_W), WIN_W), c] = jnp.where(
                        lo, oj[:WIN_W], oj[WIN_W:]).astype(BF16)

    def gated_conv():
        hq = tq // 2
        before = jnp.where(i > 0, pp_ref[...].astype(F32)[F32_ROWS:], 0.0)
        after = jnp.where(i < nt - 1, pn_ref[...].astype(F32)[:F32_ROWS], 0.0)
        for r0h, lead, trail in ((0, before, None), (hq, None, after)):
            pext[0:F32_ROWS] = (p_ref[r0h - BF16_ROWS:r0h].astype(F32)[F32_ROWS:]
                                if lead is None else lead)
            pext[F32_ROWS:F32_ROWS + hq] = p_ref[r0h:r0h + hq].astype(F32)
            pext[F32_ROWS + hq:2 * F32_ROWS + hq] = (
                p_ref[r0h + hq:r0h + hq + BF16_ROWS].astype(F32)[:F32_ROWS]
                if trail is None else trail)
            conv = (cw_ref[0:1] * pext[F32_ROWS - 1:F32_ROWS - 1 + hq]
                    + cw_ref[1:2] * pext[F32_ROWS:F32_ROWS + hq]
                    + cw_ref[2:3] * pext[F32_ROWS + 1:F32_ROWS + 1 + hq])
            y_ref[r0h:r0h + hq, NA_WIDTH:NA_WIDTH + CONV_WIDTH] = (
                b_ref[r0h:r0h + hq].astype(F32) * conv).astype(BF16)

    def mem_attention():
        for h in range(MEM_HEADS):
            c = slice(h * MEM_HEAD_DIM, (h + 1) * MEM_HEAD_DIM)
            s = lax.dot_general(qm_ref[:, c], mk_ref[:, c], (((1,), (1,)), ((), ())),
                                preferred_element_type=F32) * (1.0 / math.sqrt(MEM_HEAD_DIM))
            o0 = NA_WIDTH + CONV_WIDTH + h * MEM_HEAD_DIM
            y_ref[:, o0:o0 + MEM_HEAD_DIM] = _softmax_pv(s, mv_ref[:, c]).astype(BF16)

    scores(0, s0_ref)

    def row_pair(k, carry):
        lr = 2 * k
        scores(lr + 1, s1_ref)
        attend(lr, s0_ref)
        scores(lr + 2, s0_ref)
        attend(lr + 1, s1_ref)
        return carry

    lax.fori_loop(0, tile_rows // 2 - 1, row_pair, 0)
    scores(tile_rows - 1, s1_ref)
    attend(tile_rows - 2, s0_ref)
    attend(tile_rows - 1, s1_ref)
    gated_conv()
    mem_attention()


def _mixers(q, k, v, p, bb, qm, mk, mv, bias_tab, conv_w, tile_rows):
    b, t, _ = q.shape
    rows = t // GRID_W
    tq = tile_rows * GRID_W
    nt = t // tq
    halo = KEY_ROWS_HALO * GRID_W
    assert rows >= WIN_H and rows % tile_rows == 0 and tile_rows >= KEY_ROWS_HALO
    assert tile_rows % 2 == 0
    assert tq % halo == 0 and tq % BF16_ROWS == 0
    n_mem = mk.shape[1]

    def main(w):
        return pl.BlockSpec((None, tq, w), lambda bi, i: (bi, i, 0))

    def prev(blk, w):
        per = tq // blk
        return pl.BlockSpec((None, blk, w), lambda bi, i: (bi, jnp.maximum(i * per - 1, 0), 0))

    def nxt(blk, w):
        per = tq // blk
        last = t // blk - 1
        return pl.BlockSpec((None, blk, w), lambda bi, i: (bi, jnp.minimum((i + 1) * per, last), 0))

    kv_specs = [main(NA_WIDTH), prev(halo, NA_WIDTH), nxt(halo, NA_WIDTH)]
    mem_spec = pl.BlockSpec((None, n_mem, MEM_WIDTH), lambda bi, i: (bi, 0, 0),
                            pipeline_mode=pl.Buffered(1))
    d_mix = NA_WIDTH + CONV_WIDTH + MEM_WIDTH
    return pl.pallas_call(
        functools.partial(_mixers_kernel, rows=rows, tile_rows=tile_rows),
        grid=(b, nt),
        in_specs=[main(NA_WIDTH)] + kv_specs + kv_specs
                 + [main(CONV_WIDTH), prev(BF16_ROWS, CONV_WIDTH), nxt(BF16_ROWS, CONV_WIDTH),
                    main(CONV_WIDTH), main(MEM_WIDTH), mem_spec, mem_spec,
                    pl.BlockSpec(bias_tab.shape, lambda bi, i: (0,) * bias_tab.ndim,
                                 pipeline_mode=pl.Buffered(1)),
                    pl.BlockSpec(conv_w.shape, lambda bi, i: (0, 0))],
        out_specs=main(d_mix),
        out_shape=jax.ShapeDtypeStruct((b, t, d_mix), BF16),
        scratch_shapes=[pltpu.VMEM((tq + 2 * halo, NA_WIDTH), BF16),
                        pltpu.VMEM((tq + 2 * halo, NA_WIDTH), BF16),
                        pltpu.VMEM((tq // 2 + 2 * F32_ROWS, CONV_WIDTH), F32),
                        pltpu.VMEM((HEAD_PAIRS, 2 * GRID_W, WIN_H * GRID_W), F32),
                        pltpu.VMEM((HEAD_PAIRS, 2 * GRID_W, WIN_H * GRID_W), F32)],
        compiler_params=pltpu.CompilerParams(
            dimension_semantics=("parallel", "parallel"), vmem_limit_bytes=VMEM_LIMIT),
        name="mixers",
    )(q, k, k, k, v, v, v, p, p, p, bb, qm, mk, mv, bias_tab, conv_w)


def _out_proj_kernel(x_ref, y_ref, w_ref, o_ref):
    o_ref[...] = x_ref[...] + jnp.dot(y_ref[...], w_ref[...], preferred_element_type=F32)


def _out_proj(x, y, w_b, tm):
    b, t, d = x.shape
    row = lambda w: pl.BlockSpec((None, tm, w), lambda bi, i: (bi, i, 0))
    return pl.pallas_call(
        _out_proj_kernel,
        grid=(b, t // tm),
        in_specs=[row(d), row(y.shape[-1]), pl.BlockSpec(w_b.shape, lambda bi, i: (0, 0))],
        out_specs=row(d),
        out_shape=jax.ShapeDtypeStruct((b, t, d), F32),
        compiler_params=pltpu.CompilerParams(
            dimension_semantics=("parallel", "parallel"), vmem_limit_bytes=VMEM_LIMIT),
        name="out_proj",
    )(x, y, w_b)


def _ffn_kernel(x_ref, xp_ref, xn_ref, g_ref, wa_ref, wg_ref, cw_ref, wo_ref,
                o_ref, next_ref, pa_ref, pg_ref, *, tm):
    i = pl.program_id(1)
    j = pl.program_id(2)
    nt = pl.num_programs(1)
    nj = pl.num_programs(2)
    h0 = BF16_ROWS

    @pl.when(j == 0)
    def _():
        g = g_ref[...]
        zeros = jnp.zeros((F32_ROWS, x_ref.shape[-1]), F32)
        prev = jnp.where(i > 0, _rms(xp_ref[...], g), 0.0)
        nxt = jnp.where(i < nt - 1, _rms(xn_ref[...], g), 0.0)
        next_ref[0:h0] = jnp.concatenate([zeros, prev], axis=0).astype(BF16)
        next_ref[h0:h0 + tm] = _rms(x_ref[...], g).astype(BF16)
        next_ref[h0 + tm:2 * h0 + tm] = jnp.concatenate([nxt, zeros], axis=0).astype(BF16)
        o_ref[...] = x_ref[...]

    n2 = next_ref[...]
    pa_ref[...] = jnp.dot(n2, wa_ref[...], preferred_element_type=F32)
    pg_ref[...] = jnp.dot(n2, wg_ref[...], preferred_element_type=F32)

    def conv(ref, cw):
        return (cw[0:1] * ref[h0 - 1:h0 - 1 + tm] + cw[1:2] * ref[h0:h0 + tm]
                + cw[2:3] * ref[h0 + 1:h0 + 1 + tm])

    hidden = (jax.nn.silu(conv(pa_ref, cw_ref[j])) * conv(pg_ref, cw_ref[nj + j])).astype(BF16)
    o_ref[...] += jnp.dot(hidden, wo_ref[...], preferred_element_type=F32)


def _ffn(x1, g, w_in_b, conv_w, w_out_b, tm, tf):
    b, t, d = x1.shape
    d_ff = w_out_b.shape[0]
    nj = d_ff // tf
    assert d_ff % tf == 0 and t % tm == 0 and tm % F32_ROWS == 0
    per = tm // F32_ROWS
    last = t // F32_ROWS - 1
    taps = conv_w.shape[0]
    conv_chunks = jnp.swapaxes(conv_w.reshape(taps, 2 * nj, tf), 0, 1)
    return pl.pallas_call(
        functools.partial(_ffn_kernel, tm=tm),
        grid=(b, t // tm, nj),
        in_specs=[
            pl.BlockSpec((None, tm, d), lambda bi, i, j: (bi, i, 0)),
            pl.BlockSpec((None, F32_ROWS, d), lambda bi, i, j: (bi, jnp.maximum(i * per - 1, 0), 0)),
            pl.BlockSpec((None, F32_ROWS, d),
                         lambda bi, i, j: (bi, jnp.minimum((i + 1) * per, last), 0)),
            pl.BlockSpec((1, d), lambda bi, i, j: (0, 0)),
            pl.BlockSpec((d, tf), lambda bi, i, j: (0, j)),
            pl.BlockSpec((d, tf), lambda bi, i, j: (0, nj + j)),
            pl.BlockSpec((2 * nj, taps, tf), lambda bi, i, j: (0, 0, 0)),
            pl.BlockSpec((tf, d), lambda bi, i, j: (j, 0)),
        ],
        out_specs=pl.BlockSpec((None, tm, d), lambda bi, i, j: (bi, i, 0)),
        out_shape=jax.ShapeDtypeStruct((b, t, d), F32),
        scratch_shapes=[pltpu.VMEM((tm + 2 * BF16_ROWS, d), BF16),
                        pltpu.VMEM((tm + 2 * BF16_ROWS, tf), F32),
                        pltpu.VMEM((tm + 2 * BF16_ROWS, tf), F32)],
        compiler_params=pltpu.CompilerParams(
            dimension_semantics=("parallel", "parallel", "arbitrary"),
            vmem_limit_bytes=VMEM_LIMIT),
        name="ffn",
    )(x1, x1, x1, g, w_in_b, w_in_b, conv_chunks, w_out_b)


_TM_PROJ = 512
_TILE_ROWS = 16
_TM_FFN = 512
_TF_FFN = 512


def _layer(x, mem, p):
    mk, mv = _mem_kv(mem, p["mem_norm_g"], p["w_mem_kv"], p["mem_k_gain"])
    q, k, v, ch, bb, qm = _in_proj(x, p["g_mix"], p["w_in"], p["na_q_gain"], p["na_k_gain"],
                                   p["mem_q_gain"], _TM_PROJ)
    y = _mixers(q, k, v, ch, bb, qm, mk, mv, p["bias_tab"], p["conv_w"], _TILE_ROWS)
    x1 = _out_proj(x, y, p["w_out"], _TM_PROJ)
    return _ffn(x1, p["g_ffn"], p["w_ffn_in"], p["ffn_conv_w"], p["w_ffn_out"], _TM_FFN, _TF_FFN)


def kernel(x_prompt, x_sample, mem_prompt, mem_sample, g_mix, w_in, na_q_gain, na_k_gain,
           na_rel_bias, conv_w, mem_norm_g, w_mem_kv, mem_q_gain, mem_k_gain, w_out,
           g_ffn, w_ffn_in, ffn_conv_w, w_ffn_out):
    y_prompt, y_sample = x_prompt, x_sample
    for l in range(g_mix.shape[0]):
        row = lambda a: a[l].reshape(1, -1).astype(F32)
        q_scale = 1.0 / math.sqrt(HEAD_DIM)
        p = dict(
            g_mix=row(g_mix), g_ffn=row(g_ffn), mem_norm_g=row(mem_norm_g),
            na_q_gain=jnp.tile(row(na_q_gain), (1, 2)) * q_scale,
            na_k_gain=jnp.tile(row(na_k_gain), (1, 2)),
            mem_q_gain=row(mem_q_gain), mem_k_gain=row(mem_k_gain),
            w_in=w_in[l].astype(BF16), w_mem_kv=w_mem_kv[l].astype(BF16),
            w_out=w_out[l].astype(BF16), w_ffn_in=w_ffn_in[l].astype(BF16),
            w_ffn_out=w_ffn_out[l].astype(BF16),
            conv_w=conv_w[l].astype(F32), ffn_conv_w=ffn_conv_w[l].astype(F32),
            bias_tab=_bias_table(na_rel_bias[l]),
        )
        y_prompt = _layer(y_prompt, mem_prompt, p)
        y_sample = _layer(y_sample, mem_sample, p)
    return (y_prompt, y_sample)
```

```python
import functools
import math

import jax
import jax.numpy as jnp
import numpy as np
from jax import lax
from jax.experimental import pallas as pl
from jax.experimental.pallas import tpu as pltpu

F32 = jnp.float32
BF16 = jnp.bfloat16

GRID_W = 64
WIN_H = 8
WIN_W = 16
HEAD_DIM = 64
NA_HEADS = 16
NA_WIDTH = NA_HEADS * HEAD_DIM
CONV_WIDTH = 512
MEM_HEADS = 4
MEM_HEAD_DIM = 128
MEM_WIDTH = MEM_HEADS * MEM_HEAD_DIM
EPS = 1e-6
NEG_INF = -1e30

LANES = 128
BF16_ROWS = 16
F32_ROWS = 8
VMEM_LIMIT = 56 * 1024 * 1024

HEAD_PAIRS = NA_HEADS // 2
KEY_ROWS_HALO = WIN_H // 2
N_DR = 2 * WIN_H - 1


def _rms(x, g):
    return x * lax.rsqrt(jnp.mean(x * x, axis=-1, keepdims=True) + EPS) * g


def _lo_lanes():
    return lax.broadcasted_iota(jnp.int32, (1, LANES), 1) < HEAD_DIM


def _pair_rms(z, g2):
    lo = _lo_lanes()
    sq = z * z
    s0 = jnp.sum(jnp.where(lo, sq, 0.0), axis=-1, keepdims=True)
    s1 = jnp.sum(jnp.where(lo, 0.0, sq), axis=-1, keepdims=True)
    ms = jnp.where(lo, s0, s1) * (1.0 / HEAD_DIM)
    return z * lax.rsqrt(ms + EPS) * g2


def _softmax_pv(s, v):
    m = jnp.max(s, axis=-1, keepdims=True)
    e = jnp.exp(s - m)
    l = jnp.sum(e, axis=-1, keepdims=True)
    o = jnp.dot(e.astype(BF16), v, preferred_element_type=F32)
    return o * (1.0 / l)


def _mem_kv_kernel(mem_ref, g_ref, w_ref, kg_ref, mk_ref, mv_ref):
    n = _rms(mem_ref[...], g_ref[...]).astype(BF16)
    z = jnp.dot(n, w_ref[...], preferred_element_type=F32)
    for h in range(MEM_HEADS):
        c = slice(h * MEM_HEAD_DIM, (h + 1) * MEM_HEAD_DIM)
        mk_ref[:, c] = _rms(z[:, c], kg_ref[...]).astype(BF16)
    mv_ref[...] = z[:, MEM_WIDTH:].astype(BF16)


def _mem_kv(mem, g, w_b, k_gain):
    b, m, d = mem.shape
    out = jax.ShapeDtypeStruct((b, m, MEM_WIDTH), BF16)
    return pl.pallas_call(
        _mem_kv_kernel,
        grid=(b,),
        in_specs=[
            pl.BlockSpec((None, m, d), lambda i: (i, 0, 0)),
            pl.BlockSpec((1, d), lambda i: (0, 0)),
            pl.BlockSpec((d, 2 * MEM_WIDTH), lambda i: (0, 0)),
            pl.BlockSpec((1, MEM_HEAD_DIM), lambda i: (0, 0)),
        ],
        out_specs=[pl.BlockSpec((None, m, MEM_WIDTH), lambda i: (i, 0, 0))] * 2,
        out_shape=[out, out],
        compiler_params=pltpu.CompilerParams(
            dimension_semantics=("parallel",), vmem_limit_bytes=VMEM_LIMIT),
        name="mem_kv",
    )(mem, g, w_b, k_gain)


_Q0, _K0, _V0 = 0, NA_WIDTH, 2 * NA_WIDTH
_H0 = 3 * NA_WIDTH
_B0 = _H0 + CONV_WIDTH
_C0 = _B0 + CONV_WIDTH
_QM0 = _C0 + CONV_WIDTH
_PROJ_CHUNK = 512


def _in_proj_kernel(x_ref, g_ref, w_ref, qg_ref, kg_ref, mqg_ref,
                    q_ref, k_ref, v_ref, p_ref, b_ref, qm_ref):
    n = _rms(x_ref[...], g_ref[...]).astype(BF16)

    def proj(c0):
        return jnp.dot(n, w_ref[:, c0:c0 + _PROJ_CHUNK], preferred_element_type=F32)

    for base, gain_ref, out_ref in ((_Q0, qg_ref, q_ref), (_K0, kg_ref, k_ref)):
        for cc in range(NA_WIDTH // _PROJ_CHUNK):
            z = proj(base + cc * _PROJ_CHUNK)
            for s in range(_PROJ_CHUNK // LANES):
                o = cc * _PROJ_CHUNK + s * LANES
                out_ref[:, o:o + LANES] = _pair_rms(
                    z[:, s * LANES:(s + 1) * LANES], gain_ref[...]).astype(BF16)
    for cc in range(NA_WIDTH // _PROJ_CHUNK):
        o = cc * _PROJ_CHUNK
        v_ref[:, o:o + _PROJ_CHUNK] = proj(_V0 + o).astype(BF16)
    p_ref[...] = (proj(_C0) * proj(_H0)).astype(BF16)
    b_ref[...] = proj(_B0).astype(BF16)
    z = proj(_QM0)
    for h in range(MEM_HEADS):
        c = slice(h * MEM_HEAD_DIM, (h + 1) * MEM_HEAD_DIM)
        qm_ref[:, c] = _rms(z[:, c], mqg_ref[...]).astype(BF16)


def _in_proj(x, g, w_b, q_gain2, k_gain2, mq_gain, tm):
    b, t, d = x.shape
    nt = t // tm
    row = lambda w: pl.BlockSpec((None, tm, w), lambda bi, i: (bi, i, 0))
    const = lambda shape: pl.BlockSpec(shape, lambda bi, i: (0,) * len(shape))
    sds = lambda w: jax.ShapeDtypeStruct((b, t, w), BF16)
    return pl.pallas_call(
        _in_proj_kernel,
        grid=(b, nt),
        in_specs=[row(d), const((1, d)), const(w_b.shape),
                  const((1, LANES)), const((1, LANES)), const((1, MEM_HEAD_DIM))],
        out_specs=[row(NA_WIDTH)] * 3 + [row(CONV_WIDTH)] * 2 + [row(MEM_WIDTH)],
        out_shape=[sds(NA_WIDTH)] * 3 + [sds(CONV_WIDTH)] * 2 + [sds(MEM_WIDTH)],
        compiler_params=pltpu.CompilerParams(
            dimension_semantics=("parallel", "parallel"), vmem_limit_bytes=VMEM_LIMIT),
        name="in_proj",
    )(x, g, w_b, q_gain2, k_gain2, mq_gain)


COL_BLOCKS = GRID_W // WIN_W
Q_BLOCKS = GRID_W // WIN_W
SLAB = 2 * WIN_W
_NEEDED_COL_BLOCKS = tuple(
    tuple(cb for cb in range(COL_BLOCKS)
          if any(max(0, min(q - WIN_W // 2, GRID_W - WIN_W)) < (cb + 1) * WIN_W
                 and max(0, min(q - WIN_W // 2, GRID_W - WIN_W)) + WIN_W > cb * WIN_W
                 for q in range(j * WIN_W, (j + 1) * WIN_W)))
    for j in range(Q_BLOCKS))


_SCORE_BLOCKS = tuple((j, cb) for j in range(Q_BLOCKS) for cb in _NEEDED_COL_BLOCKS[j])


def _bias_table(rel_bias):
    h = rel_bias.shape[0]
    n_dc = 2 * WIN_W - 1
    n_blk = len(_SCORE_BLOCKS)
    cr = np.arange(WIN_W)[:, None]
    cl = np.arange(WIN_W)[None, :]
    select = np.stack([(WIN_W * (x - 1) + cl - cr + WIN_W - 1)[..., None] == np.arange(n_dc)
                       for x in range(3)]).astype(np.float32)
    mask = np.zeros((n_blk, 2, WIN_W, WIN_H, WIN_W), np.float32)
    for n, (j, cb) in enumerate(_SCORE_BLOCKS):
        q, k = j * WIN_W + cr, cb * WIN_W + cl
        q_cs = np.clip(q - WIN_W // 2, 0, GRID_W - WIN_W)
        mask[n] = np.where((k >= q_cs) & (k < q_cs + WIN_W), 0.0, NEG_INF)[None, :, None, :]
    rb = rel_bias.astype(F32).reshape(h // 2, 2, N_DR, n_dc)
    pieces = jnp.einsum("pedt,xrlt->pedxrl", rb, select, precision=lax.Precision.HIGHEST)
    pieces = jnp.pad(pieces, ((0, 0),) * 5 + ((0, LANES - WIN_W),))
    return pl.pallas_call(
        _bias_expand_kernel,
        grid=(h // 2,),
        in_specs=[pl.BlockSpec((None,) + pieces.shape[1:], lambda p: (p, 0, 0, 0, 0, 0)),
                  pl.BlockSpec((n_blk, SLAB, LANES), lambda p: (0, 0, 0))],
        out_specs=pl.BlockSpec((WIN_H, None, n_blk, SLAB, LANES), lambda p: (0, p, 0, 0, 0)),
        out_shape=jax.ShapeDtypeStruct((WIN_H, h // 2, n_blk, SLAB, LANES), F32),
        name="bias_expand",
    )(pieces, mask.reshape(n_blk, SLAB, LANES))


def _bias_expand_kernel(piece_ref, mask_ref, o_ref):
    for n, (j, cb) in enumerate(_SCORE_BLOCKS):
        for d0 in range(WIN_H):
            halves = []
            for hd in range(2):
                acc = piece_ref[hd, d0, cb - j + 1]
                for wr in range(1, WIN_H):
                    acc = acc + pltpu.roll(piece_ref[hd, d0 + wr, cb - j + 1], wr * WIN_W, axis=1)
                halves.append(acc)
            o_ref[d0, n] = jnp.concatenate(halves, axis=0) + mask_ref[n]


def _mixers_kernel(q_ref, k_ref, kp_ref, kn_ref, v_ref, vp_ref, vn_ref,
                   p_ref, pp_ref, pn_ref, b_ref, qm_ref, mk_ref, mv_ref, bt_ref, cw_ref,
                   y_ref, kext, vext, pext, s0_ref, s1_ref, *, rows, tile_rows):
    i = pl.program_id(1)
    nt = pl.num_programs(1)
    tq = tile_rows * GRID_W
    halo = KEY_ROWS_HALO * GRID_W
    lo = _lo_lanes()
    n_hp = 2 * GRID_W

    for ext, prev, main, nxt in ((kext, kp_ref, k_ref, kn_ref), (vext, vp_ref, v_ref, vn_ref)):
        ext[0:halo] = prev[...]
        ext[halo:halo + tq] = main[...]
        ext[halo + tq:halo + tq + halo] = nxt[...]

    r0 = i * tile_rows

    def window(ext, start, c):
        return jnp.concatenate(
            [ext[pl.ds(pl.multiple_of(start + wr * GRID_W + cb * WIN_W, WIN_W), WIN_W), c]
             for cb in range(COL_BLOCKS) for wr in range(WIN_H)], axis=0)

    def row_geometry(lr):
        r = r0 + lr
        ws = jnp.clip(r - WIN_H // 2, 0, rows - WIN_H)
        start = pl.multiple_of((ws - r0 + KEY_ROWS_HALO) * GRID_W, GRID_W)
        dr0 = ws - r + (WIN_H - 1)
        return start, pl.multiple_of(lr * GRID_W, GRID_W), dr0

    def scores(lr, s_ref):
        start, qrow, dr0 = row_geometry(lr)
        zero = jnp.zeros((WIN_W, LANES), BF16)
        for quad in range(HEAD_PAIRS // 2):
            c2 = slice(2 * quad * LANES, (2 * quad + 2) * LANES)
            q4 = q_ref[pl.ds(qrow, GRID_W), c2]
            pieces = []
            for half in range(2):
                for j in range(Q_BLOCKS):
                    qj = q4[j * WIN_W:(j + 1) * WIN_W, half * LANES:(half + 1) * LANES]
                    for piece in (jnp.where(lo, qj, zero), jnp.where(lo, zero, qj)):
                        pieces.append(jnp.concatenate(
                            [piece, zero] if half == 0 else [zero, piece], axis=1))
            s = lax.dot_general(jnp.concatenate(pieces, axis=0), window(kext, start, c2),
                                (((1,), (1,)), ((), ())), preferred_element_type=F32)
            for half in range(2):
                hp = 2 * quad + half
                for n, (j, cb) in enumerate(_SCORE_BLOCKS):
                    rs = slice(j * SLAB, (j + 1) * SLAB)
                    cs = slice(cb * LANES, (cb + 1) * LANES)
                    s_ref[hp, rs, cs] = (s[half * n_hp + j * SLAB:half * n_hp + (j + 1) * SLAB, cs]
                                         + bt_ref[dr0, hp, n])

    def attend(lr, s_ref):
        start, qrow, _ = row_geometry(lr)
        for quad in range(HEAD_PAIRS // 2):
            c2 = slice(2 * quad * LANES, (2 * quad + 2) * LANES)
            p_rows, inv_l = [], []
            for half in range(2):
                hp = 2 * quad + half
                for j in range(Q_BLOCKS):
                    rs = slice(j * SLAB, (j + 1) * SLAB)
                    need = _NEEDED_COL_BLOCKS[j]
                    sb = [s_ref[hp, rs, cb * LANES:(cb + 1) * LANES] for cb in need]
                    m = jnp.max(functools.reduce(jnp.maximum, sb), axis=-1, keepdims=True)
                    e = [jnp.exp(x - m) for x in sb]
                    inv_l.append(
                        1.0 / jnp.sum(functools.reduce(jnp.add, e), axis=-1, keepdims=True))
                    zero = jnp.zeros((SLAB, LANES), BF16)
                    blocks = [zero] * COL_BLOCKS
                    for cb, x in zip(need, e):
                        blocks[cb] = x.astype(BF16)
                    p_rows.append(jnp.concatenate(blocks, axis=1))
            o = jnp.dot(jnp.concatenate(p_rows, axis=0), window(vext, start, c2),
                        preferred_element_type=F32)
            for half in range(2):
                c = slice((2 * quad + half) * LANES, (2 * quad + half + 1) * LANES)
                for j in range(Q_BLOCKS):
                    r_lo = half * n_hp + j * SLAB
                    oj = o[r_lo:r_lo + SLAB, half * LANES:(half + 1) * LANES]
                    oj = oj * inv_l[half * Q_BLOCKS + j]
                    y_ref[pl.ds(pl.multiple_of(qrow + j * WIN_W, WIN_W), WIN_W), c] = jnp.where(
                        lo, oj[:WIN_W], oj[WIN_W:]).astype(BF16)

    def gated_conv():
        hq = tq // 2
        before = jnp.where(i > 0, pp_ref[...].astype(F32)[F32_ROWS:], 0.0)
        after = jnp.where(i < nt - 1, pn_ref[...].astype(F32)[:F32_ROWS], 0.0)
        for r0h, lead, trail in ((0, before, None), (hq, None, after)):
            pext[0:F32_ROWS] = (p_ref[r0h - BF16_ROWS:r0h].astype(F32)[F32_ROWS:]
                                if lead is None else lead)
            pext[F32_ROWS:F32_ROWS + hq] = p_ref[r0h:r0h + hq].astype(F32)
            pext[F32_ROWS + hq:2 * F32_ROWS + hq] = (
                p_ref[r0h + hq:r0h + hq + BF16_ROWS].astype(F32)[:F32_ROWS]
                if trail is None else trail)
            conv = (cw_ref[0:1] * pext[F32_ROWS - 1:F32_ROWS - 1 + hq]
                    + cw_ref[1:2] * pext[F32_ROWS:F32_ROWS + hq]
                    + cw_ref[2:3] * pext[F32_ROWS + 1:F32_ROWS + 1 + hq])
            y_ref[r0h:r0h + hq, NA_WIDTH:NA_WIDTH + CONV_WIDTH] = (
                b_ref[r0h:r0h + hq].astype(F32) * conv).astype(BF16)

    def mem_attention():
        for h in range(MEM_HEADS):
            c = slice(h * MEM_HEAD_DIM, (h + 1) * MEM_HEAD_DIM)
            s = lax.dot_general(qm_ref[:, c], mk_ref[:, c], (((1,), (1,)), ((), ())),
                                preferred_element_type=F32) * (1.0 / math.sqrt(MEM_HEAD_DIM))
            o0 = NA_WIDTH + CONV_WIDTH + h * MEM_HEAD_DIM
            y_ref[:, o0:o0 + MEM_HEAD_DIM] = _softmax_pv(s, mv_ref[:, c]).astype(BF16)

    scores(0, s0_ref)

    def row_pair(k, carry):
        lr = 2 * k
        scores(lr + 1, s1_ref)
        attend(lr, s0_ref)
        scores(lr + 2, s0_ref)
        attend(lr + 1, s1_ref)
        return carry

    lax.fori_loop(0, tile_rows // 2 - 1, row_pair, 0)
    scores(tile_rows - 1, s1_ref)
    attend(tile_rows - 2, s0_ref)
    attend(tile_rows - 1, s1_ref)
    gated_conv()
    mem_attention()


def _mixers(q, k, v, p, bb, qm, mk, mv, bias_tab, conv_w, tile_rows):
    b, t, _ = q.shape
    rows = t // GRID_W
    tq = tile_rows * GRID_W
    nt = t // tq
    halo = KEY_ROWS_HALO * GRID_W
    assert rows >= WIN_H and rows % tile_rows == 0 and tile_rows >= KEY_ROWS_HALO
    assert tile_rows % 2 == 0
    assert tq % halo == 0 and tq % BF16_ROWS == 0
    n_mem = mk.shape[1]

    def main(w):
        return pl.BlockSpec((None, tq, w), lambda bi, i: (bi, i, 0))

    def prev(blk, w):
        per = tq // blk
        return pl.BlockSpec((None, blk, w), lambda bi, i: (bi, jnp.maximum(i * per - 1, 0), 0))

    def nxt(blk, w):
        per = tq // blk
        last = t // blk - 1
        return pl.BlockSpec((None, blk, w), lambda bi, i: (bi, jnp.minimum((i + 1) * per, last), 0))

    kv_specs = [main(NA_WIDTH), prev(halo, NA_WIDTH), nxt(halo, NA_WIDTH)]
    mem_spec = pl.BlockSpec((None, n_mem, MEM_WIDTH), lambda bi, i: (bi, 0, 0),
                            pipeline_mode=pl.Buffered(1))
    d_mix = NA_WIDTH + CONV_WIDTH + MEM_WIDTH
    return pl.pallas_call(
        functools.partial(_mixers_kernel, rows=rows, tile_rows=tile_rows),
        grid=(b, nt),
        in_specs=[main(NA_WIDTH)] + kv_specs + kv_specs
                 + [main(CONV_WIDTH), prev(BF16_ROWS, CONV_WIDTH), nxt(BF16_ROWS, CONV_WIDTH),
                    main(CONV_WIDTH), main(MEM_WIDTH), mem_spec, mem_spec,
                    pl.BlockSpec(bias_tab.shape, lambda bi, i: (0,) * bias_tab.ndim,
                                 pipeline_mode=pl.Buffered(1)),
                    pl.BlockSpec(conv_w.shape, lambda bi, i: (0, 0))],
        out_specs=main(d_mix),
        out_shape=jax.ShapeDtypeStruct((b, t, d_mix), BF16),
        scratch_shapes=[pltpu.VMEM((tq + 2 * halo, NA_WIDTH), BF16),
                        pltpu.VMEM((tq + 2 * halo, NA_WIDTH), BF16),
                        pltpu.VMEM((tq // 2 + 2 * F32_ROWS, CONV_WIDTH), F32),
                        pltpu.VMEM((HEAD_PAIRS, 2 * GRID_W, WIN_H * GRID_W), F32),
                        pltpu.VMEM((HEAD_PAIRS, 2 * GRID_W, WIN_H * GRID_W), F32)],
        compiler_params=pltpu.CompilerParams(
            dimension_semantics=("parallel", "parallel"), vmem_limit_bytes=VMEM_LIMIT),
        name="mixers",
    )(q, k, k, k, v, v, v, p, p, p, bb, qm, mk, mv, bias_tab, conv_w)


def _out_proj_kernel(x_ref, y_ref, w_ref, o_ref):
    o_ref[...] = x_ref[...] + jnp.dot(y_ref[...], w_ref[...], preferred_element_type=F32)


def _out_proj(x, y, w_b, tm):
    b, t, d = x.shape
    row = lambda w: pl.BlockSpec((None, tm, w), lambda bi, i: (bi, i, 0))
    return pl.pallas_call(
        _out_proj_kernel,
        grid=(b, t // tm),
        in_specs=[row(d), row(y.shape[-1]), pl.BlockSpec(w_b.shape, lambda bi, i: (0, 0))],
        out_specs=row(d),
        out_shape=jax.ShapeDtypeStruct((b, t, d), F32),
        compiler_params=pltpu.CompilerParams(
            dimension_semantics=("parallel", "parallel"), vmem_limit_bytes=VMEM_LIMIT),
        name="out_proj",
    )(x, y, w_b)


def _ffn_kernel(x_ref, xp_ref, xn_ref, g_ref, wa_ref, wg_ref, cw_ref, wo_ref,
                o_ref, next_ref, pa_ref, pg_ref, *, tm):
    i = pl.program_id(1)
    j = pl.program_id(2)
    nt = pl.num_programs(1)
    nj = pl.num_programs(2)
    h0 = BF16_ROWS

    @pl.when(j == 0)
    def _():
        g = g_ref[...]
        zeros = jnp.zeros((F32_ROWS, x_ref.shape[-1]), F32)
        prev = jnp.where(i > 0, _rms(xp_ref[...], g), 0.0)
        nxt = jnp.where(i < nt - 1, _rms(xn_ref[...], g), 0.0)
        next_ref[0:h0] = jnp.concatenate([zeros, prev], axis=0).astype(BF16)
        next_ref[h0:h0 + tm] = _rms(x_ref[...], g).astype(BF16)
        next_ref[h0 + tm:2 * h0 + tm] = jnp.concatenate([nxt, zeros], axis=0).astype(BF16)
        o_ref[...] = x_ref[...]

    n2 = next_ref[...]
    pa_ref[...] = jnp.dot(n2, wa_ref[...], preferred_element_type=F32)
    pg_ref[...] = jnp.dot(n2, wg_ref[...], preferred_element_type=F32)

    def conv(ref, cw):
        return (cw[0:1] * ref[h0 - 1:h0 - 1 + tm] + cw[1:2] * ref[h0:h0 + tm]
                + cw[2:3] * ref[h0 + 1:h0 + 1 + tm])

    hidden = (jax.nn.silu(conv(pa_ref, cw_ref[j])) * conv(pg_ref, cw_ref[nj + j])).astype(BF16)
    o_ref[...] += jnp.dot(hidden, wo_ref[...], preferred_element_type=F32)


def _ffn(x1, g, w_in_b, conv_w, w_out_b, tm, tf):
    b, t, d = x1.shape
    d_ff = w_out_b.shape[0]
    nj = d_ff // tf
    assert d_ff % tf == 0 and t % tm == 0 and tm % F32_ROWS == 0
    per = tm // F32_ROWS
    last = t // F32_ROWS - 1
    taps = conv_w.shape[0]
    conv_chunks = jnp.swapaxes(conv_w.reshape(taps, 2 * nj, tf), 0, 1)
    return pl.pallas_call(
        functools.partial(_ffn_kernel, tm=tm),
        grid=(b, t // tm, nj),
        in_specs=[
            pl.BlockSpec((None, tm, d), lambda bi, i, j: (bi, i, 0)),
            pl.BlockSpec((None, F32_ROWS, d), lambda bi, i, j: (bi, jnp.maximum(i * per - 1, 0), 0)),
            pl.BlockSpec((None, F32_ROWS, d),
                         lambda bi, i, j: (bi, jnp.minimum((i + 1) * per, last), 0)),
            pl.BlockSpec((1, d), lambda bi, i, j: (0, 0)),
            pl.BlockSpec((d, tf), lambda bi, i, j: (0, j)),
            pl.BlockSpec((d, tf), lambda bi, i, j: (0, nj + j)),
            pl.BlockSpec((2 * nj, taps, tf), lambda bi, i, j: (0, 0, 0)),
            pl.BlockSpec((tf, d), lambda bi, i, j: (j, 0)),
        ],
        out_specs=pl.BlockSpec((None, tm, d), lambda bi, i, j: (bi, i, 0)),
        out_shape=jax.ShapeDtypeStruct((b, t, d), F32),
        scratch_shapes=[pltpu.VMEM((tm + 2 * BF16_ROWS, d), BF16),
                        pltpu.VMEM((tm + 2 * BF16_ROWS, tf), F32),
                        pltpu.VMEM((tm + 2 * BF16_ROWS, tf), F32)],
        compiler_params=pltpu.CompilerParams(
            dimension_semantics=("parallel", "parallel", "arbitrary"),
            vmem_limit_bytes=VMEM_LIMIT),
        name="ffn",
    )(x1, x1, x1, g, w_in_b, w_in_b, conv_chunks, w_out_b)


_TM_PROJ = 512
_TILE_ROWS = 8
_TM_FFN = 512
_TF_FFN = 512


def _layer(x, mem, p):
    mk, mv = _mem_kv(mem, p["mem_norm_g"], p["w_mem_kv"], p["mem_k_gain"])
    q, k, v, ch, bb, qm = _in_proj(x, p["g_mix"], p["w_in"], p["na_q_gain"], p["na_k_gain"],
                                   p["mem_q_gain"], _TM_PROJ)
    y = _mixers(q, k, v, ch, bb, qm, mk, mv, p["bias_tab"], p["conv_w"], _TILE_ROWS)
    x1 = _out_proj(x, y, p["w_out"], _TM_PROJ)
    return _ffn(x1, p["g_ffn"], p["w_ffn_in"], p["ffn_conv_w"], p["w_ffn_out"], _TM_FFN, _TF_FFN)


def kernel(x_prompt, x_sample, mem_prompt, mem_sample, g_mix, w_in, na_q_gain, na_k_gain,
           na_rel_bias, conv_w, mem_norm_g, w_mem_kv, mem_q_gain, mem_k_gain, w_out,
           g_ffn, w_ffn_in, ffn_conv_w, w_ffn_out):
    y_prompt, y_sample = x_prompt, x_sample
    for l in range(g_mix.shape[0]):
        row = lambda a: a[l].reshape(1, -1).astype(F32)
        q_scale = 1.0 / math.sqrt(HEAD_DIM)
        p = dict(
            g_mix=row(g_mix), g_ffn=row(g_ffn), mem_norm_g=row(mem_norm_g),
            na_q_gain=jnp.tile(row(na_q_gain), (1, 2)) * q_scale,
            na_k_gain=jnp.tile(row(na_k_gain), (1, 2)),
            mem_q_gain=row(mem_q_gain), mem_k_gain=row(mem_k_gain),
            w_in=w_in[l].astype(BF16), w_mem_kv=w_mem_kv[l].astype(BF16),
            w_out=w_out[l].astype(BF16), w_ffn_in=w_ffn_in[l].astype(BF16),
            w_ffn_out=w_ffn_out[l].astype(BF16),
            conv_w=conv_w[l].astype(F32), ffn_conv_w=ffn_conv_w[l].astype(F32),
            bias_tab=_bias_table(na_rel_bias[l]),
        )
        y_prompt = _layer(y_prompt, mem_prompt, p)
        y_sample = _layer(y_sample, mem_sample, p)
    return (y_prompt, y_sample)
```

```python
import functools
import math

import jax
import jax.numpy as jnp
import numpy as np
from jax import lax
from jax.experimental import pallas as pl
from jax.experimental.pallas import tpu as pltpu

F32 = jnp.float32
BF16 = jnp.bfloat16

GRID_W = 64
WIN_H = 8
WIN_W = 16
HEAD_DIM = 64
NA_HEADS = 16
NA_WIDTH = NA_HEADS * HEAD_DIM
CONV_WIDTH = 512
MEM_HEADS = 4
MEM_HEAD_DIM = 128
MEM_WIDTH = MEM_HEADS * MEM_HEAD_DIM
EPS = 1e-6
NEG_INF = -1e30

LANES = 128
BF16_ROWS = 16
F32_ROWS = 8
VMEM_LIMIT = 58 * 1024 * 1024

HEAD_PAIRS = NA_HEADS // 2
KEY_ROWS_HALO = WIN_H // 2
N_DR = 2 * WIN_H - 1


def _rms(x, g):
    return x * lax.rsqrt(jnp.mean(x * x, axis=-1, keepdims=True) + EPS) * g


def _lo_lanes():
    return lax.broadcasted_iota(jnp.int32, (1, LANES), 1) < HEAD_DIM


def _pair_rms(z, g2):
    lo = _lo_lanes()
    sq = z * z
    s0 = jnp.sum(jnp.where(lo, sq, 0.0), axis=-1, keepdims=True)
    s1 = jnp.sum(jnp.where(lo, 0.0, sq), axis=-1, keepdims=True)
    ms = jnp.where(lo, s0, s1) * (1.0 / HEAD_DIM)
    return z * lax.rsqrt(ms + EPS) * g2


def _softmax_pv(s, v):
    m = jnp.max(s, axis=-1, keepdims=True)
    e = jnp.exp(s - m)
    l = jnp.sum(e, axis=-1, keepdims=True)
    o = jnp.dot(e.astype(BF16), v, preferred_element_type=F32)
    return o * (1.0 / l)


def _mem_kv_kernel(mem_ref, g_ref, w_ref, kg_ref, mk_ref, mv_ref):
    n = _rms(mem_ref[...], g_ref[...]).astype(BF16)
    z = jnp.dot(n, w_ref[...], preferred_element_type=F32)
    for h in range(MEM_HEADS):
        c = slice(h * MEM_HEAD_DIM, (h + 1) * MEM_HEAD_DIM)
        mk_ref[:, c] = _rms(z[:, c], kg_ref[...]).astype(BF16)
    mv_ref[...] = z[:, MEM_WIDTH:].astype(BF16)


def _mem_kv(mem, g, w_b, k_gain):
    b, m, d = mem.shape
    out = jax.ShapeDtypeStruct((b, m, MEM_WIDTH), BF16)
    return pl.pallas_call(
        _mem_kv_kernel,
        grid=(b,),
        in_specs=[
            pl.BlockSpec((None, m, d), lambda i: (i, 0, 0)),
            pl.BlockSpec((1, d), lambda i: (0, 0)),
            pl.BlockSpec((d, 2 * MEM_WIDTH), lambda i: (0, 0)),
            pl.BlockSpec((1, MEM_HEAD_DIM), lambda i: (0, 0)),
        ],
        out_specs=[pl.BlockSpec((None, m, MEM_WIDTH), lambda i: (i, 0, 0))] * 2,
        out_shape=[out, out],
        compiler_params=pltpu.CompilerParams(
            dimension_semantics=("parallel",), vmem_limit_bytes=VMEM_LIMIT),
        name="mem_kv",
    )(mem, g, w_b, k_gain)


_Q0, _K0, _V0 = 0, NA_WIDTH, 2 * NA_WIDTH
_H0 = 3 * NA_WIDTH
_B0 = _H0 + CONV_WIDTH
_C0 = _B0 + CONV_WIDTH
_QM0 = _C0 + CONV_WIDTH
_PROJ_CHUNK = 512


def _in_proj_kernel(x_ref, g_ref, w_ref, qg_ref, kg_ref, mqg_ref,
                    q_ref, k_ref, v_ref, p_ref, b_ref, qm_ref):
    n = _rms(x_ref[...], g_ref[...]).astype(BF16)

    def proj(c0):
        return jnp.dot(n, w_ref[:, c0:c0 + _PROJ_CHUNK], preferred_element_type=F32)

    for base, gain_ref, out_ref in ((_Q0, qg_ref, q_ref), (_K0, kg_ref, k_ref)):
        for cc in range(NA_WIDTH // _PROJ_CHUNK):
            z = proj(base + cc * _PROJ_CHUNK)
            for s in range(_PROJ_CHUNK // LANES):
                o = cc * _PROJ_CHUNK + s * LANES
                out_ref[:, o:o + LANES] = _pair_rms(
                    z[:, s * LANES:(s + 1) * LANES], gain_ref[...]).astype(BF16)
    for cc in range(NA_WIDTH // _PROJ_CHUNK):
        o = cc * _PROJ_CHUNK
        v_ref[:, o:o + _PROJ_CHUNK] = proj(_V0 + o).astype(BF16)
    p_ref[...] = (proj(_C0) * proj(_H0)).astype(BF16)
    b_ref[...] = proj(_B0).astype(BF16)
    z = proj(_QM0)
    for h in range(MEM_HEADS):
        c = slice(h * MEM_HEAD_DIM, (h + 1) * MEM_HEAD_DIM)
        qm_ref[:, c] = _rms(z[:, c], mqg_ref[...]).astype(BF16)


def _in_proj(x, g, w_b, q_gain2, k_gain2, mq_gain, tm):
    b, t, d = x.shape
    nt = t // tm
    row = lambda w: pl.BlockSpec((None, tm, w), lambda bi, i: (bi, i, 0))
    const = lambda shape: pl.BlockSpec(shape, lambda bi, i: (0,) * len(shape))
    sds = lambda w: jax.ShapeDtypeStruct((b, t, w), BF16)
    return pl.pallas_call(
        _in_proj_kernel,
        grid=(b, nt),
        in_specs=[row(d), const((1, d)), const(w_b.shape),
                  const((1, LANES)), const((1, LANES)), const((1, MEM_HEAD_DIM))],
        out_specs=[row(NA_WIDTH)] * 3 + [row(CONV_WIDTH)] * 2 + [row(MEM_WIDTH)],
        out_shape=[sds(NA_WIDTH)] * 3 + [sds(CONV_WIDTH)] * 2 + [sds(MEM_WIDTH)],
        compiler_params=pltpu.CompilerParams(
            dimension_semantics=("parallel", "parallel"), vmem_limit_bytes=VMEM_LIMIT),
        name="in_proj",
    )(x, g, w_b, q_gain2, k_gain2, mq_gain)


COL_BLOCKS = GRID_W // WIN_W
Q_BLOCKS = GRID_W // WIN_W
SLAB = 2 * WIN_W
_NEEDED_COL_BLOCKS = tuple(
    tuple(cb for cb in range(COL_BLOCKS)
          if any(max(0, min(q - WIN_W // 2, GRID_W - WIN_W)) < (cb + 1) * WIN_W
                 and max(0, min(q - WIN_W // 2, GRID_W - WIN_W)) + WIN_W > cb * WIN_W
                 for q in range(j * WIN_W, (j + 1) * WIN_W)))
    for j in range(Q_BLOCKS))


_SCORE_BLOCKS = tuple((j, cb) for j in range(Q_BLOCKS) for cb in _NEEDED_COL_BLOCKS[j])


def _bias_table(rel_bias):
    h = rel_bias.shape[0]
    n_dc = 2 * WIN_W - 1
    n_blk = len(_SCORE_BLOCKS)
    cr = np.arange(WIN_W)[:, None]
    cl = np.arange(WIN_W)[None, :]
    select = np.stack([(WIN_W * (x - 1) + cl - cr + WIN_W - 1)[..., None] == np.arange(n_dc)
                       for x in range(3)]).astype(np.float32)
    mask = np.zeros((n_blk, 2, WIN_W, WIN_H, WIN_W), np.float32)
    for n, (j, cb) in enumerate(_SCORE_BLOCKS):
        q, k = j * WIN_W + cr, cb * WIN_W + cl
        q_cs = np.clip(q - WIN_W // 2, 0, GRID_W - WIN_W)
        mask[n] = np.where((k >= q_cs) & (k < q_cs + WIN_W), 0.0, NEG_INF)[None, :, None, :]
    rb = rel_bias.astype(F32).reshape(h // 2, 2, N_DR, n_dc)
    pieces = jnp.einsum("pedt,xrlt->pedxrl", rb, select, precision=lax.Precision.HIGHEST)
    pieces = jnp.pad(pieces, ((0, 0),) * 5 + ((0, LANES - WIN_W),))
    return pl.pallas_call(
        _bias_expand_kernel,
        grid=(h // 2,),
        in_specs=[pl.BlockSpec((None,) + pieces.shape[1:], lambda p: (p, 0, 0, 0, 0, 0)),
                  pl.BlockSpec((n_blk, SLAB, LANES), lambda p: (0, 0, 0))],
        out_specs=pl.BlockSpec((WIN_H, None, n_blk, SLAB, LANES), lambda p: (0, p, 0, 0, 0)),
        out_shape=jax.ShapeDtypeStruct((WIN_H, h // 2, n_blk, SLAB, LANES), F32),
        name="bias_expand",
    )(pieces, mask.reshape(n_blk, SLAB, LANES))


def _bias_expand_kernel(piece_ref, mask_ref, o_ref):
    for n, (j, cb) in enumerate(_SCORE_BLOCKS):
        for d0 in range(WIN_H):
            halves = []
            for hd in range(2):
                acc = piece_ref[hd, d0, cb - j + 1]
                for wr in range(1, WIN_H):
                    acc = acc + pltpu.roll(piece_ref[hd, d0 + wr, cb - j + 1], wr * WIN_W, axis=1)
                halves.append(acc)
            o_ref[d0, n] = jnp.concatenate(halves, axis=0) + mask_ref[n]


def _mixers_kernel(q_ref, k_ref, kp_ref, kn_ref, v_ref, vp_ref, vn_ref,
                   p_ref, pp_ref, pn_ref, b_ref, qm_ref, mk_ref, mv_ref, bt_ref, cw_ref,
                   y_ref, kext, vext, pext, s0_ref, s1_ref, *, rows, tile_rows):
    i = pl.program_id(1)
    nt = pl.num_programs(1)
    tq = tile_rows * GRID_W
    halo = KEY_ROWS_HALO * GRID_W
    lo = _lo_lanes()
    n_hp = 2 * GRID_W

    for ext, prev, main, nxt in ((kext, kp_ref, k_ref, kn_ref), (vext, vp_ref, v_ref, vn_ref)):
        ext[0:halo] = prev[...]
        ext[halo:halo + tq] = main[...]
        ext[halo + tq:halo + tq + halo] = nxt[...]

    r0 = i * tile_rows

    def window(ext, start, c):
        return jnp.concatenate(
            [ext[pl.ds(pl.multiple_of(start + wr * GRID_W + cb * WIN_W, WIN_W), WIN_W), c]
             for cb in range(COL_BLOCKS) for wr in range(WIN_H)], axis=0)

    def row_geometry(lr):
        r = r0 + lr
        ws = jnp.clip(r - WIN_H // 2, 0, rows - WIN_H)
        start = pl.multiple_of((ws - r0 + KEY_ROWS_HALO) * GRID_W, GRID_W)
        dr0 = ws - r + (WIN_H - 1)
        return start, pl.multiple_of(lr * GRID_W, GRID_W), dr0

    def scores(lr, s_ref):
        start, qrow, dr0 = row_geometry(lr)
        zero = jnp.zeros((WIN_W, LANES), BF16)
        for quad in range(HEAD_PAIRS // 2):
            c2 = slice(2 * quad * LANES, (2 * quad + 2) * LANES)
            q4 = q_ref[pl.ds(qrow, GRID_W), c2]
            pieces = []
            for half in range(2):
                for j in range(Q_BLOCKS):
                    qj = q4[j * WIN_W:(j + 1) * WIN_W, half * LANES:(half + 1) * LANES]
                    for piece in (jnp.where(lo, qj, zero), jnp.where(lo, zero, qj)):
                        pieces.append(jnp.concatenate(
                            [piece, zero] if half == 0 else [zero, piece], axis=1))
            s = lax.dot_general(jnp.concatenate(pieces, axis=0), window(kext, start, c2),
                                (((1,), (1,)), ((), ())), preferred_element_type=F32)
            for half in range(2):
                hp = 2 * quad + half
                for n, (j, cb) in enumerate(_SCORE_BLOCKS):
                    rs = slice(j * SLAB, (j + 1) * SLAB)
                    cs = slice(cb * LANES, (cb + 1) * LANES)
                    s_ref[hp, rs, cs] = (s[half * n_hp + j * SLAB:half * n_hp + (j + 1) * SLAB, cs]
                                         + bt_ref[dr0, hp, n])

    def attend(lr, s_ref):
        start, qrow, _ = row_geometry(lr)
        for quad in range(HEAD_PAIRS // 2):
            c2 = slice(2 * quad * LANES, (2 * quad + 2) * LANES)
            p_rows, inv_l = [], []
            for half in range(2):
                hp = 2 * quad + half
                for j in range(Q_BLOCKS):
                    rs = slice(j * SLAB, (j + 1) * SLAB)
                    need = _NEEDED_COL_BLOCKS[j]
                    sb = [s_ref[hp, rs, cb * LANES:(cb + 1) * LANES] for cb in need]
                    m = jnp.max(functools.reduce(jnp.maximum, sb), axis=-1, keepdims=True)
                    e = [jnp.exp(x - m) for x in sb]
                    inv_l.append(
                        1.0 / jnp.sum(functools.reduce(jnp.add, e), axis=-1, keepdims=True))
                    zero = jnp.zeros((SLAB, LANES), BF16)
                    blocks = [zero] * COL_BLOCKS
                    for cb, x in zip(need, e):
                        blocks[cb] = x.astype(BF16)
                    p_rows.append(jnp.concatenate(blocks, axis=1))
            o = jnp.dot(jnp.concatenate(p_rows, axis=0), window(vext, start, c2),
                        preferred_element_type=F32)
            for half in range(2):
                c = slice((2 * quad + half) * LANES, (2 * quad + half + 1) * LANES)
                for j in range(Q_BLOCKS):
                    r_lo = half * n_hp + j * SLAB
                    oj = o[r_lo:r_lo + SLAB, half * LANES:(half + 1) * LANES]
                    oj = oj * inv_l[half * Q_BLOCKS + j]
                    y_ref[pl.ds(pl.multiple_of(qrow + j * WIN_W, WIN_W), WIN_W), c] = jnp.where(
                        lo, oj[:WIN_W], oj[WIN_W:]).astype(BF16)

    def gated_conv():
        hq = tq // 2
        before = jnp.where(i > 0, pp_ref[...].astype(F32)[F32_ROWS:], 0.0)
        after = jnp.where(i < nt - 1, pn_ref[...].astype(F32)[:F32_ROWS], 0.0)
        for r0h, lead, trail in ((0, before, None), (hq, None, after)):
            pext[0:F32_ROWS] = (p_ref[r0h - BF16_ROWS:r0h].astype(F32)[F32_ROWS:]
                                if lead is None else lead)
            pext[F32_ROWS:F32_ROWS + hq] = p_ref[r0h:r0h + hq].astype(F32)
            pext[F32_ROWS + hq:2 * F32_ROWS + hq] = (
                p_ref[r0h + hq:r0h + hq + BF16_ROWS].astype(F32)[:F32_ROWS]
                if trail is None else trail)
            conv = (cw_ref[0:1] * pext[F32_ROWS - 1:F32_ROWS - 1 + hq]
                    + cw_ref[1:2] * pext[F32_ROWS:F32_ROWS + hq]
                    + cw_ref[2:3] * pext[F32_ROWS + 1:F32_ROWS + 1 + hq])
            y_ref[r0h:r0h + hq, NA_WIDTH:NA_WIDTH + CONV_WIDTH] = (
                b_ref[r0h:r0h + hq].astype(F32) * conv).astype(BF16)

    def mem_attention():
        for h in range(MEM_HEADS):
            c = slice(h * MEM_HEAD_DIM, (h + 1) * MEM_HEAD_DIM)
            s = lax.dot_general(qm_ref[:, c], mk_ref[:, c], (((1,), (1,)), ((), ())),
                                preferred_element_type=F32) * (1.0 / math.sqrt(MEM_HEAD_DIM))
            o0 = NA_WIDTH + CONV_WIDTH + h * MEM_HEAD_DIM
            y_ref[:, o0:o0 + MEM_HEAD_DIM] = _softmax_pv(s, mv_ref[:, c]).astype(BF16)

    scores(0, s0_ref)

    def row_pair(k, carry):
        lr = 2 * k
        scores(lr + 1, s1_ref)
        attend(lr, s0_ref)
        scores(lr + 2, s0_ref)
        attend(lr + 1, s1_ref)
        return carry

    lax.fori_loop(0, tile_rows // 2 - 1, row_pair, 0)
    scores(tile_rows - 1, s1_ref)
    attend(tile_rows - 2, s0_ref)
    attend(tile_rows - 1, s1_ref)
    gated_conv()
    mem_attention()


def _mixers(q, k, v, p, bb, qm, mk, mv, bias_tab, conv_w, tile_rows):
    b, t, _ = q.shape
    rows = t // GRID_W
    tq = tile_rows * GRID_W
    nt = t // tq
    halo = KEY_ROWS_HALO * GRID_W
    assert rows >= WIN_H and rows % tile_rows == 0 and tile_rows >= KEY_ROWS_HALO
    assert tile_rows % 2 == 0
    assert tq % halo == 0 and tq % BF16_ROWS == 0
    n_mem = mk.shape[1]

    def main(w):
        return pl.BlockSpec((None, tq, w), lambda bi, i: (bi, i, 0))

    def prev(blk, w):
        per = tq // blk
        return pl.BlockSpec((None, blk, w), lambda bi, i: (bi, jnp.maximum(i * per - 1, 0), 0))

    def nxt(blk, w):
        per = tq // blk
        last = t // blk - 1
        return pl.BlockSpec((None, blk, w), lambda bi, i: (bi, jnp.minimum((i + 1) * per, last), 0))

    kv_specs = [main(NA_WIDTH), prev(halo, NA_WIDTH), nxt(halo, NA_WIDTH)]
    mem_spec = pl.BlockSpec((None, n_mem, MEM_WIDTH), lambda bi, i: (bi, 0, 0),
                            pipeline_mode=pl.Buffered(1))
    d_mix = NA_WIDTH + CONV_WIDTH + MEM_WIDTH
    return pl.pallas_call(
        functools.partial(_mixers_kernel, rows=rows, tile_rows=tile_rows),
        grid=(b, nt),
        in_specs=[main(NA_WIDTH)] + kv_specs + kv_specs
                 + [main(CONV_WIDTH), prev(BF16_ROWS, CONV_WIDTH), nxt(BF16_ROWS, CONV_WIDTH),
                    main(CONV_WIDTH), main(MEM_WIDTH), mem_spec, mem_spec,
                    pl.BlockSpec(bias_tab.shape, lambda bi, i: (0,) * bias_tab.ndim,
                                 pipeline_mode=pl.Buffered(1)),
                    pl.BlockSpec(conv_w.shape, lambda bi, i: (0, 0))],
        out_specs=main(d_mix),
        out_shape=jax.ShapeDtypeStruct((b, t, d_mix), BF16),
        scratch_shapes=[pltpu.VMEM((tq + 2 * halo, NA_WIDTH), BF16),
                        pltpu.VMEM((tq + 2 * halo, NA_WIDTH), BF16),
                        pltpu.VMEM((tq // 2 + 2 * F32_ROWS, CONV_WIDTH), F32),
                        pltpu.VMEM((HEAD_PAIRS, 2 * GRID_W, WIN_H * GRID_W), F32),
                        pltpu.VMEM((HEAD_PAIRS, 2 * GRID_W, WIN_H * GRID_W), F32)],
        compiler_params=pltpu.CompilerParams(
            dimension_semantics=("parallel", "parallel"), vmem_limit_bytes=VMEM_LIMIT),
        name="mixers",
    )(q, k, k, k, v, v, v, p, p, p, bb, qm, mk, mv, bias_tab, conv_w)


def _out_proj_kernel(x_ref, y_ref, w_ref, o_ref):
    o_ref[...] = x_ref[...] + jnp.dot(y_ref[...], w_ref[...], preferred_element_type=F32)


def _out_proj(x, y, w_b, tm):
    b, t, d = x.shape
    row = lambda w: pl.BlockSpec((None, tm, w), lambda bi, i: (bi, i, 0))
    return pl.pallas_call(
        _out_proj_kernel,
        grid=(b, t // tm),
        in_specs=[row(d), row(y.shape[-1]), pl.BlockSpec(w_b.shape, lambda bi, i: (0, 0))],
        out_specs=row(d),
        out_shape=jax.ShapeDtypeStruct((b, t, d), F32),
        compiler_params=pltpu.CompilerParams(
            dimension_semantics=("parallel", "parallel"), vmem_limit_bytes=VMEM_LIMIT),
        name="out_proj",
    )(x, y, w_b)


_FFN_OUT_CHUNK = 512
_FFN_UP_CHUNK = 256
_FFN_NORM_ROWS = 256


def _ffn_kernel(x_ref, xp_ref, xn_ref, g_ref, wa_ref, wg_ref, cw_ref, wo_ref,
                o_ref, next_ref, pa_ref, pg_ref, *, tm):
    i = pl.program_id(1)
    j = pl.program_id(2)
    nt = pl.num_programs(1)
    nj = pl.num_programs(2)
    h0 = BF16_ROWS

    @pl.when(j == 0)
    def _():
        g = g_ref[...]
        zeros = jnp.zeros((F32_ROWS, x_ref.shape[-1]), F32)
        prev = jnp.where(i > 0, _rms(xp_ref[...], g), 0.0)
        nxt = jnp.where(i < nt - 1, _rms(xn_ref[...], g), 0.0)
        next_ref[0:h0] = jnp.concatenate([zeros, prev], axis=0).astype(BF16)
        for r in range(0, tm, _FFN_NORM_ROWS):
            xr = x_ref[r:r + _FFN_NORM_ROWS]
            next_ref[h0 + r:h0 + r + _FFN_NORM_ROWS] = _rms(xr, g).astype(BF16)
            o_ref[r:r + _FFN_NORM_ROWS] = xr
        next_ref[h0 + tm:2 * h0 + tm] = jnp.concatenate([nxt, zeros], axis=0).astype(BF16)

    n2 = next_ref[...]
    for dst_ref, w_ref in ((pa_ref, wa_ref), (pg_ref, wg_ref)):
        for n0 in range(0, w_ref.shape[-1], _FFN_UP_CHUNK):
            c = slice(n0, n0 + _FFN_UP_CHUNK)
            dst_ref[:, c] = jnp.dot(n2, w_ref[:, c], preferred_element_type=F32)

    def conv(ref, cw):
        return (cw[0:1] * ref[h0 - 1:h0 - 1 + tm] + cw[1:2] * ref[h0:h0 + tm]
                + cw[2:3] * ref[h0 + 1:h0 + 1 + tm])

    hidden = (jax.nn.silu(conv(pa_ref, cw_ref[j])) * conv(pg_ref, cw_ref[nj + j])).astype(BF16)
    for n0 in range(0, o_ref.shape[-1], _FFN_OUT_CHUNK):
        c = slice(n0, n0 + _FFN_OUT_CHUNK)
        o_ref[:, c] += jnp.dot(hidden, wo_ref[:, c], preferred_element_type=F32)


def _ffn(x1, g, w_in_b, conv_w, w_out_b, tm, tf):
    b, t, d = x1.shape
    d_ff = w_out_b.shape[0]
    nj = d_ff // tf
    assert d_ff % tf == 0 and t % tm == 0 and tm % F32_ROWS == 0
    per = tm // F32_ROWS
    last = t // F32_ROWS - 1
    taps = conv_w.shape[0]
    conv_chunks = jnp.swapaxes(conv_w.reshape(taps, 2 * nj, tf), 0, 1)
    return pl.pallas_call(
        functools.partial(_ffn_kernel, tm=tm),
        grid=(b, t // tm, nj),
        in_specs=[
            pl.BlockSpec((None, tm, d), lambda bi, i, j: (bi, i, 0)),
            pl.BlockSpec((None, F32_ROWS, d), lambda bi, i, j: (bi, jnp.maximum(i * per - 1, 0), 0)),
            pl.BlockSpec((None, F32_ROWS, d),
                         lambda bi, i, j: (bi, jnp.minimum((i + 1) * per, last), 0)),
            pl.BlockSpec((1, d), lambda bi, i, j: (0, 0)),
            pl.BlockSpec((d, tf), lambda bi, i, j: (0, j)),
            pl.BlockSpec((d, tf), lambda bi, i, j: (0, nj + j)),
            pl.BlockSpec((2 * nj, taps, tf), lambda bi, i, j: (0, 0, 0)),
            pl.BlockSpec((tf, d), lambda bi, i, j: (j, 0)),
        ],
        out_specs=pl.BlockSpec((None, tm, d), lambda bi, i, j: (bi, i, 0)),
        out_shape=jax.ShapeDtypeStruct((b, t, d), F32),
        scratch_shapes=[pltpu.VMEM((tm + 2 * BF16_ROWS, d), BF16),
                        pltpu.VMEM((tm + 2 * BF16_ROWS, tf), F32),
                        pltpu.VMEM((tm + 2 * BF16_ROWS, tf), F32)],
        compiler_params=pltpu.CompilerParams(
            dimension_semantics=("parallel", "parallel", "arbitrary"),
            vmem_limit_bytes=VMEM_LIMIT),
        name="ffn",
    )(x1, x1, x1, g, w_in_b, w_in_b, conv_chunks, w_out_b)


_TM_PROJ = 512
_TILE_ROWS = 8
_TM_FFN = 1024
_TF_FFN = 512


def _layer(x, mem, p):
    mk, mv = _mem_kv(mem, p["mem_norm_g"], p["w_mem_kv"], p["mem_k_gain"])
    q, k, v, ch, bb, qm = _in_proj(x, p["g_mix"], p["w_in"], p["na_q_gain"], p["na_k_gain"],
                                   p["mem_q_gain"], _TM_PROJ)
    y = _mixers(q, k, v, ch, bb, qm, mk, mv, p["bias_tab"], p["conv_w"], _TILE_ROWS)
    x1 = _out_proj(x, y, p["w_out"], _TM_PROJ)
    return _ffn(x1, p["g_ffn"], p["w_ffn_in"], p["ffn_conv_w"], p["w_ffn_out"], _TM_FFN, _TF_FFN)


def kernel(x_prompt, x_sample, mem_prompt, mem_sample, g_mix, w_in, na_q_gain, na_k_gain,
           na_rel_bias, conv_w, mem_norm_g, w_mem_kv, mem_q_gain, mem_k_gain, w_out,
           g_ffn, w_ffn_in, ffn_conv_w, w_ffn_out):
    y_prompt, y_sample = x_prompt, x_sample
    for l in range(g_mix.shape[0]):
        row = lambda a: a[l].reshape(1, -1).astype(F32)
        q_scale = 1.0 / math.sqrt(HEAD_DIM)
        p = dict(
            g_mix=row(g_mix), g_ffn=row(g_ffn), mem_norm_g=row(mem_norm_g),
            na_q_gain=jnp.tile(row(na_q_gain), (1, 2)) * q_scale,
            na_k_gain=jnp.tile(row(na_k_gain), (1, 2)),
            mem_q_gain=row(mem_q_gain), mem_k_gain=row(mem_k_gain),
            w_in=w_in[l].astype(BF16), w_mem_kv=w_mem_kv[l].astype(BF16),
            w_out=w_out[l].astype(BF16), w_ffn_in=w_ffn_in[l].astype(BF16),
            w_ffn_out=w_ffn_out[l].astype(BF16),
            conv_w=conv_w[l].astype(F32), ffn_conv_w=ffn_conv_w[l].astype(F32),
            bias_tab=_bias_table(na_rel_bias[l]),
        )
        y_prompt = _layer(y_prompt, mem_prompt, p)
        y_sample = _layer(y_sample, mem_sample, p)
    return (y_prompt, y_sample)
```

```python
import functools
import math

import jax
import jax.numpy as jnp
import numpy as np
from jax import lax
from jax.experimental import pallas as pl
from jax.experimental.pallas import tpu as pltpu

F32 = jnp.float32
BF16 = jnp.bfloat16

GRID_W = 64
WIN_H = 8
WIN_W = 16
HEAD_DIM = 64
NA_HEADS = 16
NA_WIDTH = NA_HEADS * HEAD_DIM
CONV_WIDTH = 512
MEM_HEADS = 4
MEM_HEAD_DIM = 128
MEM_WIDTH = MEM_HEADS * MEM_HEAD_DIM
EPS = 1e-6
NEG_INF = -1e30

LANES = 128
BF16_ROWS = 16
F32_ROWS = 8
VMEM_LIMIT = 58 * 1024 * 1024

HEAD_PAIRS = NA_HEADS // 2
KEY_ROWS_HALO = WIN_H // 2
N_DR = 2 * WIN_H - 1


def _rms(x, g):
    return x * lax.rsqrt(jnp.mean(x * x, axis=-1, keepdims=True) + EPS) * g


def _lo_lanes():
    return lax.broadcasted_iota(jnp.int32, (1, LANES), 1) < HEAD_DIM


def _pair_rms(z, g2):
    lo = _lo_lanes()
    sq = z * z
    s0 = jnp.sum(jnp.where(lo, sq, 0.0), axis=-1, keepdims=True)
    s1 = jnp.sum(jnp.where(lo, 0.0, sq), axis=-1, keepdims=True)
    ms = jnp.where(lo, s0, s1) * (1.0 / HEAD_DIM)
    return z * lax.rsqrt(ms + EPS) * g2


def _softmax_pv(s, v):
    m = jnp.max(s, axis=-1, keepdims=True)
    e = jnp.exp(s - m)
    l = jnp.sum(e, axis=-1, keepdims=True)
    o = jnp.dot(e.astype(BF16), v, preferred_element_type=F32)
    return o * (1.0 / l)


def _mem_kv_kernel(mem_ref, g_ref, w_ref, kg_ref, mk_ref, mv_ref):
    n = _rms(mem_ref[...], g_ref[...]).astype(BF16)
    z = jnp.dot(n, w_ref[...], preferred_element_type=F32)
    for h in range(MEM_HEADS):
        c = slice(h * MEM_HEAD_DIM, (h + 1) * MEM_HEAD_DIM)
        mk_ref[:, c] = _rms(z[:, c], kg_ref[...]).astype(BF16)
    mv_ref[...] = z[:, MEM_WIDTH:].astype(BF16)


def _mem_kv(mem, g, w_b, k_gain):
    b, m, d = mem.shape
    out = jax.ShapeDtypeStruct((b, m, MEM_WIDTH), BF16)
    return pl.pallas_call(
        _mem_kv_kernel,
        grid=(b,),
        in_specs=[
            pl.BlockSpec((None, m, d), lambda i: (i, 0, 0)),
            pl.BlockSpec((1, d), lambda i: (0, 0)),
            pl.BlockSpec((d, 2 * MEM_WIDTH), lambda i: (0, 0)),
            pl.BlockSpec((1, MEM_HEAD_DIM), lambda i: (0, 0)),
        ],
        out_specs=[pl.BlockSpec((None, m, MEM_WIDTH), lambda i: (i, 0, 0))] * 2,
        out_shape=[out, out],
        compiler_params=pltpu.CompilerParams(
            dimension_semantics=("parallel",), vmem_limit_bytes=VMEM_LIMIT),
        name="mem_kv",
    )(mem, g, w_b, k_gain)


_Q0, _K0, _V0 = 0, NA_WIDTH, 2 * NA_WIDTH
_H0 = 3 * NA_WIDTH
_B0 = _H0 + CONV_WIDTH
_C0 = _B0 + CONV_WIDTH
_QM0 = _C0 + CONV_WIDTH
_PROJ_CHUNK = 512


def _in_proj_kernel(x_ref, g_ref, w_ref, qg_ref, kg_ref, mqg_ref,
                    q_ref, k_ref, v_ref, p_ref, b_ref, qm_ref):
    n = _rms(x_ref[...], g_ref[...]).astype(BF16)

    def proj(c0):
        return jnp.dot(n, w_ref[:, c0:c0 + _PROJ_CHUNK], preferred_element_type=F32)

    for base, gain_ref, out_ref in ((_Q0, qg_ref, q_ref), (_K0, kg_ref, k_ref)):
        for cc in range(NA_WIDTH // _PROJ_CHUNK):
            z = proj(base + cc * _PROJ_CHUNK)
            for s in range(_PROJ_CHUNK // LANES):
                o = cc * _PROJ_CHUNK + s * LANES
                out_ref[:, o:o + LANES] = _pair_rms(
                    z[:, s * LANES:(s + 1) * LANES], gain_ref[...]).astype(BF16)
    for cc in range(NA_WIDTH // _PROJ_CHUNK):
        o = cc * _PROJ_CHUNK
        v_ref[:, o:o + _PROJ_CHUNK] = proj(_V0 + o).astype(BF16)
    p_ref[...] = (proj(_C0) * proj(_H0)).astype(BF16)
    b_ref[...] = proj(_B0).astype(BF16)
    z = proj(_QM0)
    for h in range(MEM_HEADS):
        c = slice(h * MEM_HEAD_DIM, (h + 1) * MEM_HEAD_DIM)
        qm_ref[:, c] = _rms(z[:, c], mqg_ref[...]).astype(BF16)


def _in_proj(x, g, w_b, q_gain2, k_gain2, mq_gain, tm):
    b, t, d = x.shape
    nt = t // tm
    row = lambda w: pl.BlockSpec((None, tm, w), lambda bi, i: (bi, i, 0))
    const = lambda shape: pl.BlockSpec(shape, lambda bi, i: (0,) * len(shape))
    sds = lambda w: jax.ShapeDtypeStruct((b, t, w), BF16)
    return pl.pallas_call(
        _in_proj_kernel,
        grid=(b, nt),
        in_specs=[row(d), const((1, d)), const(w_b.shape),
                  const((1, LANES)), const((1, LANES)), const((1, MEM_HEAD_DIM))],
        out_specs=[row(NA_WIDTH)] * 3 + [row(CONV_WIDTH)] * 2 + [row(MEM_WIDTH)],
        out_shape=[sds(NA_WIDTH)] * 3 + [sds(CONV_WIDTH)] * 2 + [sds(MEM_WIDTH)],
        compiler_params=pltpu.CompilerParams(
            dimension_semantics=("parallel", "parallel"), vmem_limit_bytes=VMEM_LIMIT),
        name="in_proj",
    )(x, g, w_b, q_gain2, k_gain2, mq_gain)


COL_BLOCKS = GRID_W // WIN_W
Q_BLOCKS = GRID_W // WIN_W
SLAB = 2 * WIN_W
_NEEDED_COL_BLOCKS = tuple(
    tuple(cb for cb in range(COL_BLOCKS)
          if any(max(0, min(q - WIN_W // 2, GRID_W - WIN_W)) < (cb + 1) * WIN_W
                 and max(0, min(q - WIN_W // 2, GRID_W - WIN_W)) + WIN_W > cb * WIN_W
                 for q in range(j * WIN_W, (j + 1) * WIN_W)))
    for j in range(Q_BLOCKS))


_SCORE_BLOCKS = tuple((j, cb) for j in range(Q_BLOCKS) for cb in _NEEDED_COL_BLOCKS[j])


def _bias_table(rel_bias):
    h = rel_bias.shape[0]
    n_dc = 2 * WIN_W - 1
    n_blk = len(_SCORE_BLOCKS)
    cr = np.arange(WIN_W)[:, None]
    cl = np.arange(WIN_W)[None, :]
    select = np.stack([(WIN_W * (x - 1) + cl - cr + WIN_W - 1)[..., None] == np.arange(n_dc)
                       for x in range(3)]).astype(np.float32)
    mask = np.zeros((n_blk, 2, WIN_W, WIN_H, WIN_W), np.float32)
    for n, (j, cb) in enumerate(_SCORE_BLOCKS):
        q, k = j * WIN_W + cr, cb * WIN_W + cl
        q_cs = np.clip(q - WIN_W // 2, 0, GRID_W - WIN_W)
        mask[n] = np.where((k >= q_cs) & (k < q_cs + WIN_W), 0.0, NEG_INF)[None, :, None, :]
    rb = rel_bias.astype(F32).reshape(h // 2, 2, N_DR, n_dc)
    pieces = jnp.einsum("pedt,xrlt->pedxrl", rb, select, precision=lax.Precision.HIGHEST)
    pieces = jnp.pad(pieces, ((0, 0),) * 5 + ((0, LANES - WIN_W),))
    return pl.pallas_call(
        _bias_expand_kernel,
        grid=(h // 2,),
        in_specs=[pl.BlockSpec((None,) + pieces.shape[1:], lambda p: (p, 0, 0, 0, 0, 0)),
                  pl.BlockSpec((n_blk, SLAB, LANES), lambda p: (0, 0, 0))],
        out_specs=pl.BlockSpec((WIN_H, None, n_blk, SLAB, LANES), lambda p: (0, p, 0, 0, 0)),
        out_shape=jax.ShapeDtypeStruct((WIN_H, h // 2, n_blk, SLAB, LANES), F32),
        name="bias_expand",
    )(pieces, mask.reshape(n_blk, SLAB, LANES))


def _bias_expand_kernel(piece_ref, mask_ref, o_ref):
    for n, (j, cb) in enumerate(_SCORE_BLOCKS):
        for d0 in range(WIN_H):
            halves = []
            for hd in range(2):
                acc = piece_ref[hd, d0, cb - j + 1]
                for wr in range(1, WIN_H):
                    acc = acc + pltpu.roll(piece_ref[hd, d0 + wr, cb - j + 1], wr * WIN_W, axis=1)
                halves.append(acc)
            o_ref[d0, n] = jnp.concatenate(halves, axis=0) + mask_ref[n]


def _mixers_kernel(q_ref, k_ref, kp_ref, kn_ref, v_ref, vp_ref, vn_ref,
                   p_ref, pp_ref, pn_ref, b_ref, qm_ref, mk_ref, mv_ref, bt_ref, cw_ref,
                   y_ref, kext, vext, pext, s0_ref, s1_ref, *, rows, tile_rows):
    i = pl.program_id(1)
    nt = pl.num_programs(1)
    tq = tile_rows * GRID_W
    halo = KEY_ROWS_HALO * GRID_W
    lo = _lo_lanes()
    n_hp = 2 * GRID_W

    for ext, prev, main, nxt in ((kext, kp_ref, k_ref, kn_ref), (vext, vp_ref, v_ref, vn_ref)):
        ext[0:halo] = prev[...]
        ext[halo:halo + tq] = main[...]
        ext[halo + tq:halo + tq + halo] = nxt[...]

    r0 = i * tile_rows

    def window(ext, start, c):
        return jnp.concatenate(
            [ext[pl.ds(pl.multiple_of(start + wr * GRID_W + cb * WIN_W, WIN_W), WIN_W), c]
             for cb in range(COL_BLOCKS) for wr in range(WIN_H)], axis=0)

    def row_geometry(lr):
        r = r0 + lr
        ws = jnp.clip(r - WIN_H // 2, 0, rows - WIN_H)
        start = pl.multiple_of((ws - r0 + KEY_ROWS_HALO) * GRID_W, GRID_W)
        dr0 = ws - r + (WIN_H - 1)
        return start, pl.multiple_of(lr * GRID_W, GRID_W), dr0

    def scores(lr, s_ref):
        start, qrow, dr0 = row_geometry(lr)
        zero = jnp.zeros((WIN_W, LANES), BF16)
        for quad in range(HEAD_PAIRS // 2):
            c2 = slice(2 * quad * LANES, (2 * quad + 2) * LANES)
            q4 = q_ref[pl.ds(qrow, GRID_W), c2]
            pieces = []
            for half in range(2):
                for j in range(Q_BLOCKS):
                    qj = q4[j * WIN_W:(j + 1) * WIN_W, half * LANES:(half + 1) * LANES]
                    for piece in (jnp.where(lo, qj, zero), jnp.where(lo, zero, qj)):
                        pieces.append(jnp.concatenate(
                            [piece, zero] if half == 0 else [zero, piece], axis=1))
            s = lax.dot_general(jnp.concatenate(pieces, axis=0), window(kext, start, c2),
                                (((1,), (1,)), ((), ())), preferred_element_type=F32)
            for half in range(2):
                hp = 2 * quad + half
                for n, (j, cb) in enumerate(_SCORE_BLOCKS):
                    rs = slice(j * SLAB, (j + 1) * SLAB)
                    cs = slice(cb * LANES, (cb + 1) * LANES)
                    s_ref[hp, rs, cs] = (s[half * n_hp + j * SLAB:half * n_hp + (j + 1) * SLAB, cs]
                                         + bt_ref[dr0, hp, n])

    def attend(lr, s_ref):
        start, qrow, _ = row_geometry(lr)
        for quad in range(HEAD_PAIRS // 2):
            c2 = slice(2 * quad * LANES, (2 * quad + 2) * LANES)
            p_rows, inv_l = [], []
            for half in range(2):
                hp = 2 * quad + half
                for j in range(Q_BLOCKS):
                    rs = slice(j * SLAB, (j + 1) * SLAB)
                    need = _NEEDED_COL_BLOCKS[j]
                    sb = [s_ref[hp, rs, cb * LANES:(cb + 1) * LANES] for cb in need]
                    m = jnp.max(functools.reduce(jnp.maximum, sb), axis=-1, keepdims=True)
                    e = [jnp.exp(x - m) for x in sb]
                    inv_l.append(
                        1.0 / jnp.sum(functools.reduce(jnp.add, e), axis=-1, keepdims=True))
                    zero = jnp.zeros((SLAB, LANES), BF16)
                    blocks = [zero] * COL_BLOCKS
                    for cb, x in zip(need, e):
                        blocks[cb] = x.astype(BF16)
                    p_rows.append(jnp.concatenate(blocks, axis=1))
            o = jnp.dot(jnp.concatenate(p_rows, axis=0), window(vext, start, c2),
                        preferred_element_type=F32)
            for half in range(2):
                c = slice((2 * quad + half) * LANES, (2 * quad + half + 1) * LANES)
                for j in range(Q_BLOCKS):
                    r_lo = half * n_hp + j * SLAB
                    oj = o[r_lo:r_lo + SLAB, half * LANES:(half + 1) * LANES]
                    oj = oj * inv_l[half * Q_BLOCKS + j]
                    y_ref[pl.ds(pl.multiple_of(qrow + j * WIN_W, WIN_W), WIN_W), c] = jnp.where(
                        lo, oj[:WIN_W], oj[WIN_W:]).astype(BF16)

    def gated_conv():
        hq = tq // 2
        before = jnp.where(i > 0, pp_ref[...].astype(F32)[F32_ROWS:], 0.0)
        after = jnp.where(i < nt - 1, pn_ref[...].astype(F32)[:F32_ROWS], 0.0)
        for r0h, lead, trail in ((0, before, None), (hq, None, after)):
            pext[0:F32_ROWS] = (p_ref[r0h - BF16_ROWS:r0h].astype(F32)[F32_ROWS:]
                                if lead is None else lead)
            pext[F32_ROWS:F32_ROWS + hq] = p_ref[r0h:r0h + hq].astype(F32)
            pext[F32_ROWS + hq:2 * F32_ROWS + hq] = (
                p_ref[r0h + hq:r0h + hq + BF16_ROWS].astype(F32)[:F32_ROWS]
                if trail is None else trail)
            conv = (cw_ref[0:1] * pext[F32_ROWS - 1:F32_ROWS - 1 + hq]
                    + cw_ref[1:2] * pext[F32_ROWS:F32_ROWS + hq]
                    + cw_ref[2:3] * pext[F32_ROWS + 1:F32_ROWS + 1 + hq])
            y_ref[r0h:r0h + hq, NA_WIDTH:NA_WIDTH + CONV_WIDTH] = (
                b_ref[r0h:r0h + hq].astype(F32) * conv).astype(BF16)

    def mem_attention():
        for h in range(MEM_HEADS):
            c = slice(h * MEM_HEAD_DIM, (h + 1) * MEM_HEAD_DIM)
            s = lax.dot_general(qm_ref[:, c], mk_ref[:, c], (((1,), (1,)), ((), ())),
                                preferred_element_type=F32) * (1.0 / math.sqrt(MEM_HEAD_DIM))
            o0 = NA_WIDTH + CONV_WIDTH + h * MEM_HEAD_DIM
            y_ref[:, o0:o0 + MEM_HEAD_DIM] = _softmax_pv(s, mv_ref[:, c]).astype(BF16)

    scores(0, s0_ref)

    def row_pair(k, carry):
        lr = 2 * k
        scores(lr + 1, s1_ref)
        attend(lr, s0_ref)
        scores(lr + 2, s0_ref)
        attend(lr + 1, s1_ref)
        return carry

    lax.fori_loop(0, tile_rows // 2 - 1, row_pair, 0)
    scores(tile_rows - 1, s1_ref)
    attend(tile_rows - 2, s0_ref)
    attend(tile_rows - 1, s1_ref)
    gated_conv()
    mem_attention()


def _mixers(q, k, v, p, bb, qm, mk, mv, bias_tab, conv_w, tile_rows):
    b, t, _ = q.shape
    rows = t // GRID_W
    tq = tile_rows * GRID_W
    nt = t // tq
    halo = KEY_ROWS_HALO * GRID_W
    assert rows >= WIN_H and rows % tile_rows == 0 and tile_rows >= KEY_ROWS_HALO
    assert tile_rows % 2 == 0
    assert tq % halo == 0 and tq % BF16_ROWS == 0
    n_mem = mk.shape[1]

    def main(w):
        return pl.BlockSpec((None, tq, w), lambda bi, i: (bi, i, 0))

    def prev(blk, w):
        per = tq // blk
        return pl.BlockSpec((None, blk, w), lambda bi, i: (bi, jnp.maximum(i * per - 1, 0), 0))

    def nxt(blk, w):
        per = tq // blk
        last = t // blk - 1
        return pl.BlockSpec((None, blk, w), lambda bi, i: (bi, jnp.minimum((i + 1) * per, last), 0))

    kv_specs = [main(NA_WIDTH), prev(halo, NA_WIDTH), nxt(halo, NA_WIDTH)]
    mem_spec = pl.BlockSpec((None, n_mem, MEM_WIDTH), lambda bi, i: (bi, 0, 0),
                            pipeline_mode=pl.Buffered(1))
    d_mix = NA_WIDTH + CONV_WIDTH + MEM_WIDTH
    return pl.pallas_call(
        functools.partial(_mixers_kernel, rows=rows, tile_rows=tile_rows),
        grid=(b, nt),
        in_specs=[main(NA_WIDTH)] + kv_specs + kv_specs
                 + [main(CONV_WIDTH), prev(BF16_ROWS, CONV_WIDTH), nxt(BF16_ROWS, CONV_WIDTH),
                    main(CONV_WIDTH), main(MEM_WIDTH), mem_spec, mem_spec,
                    pl.BlockSpec(bias_tab.shape, lambda bi, i: (0,) * bias_tab.ndim,
                                 pipeline_mode=pl.Buffered(1)),
                    pl.BlockSpec(conv_w.shape, lambda bi, i: (0, 0))],
        out_specs=main(d_mix),
        out_shape=jax.ShapeDtypeStruct((b, t, d_mix), BF16),
        scratch_shapes=[pltpu.VMEM((tq + 2 * halo, NA_WIDTH), BF16),
                        pltpu.VMEM((tq + 2 * halo, NA_WIDTH), BF16),
                        pltpu.VMEM((tq // 2 + 2 * F32_ROWS, CONV_WIDTH), F32),
                        pltpu.VMEM((HEAD_PAIRS, 2 * GRID_W, WIN_H * GRID_W), F32),
                        pltpu.VMEM((HEAD_PAIRS, 2 * GRID_W, WIN_H * GRID_W), F32)],
        compiler_params=pltpu.CompilerParams(
            dimension_semantics=("parallel", "parallel"), vmem_limit_bytes=VMEM_LIMIT),
        name="mixers",
    )(q, k, k, k, v, v, v, p, p, p, bb, qm, mk, mv, bias_tab, conv_w)


def _out_proj_kernel(x_ref, y_ref, w_ref, o_ref):
    o_ref[...] = x_ref[...] + jnp.dot(y_ref[...], w_ref[...], preferred_element_type=F32)


def _out_proj(x, y, w_b, tm):
    b, t, d = x.shape
    row = lambda w: pl.BlockSpec((None, tm, w), lambda bi, i: (bi, i, 0))
    return pl.pallas_call(
        _out_proj_kernel,
        grid=(b, t // tm),
        in_specs=[row(d), row(y.shape[-1]), pl.BlockSpec(w_b.shape, lambda bi, i: (0, 0))],
        out_specs=row(d),
        out_shape=jax.ShapeDtypeStruct((b, t, d), F32),
        compiler_params=pltpu.CompilerParams(
            dimension_semantics=("parallel", "parallel"), vmem_limit_bytes=VMEM_LIMIT),
        name="out_proj",
    )(x, y, w_b)


_FFN_OUT_CHUNK = 2048
_FFN_UP_CHUNK = 256
_FFN_NORM_ROWS = 256
_FFN_GATE_ROWS = 256


def _ffn_kernel(x_ref, xp_ref, xn_ref, g_ref, wa_ref, wg_ref, cw_ref, wo_ref,
                o_ref, next_ref, pa_ref, pg_ref, *, tm):
    i = pl.program_id(1)
    j = pl.program_id(2)
    nt = pl.num_programs(1)
    nj = pl.num_programs(2)
    h0 = BF16_ROWS

    @pl.when(j == 0)
    def _():
        g = g_ref[...]
        zeros = jnp.zeros((F32_ROWS, x_ref.shape[-1]), F32)
        prev = jnp.where(i > 0, _rms(xp_ref[...], g), 0.0)
        nxt = jnp.where(i < nt - 1, _rms(xn_ref[...], g), 0.0)
        next_ref[0:h0] = jnp.concatenate([zeros, prev], axis=0).astype(BF16)
        for r in range(0, tm, _FFN_NORM_ROWS):
            xr = x_ref[r:r + _FFN_NORM_ROWS]
            next_ref[h0 + r:h0 + r + _FFN_NORM_ROWS] = _rms(xr, g).astype(BF16)
            o_ref[r:r + _FFN_NORM_ROWS] = xr
        next_ref[h0 + tm:2 * h0 + tm] = jnp.concatenate([nxt, zeros], axis=0).astype(BF16)

    n2 = next_ref[...]
    ups = [(dst_ref, w_ref, slice(n0, n0 + _FFN_UP_CHUNK))
           for dst_ref, w_ref in ((pa_ref, wa_ref), (pg_ref, wg_ref))
           for n0 in range(0, w_ref.shape[-1], _FFN_UP_CHUNK)]
    for dst_ref, w_ref, c in ups[:-1]:
        dst_ref[:, c] = jnp.dot(n2, w_ref[:, c], preferred_element_type=F32)
    dst_ref, w_ref, c = ups[-1]
    head = h0 + _FFN_GATE_ROWS + BF16_ROWS
    dst_ref[0:head, c] = jnp.dot(n2[0:head], w_ref[:, c], preferred_element_type=F32)
    dst_ref[head:, c] = jnp.dot(n2[head:], w_ref[:, c], preferred_element_type=F32)

    def conv(ref, cw, r, n):
        return (cw[0:1] * ref[h0 - 1 + r:h0 - 1 + r + n] + cw[1:2] * ref[h0 + r:h0 + r + n]
                + cw[2:3] * ref[h0 + 1 + r:h0 + 1 + r + n])

    for r in range(0, tm, _FFN_GATE_ROWS):
        n = _FFN_GATE_ROWS
        hidden = (jax.nn.silu(conv(pa_ref, cw_ref[j], r, n))
                  * conv(pg_ref, cw_ref[nj + j], r, n)).astype(BF16)
        for n0 in range(0, o_ref.shape[-1], _FFN_OUT_CHUNK):
            c = slice(n0, n0 + _FFN_OUT_CHUNK)
            o_ref[r:r + n, c] += jnp.dot(hidden, wo_ref[:, c], preferred_element_type=F32)


def _ffn(x1, g, w_in_b, conv_w, w_out_b, tm, tf):
    b, t, d = x1.shape
    d_ff = w_out_b.shape[0]
    nj = d_ff // tf
    assert d_ff % tf == 0 and t % tm == 0 and tm % F32_ROWS == 0
    per = tm // F32_ROWS
    last = t // F32_ROWS - 1
    taps = conv_w.shape[0]
    conv_chunks = jnp.swapaxes(conv_w.reshape(taps, 2 * nj, tf), 0, 1)
    return pl.pallas_call(
        functools.partial(_ffn_kernel, tm=tm),
        grid=(b, t // tm, nj),
        in_specs=[
            pl.BlockSpec((None, tm, d), lambda bi, i, j: (bi, i, 0)),
            pl.BlockSpec((None, F32_ROWS, d), lambda bi, i, j: (bi, jnp.maximum(i * per - 1, 0), 0)),
            pl.BlockSpec((None, F32_ROWS, d),
                         lambda bi, i, j: (bi, jnp.minimum((i + 1) * per, last), 0)),
            pl.BlockSpec((1, d), lambda bi, i, j: (0, 0)),
            pl.BlockSpec((d, tf), lambda bi, i, j: (0, j)),
            pl.BlockSpec((d, tf), lambda bi, i, j: (0, nj + j)),
            pl.BlockSpec((2 * nj, taps, tf), lambda bi, i, j: (0, 0, 0)),
            pl.BlockSpec((tf, d), lambda bi, i, j: (j, 0)),
        ],
        out_specs=pl.BlockSpec((None, tm, d), lambda bi, i, j: (bi, i, 0)),
        out_shape=jax.ShapeDtypeStruct((b, t, d), F32),
        scratch_shapes=[pltpu.VMEM((tm + 2 * BF16_ROWS, d), BF16),
                        pltpu.VMEM((tm + 2 * BF16_ROWS, tf), F32),
                        pltpu.VMEM((tm + 2 * BF16_ROWS, tf), F32)],
        compiler_params=pltpu.CompilerParams(
            dimension_semantics=("parallel", "parallel", "arbitrary"),
            vmem_limit_bytes=VMEM_LIMIT),
        name="ffn",
    )(x1, x1, x1, g, w_in_b, w_in_b, conv_chunks, w_out_b)


_TM_PROJ = 512
_TILE_ROWS = 8
_TM_FFN = 1024
_TF_FFN = 512


def _layer(x, mem, p):
    mk, mv = _mem_kv(mem, p["mem_norm_g"], p["w_mem_kv"], p["mem_k_gain"])
    q, k, v, ch, bb, qm = _in_proj(x, p["g_mix"], p["w_in"], p["na_q_gain"], p["na_k_gain"],
                                   p["mem_q_gain"], _TM_PROJ)
    y = _mixers(q, k, v, ch, bb, qm, mk, mv, p["bias_tab"], p["conv_w"], _TILE_ROWS)
    x1 = _out_proj(x, y, p["w_out"], _TM_PROJ)
    return _ffn(x1, p["g_ffn"], p["w_ffn_in"], p["ffn_conv_w"], p["w_ffn_out"], _TM_FFN, _TF_FFN)


def kernel(x_prompt, x_sample, mem_prompt, mem_sample, g_mix, w_in, na_q_gain, na_k_gain,
           na_rel_bias, conv_w, mem_norm_g, w_mem_kv, mem_q_gain, mem_k_gain, w_out,
           g_ffn, w_ffn_in, ffn_conv_w, w_ffn_out):
    y_prompt, y_sample = x_prompt, x_sample
    for l in range(g_mix.shape[0]):
        row = lambda a: a[l].reshape(1, -1).astype(F32)
        q_scale = 1.0 / math.sqrt(HEAD_DIM)
        p = dict(
            g_mix=row(g_mix), g_ffn=row(g_ffn), mem_norm_g=row(mem_norm_g),
            na_q_gain=jnp.tile(row(na_q_gain), (1, 2)) * q_scale,
            na_k_gain=jnp.tile(row(na_k_gain), (1, 2)),
            mem_q_gain=row(mem_q_gain), mem_k_gain=row(mem_k_gain),
            w_in=w_in[l].astype(BF16), w_mem_kv=w_mem_kv[l].astype(BF16),
            w_out=w_out[l].astype(BF16), w_ffn_in=w_ffn_in[l].astype(BF16),
            w_ffn_out=w_ffn_out[l].astype(BF16),
            conv_w=conv_w[l].astype(F32), ffn_conv_w=ffn_conv_w[l].astype(F32),
            bias_tab=_bias_table(na_rel_bias[l]),
        )
        y_prompt = _layer(y_prompt, mem_prompt, p)
        y_sample = _layer(y_sample, mem_sample, p)
    return (y_prompt, y_sample)
```

```python
import functools
import math

import jax
import jax.numpy as jnp
import numpy as np
from jax import lax
from jax.experimental import pallas as pl
from jax.experimental.pallas import tpu as pltpu

F32 = jnp.float32
BF16 = jnp.bfloat16

GRID_W = 64
WIN_H = 8
WIN_W = 16
HEAD_DIM = 64
NA_HEADS = 16
NA_WIDTH = NA_HEADS * HEAD_DIM
CONV_WIDTH = 512
MEM_HEADS = 4
MEM_HEAD_DIM = 128
MEM_WIDTH = MEM_HEADS * MEM_HEAD_DIM
EPS = 1e-6
NEG_INF = -1e30

LANES = 128
BF16_ROWS = 16
F32_ROWS = 8
VMEM_LIMIT = 58 * 1024 * 1024

HEAD_PAIRS = NA_HEADS // 2
KEY_ROWS_HALO = WIN_H // 2
N_DR = 2 * WIN_H - 1


def _rms(x, g):
    return x * lax.rsqrt(jnp.mean(x * x, axis=-1, keepdims=True) + EPS) * g


def _lo_lanes():
    return lax.broadcasted_iota(jnp.int32, (1, LANES), 1) < HEAD_DIM


def _pair_rms(z, g2):
    lo = _lo_lanes()
    sq = z * z
    s0 = jnp.sum(jnp.where(lo, sq, 0.0), axis=-1, keepdims=True)
    s1 = jnp.sum(jnp.where(lo, 0.0, sq), axis=-1, keepdims=True)
    ms = jnp.where(lo, s0, s1) * (1.0 / HEAD_DIM)
    return z * lax.rsqrt(ms + EPS) * g2


def _softmax_pv(s, v):
    m = jnp.max(s, axis=-1, keepdims=True)
    e = jnp.exp(s - m)
    l = jnp.sum(e, axis=-1, keepdims=True)
    o = jnp.dot(e.astype(BF16), v, preferred_element_type=F32)
    return o * (1.0 / l)


def _mem_kv_kernel(mem_ref, g_ref, w_ref, kg_ref, mk_ref, mv_ref):
    n = _rms(mem_ref[...], g_ref[...]).astype(BF16)
    z = jnp.dot(n, w_ref[...], preferred_element_type=F32)
    for h in range(MEM_HEADS):
        c = slice(h * MEM_HEAD_DIM, (h + 1) * MEM_HEAD_DIM)
        mk_ref[:, c] = _rms(z[:, c], kg_ref[...]).astype(BF16)
    mv_ref[...] = z[:, MEM_WIDTH:].astype(BF16)


def _mem_kv(mem, g, w_b, k_gain):
    b, m, d = mem.shape
    out = jax.ShapeDtypeStruct((b, m, MEM_WIDTH), BF16)
    return pl.pallas_call(
        _mem_kv_kernel,
        grid=(b,),
        in_specs=[
            pl.BlockSpec((None, m, d), lambda i: (i, 0, 0)),
            pl.BlockSpec((1, d), lambda i: (0, 0)),
            pl.BlockSpec((d, 2 * MEM_WIDTH), lambda i: (0, 0)),
            pl.BlockSpec((1, MEM_HEAD_DIM), lambda i: (0, 0)),
        ],
        out_specs=[pl.BlockSpec((None, m, MEM_WIDTH), lambda i: (i, 0, 0))] * 2,
        out_shape=[out, out],
        compiler_params=pltpu.CompilerParams(
            dimension_semantics=("parallel",), vmem_limit_bytes=VMEM_LIMIT),
        name="mem_kv",
    )(mem, g, w_b, k_gain)


_Q0, _K0, _V0 = 0, NA_WIDTH, 2 * NA_WIDTH
_H0 = 3 * NA_WIDTH
_B0 = _H0 + CONV_WIDTH
_C0 = _B0 + CONV_WIDTH
_QM0 = _C0 + CONV_WIDTH
_PROJ_CHUNK = 512


def _in_proj_kernel(x_ref, g_ref, w_ref, qg_ref, kg_ref, mqg_ref,
                    q_ref, k_ref, v_ref, p_ref, b_ref, qm_ref):
    n = _rms(x_ref[...], g_ref[...]).astype(BF16)

    def proj(c0):
        return jnp.dot(n, w_ref[:, c0:c0 + _PROJ_CHUNK], preferred_element_type=F32)

    for base, gain_ref, out_ref in ((_Q0, qg_ref, q_ref), (_K0, kg_ref, k_ref)):
        for cc in range(NA_WIDTH // _PROJ_CHUNK):
            z = proj(base + cc * _PROJ_CHUNK)
            for s in range(_PROJ_CHUNK // LANES):
                o = cc * _PROJ_CHUNK + s * LANES
                out_ref[:, o:o + LANES] = _pair_rms(
                    z[:, s * LANES:(s + 1) * LANES], gain_ref[...]).astype(BF16)
    for cc in range(NA_WIDTH // _PROJ_CHUNK):
        o = cc * _PROJ_CHUNK
        v_ref[:, o:o + _PROJ_CHUNK] = proj(_V0 + o).astype(BF16)
    p_ref[...] = (proj(_C0) * proj(_H0)).astype(BF16)
    b_ref[...] = proj(_B0).astype(BF16)
    z = proj(_QM0)
    for h in range(MEM_HEADS):
        c = slice(h * MEM_HEAD_DIM, (h + 1) * MEM_HEAD_DIM)
        qm_ref[:, c] = _rms(z[:, c], mqg_ref[...]).astype(BF16)


def _in_proj(x, g, w_b, q_gain2, k_gain2, mq_gain, tm):
    b, t, d = x.shape
    nt = t // tm
    row = lambda w: pl.BlockSpec((None, tm, w), lambda bi, i: (bi, i, 0))
    const = lambda shape: pl.BlockSpec(shape, lambda bi, i: (0,) * len(shape))
    sds = lambda w: jax.ShapeDtypeStruct((b, t, w), BF16)
    return pl.pallas_call(
        _in_proj_kernel,
        grid=(b, nt),
        in_specs=[row(d), const((1, d)), const(w_b.shape),
                  const((1, LANES)), const((1, LANES)), const((1, MEM_HEAD_DIM))],
        out_specs=[row(NA_WIDTH)] * 3 + [row(CONV_WIDTH)] * 2 + [row(MEM_WIDTH)],
        out_shape=[sds(NA_WIDTH)] * 3 + [sds(CONV_WIDTH)] * 2 + [sds(MEM_WIDTH)],
        compiler_params=pltpu.CompilerParams(
            dimension_semantics=("parallel", "parallel"), vmem_limit_bytes=VMEM_LIMIT),
        name="in_proj",
    )(x, g, w_b, q_gain2, k_gain2, mq_gain)


COL_BLOCKS = GRID_W // WIN_W
Q_BLOCKS = GRID_W // WIN_W
SLAB = 2 * WIN_W
_NEEDED_COL_BLOCKS = tuple(
    tuple(cb for cb in range(COL_BLOCKS)
          if any(max(0, min(q - WIN_W // 2, GRID_W - WIN_W)) < (cb + 1) * WIN_W
                 and max(0, min(q - WIN_W // 2, GRID_W - WIN_W)) + WIN_W > cb * WIN_W
                 for q in range(j * WIN_W, (j + 1) * WIN_W)))
    for j in range(Q_BLOCKS))


_SCORE_BLOCKS = tuple((j, cb) for j in range(Q_BLOCKS) for cb in _NEEDED_COL_BLOCKS[j])


def _bias_table(rel_bias):
    h = rel_bias.shape[0]
    n_dc = 2 * WIN_W - 1
    n_blk = len(_SCORE_BLOCKS)
    cr = np.arange(WIN_W)[:, None]
    cl = np.arange(WIN_W)[None, :]
    select = np.stack([(WIN_W * (x - 1) + cl - cr + WIN_W - 1)[..., None] == np.arange(n_dc)
                       for x in range(3)]).astype(np.float32)
    mask = np.zeros((n_blk, 2, WIN_W, WIN_H, WIN_W), np.float32)
    for n, (j, cb) in enumerate(_SCORE_BLOCKS):
        q, k = j * WIN_W + cr, cb * WIN_W + cl
        q_cs = np.clip(q - WIN_W // 2, 0, GRID_W - WIN_W)
        mask[n] = np.where((k >= q_cs) & (k < q_cs + WIN_W), 0.0, NEG_INF)[None, :, None, :]
    rb = rel_bias.astype(F32).reshape(h // 2, 2, N_DR, n_dc)
    pieces = jnp.einsum("pedt,xrlt->pedxrl", rb, select, precision=lax.Precision.HIGHEST)
    pieces = jnp.pad(pieces, ((0, 0),) * 5 + ((0, LANES - WIN_W),))
    return pl.pallas_call(
        _bias_expand_kernel,
        grid=(h // 2,),
        in_specs=[pl.BlockSpec((None,) + pieces.shape[1:], lambda p: (p, 0, 0, 0, 0, 0)),
                  pl.BlockSpec((n_blk, SLAB, LANES), lambda p: (0, 0, 0))],
        out_specs=pl.BlockSpec((WIN_H, None, n_blk, SLAB, LANES), lambda p: (0, p, 0, 0, 0)),
        out_shape=jax.ShapeDtypeStruct((WIN_H, h // 2, n_blk, SLAB, LANES), F32),
        name="bias_expand",
    )(pieces, mask.reshape(n_blk, SLAB, LANES))


def _bias_expand_kernel(piece_ref, mask_ref, o_ref):
    def window_rows(x, d0):
        halves = []
        for hd in range(2):
            acc = piece_ref[hd, d0, x]
            for wr in range(1, WIN_H):
                acc = acc + pltpu.roll(piece_ref[hd, d0 + wr, x], wr * WIN_W, axis=1)
            halves.append(acc)
        return jnp.concatenate(halves, axis=0)

    for d0 in range(WIN_H):
        unmasked = [window_rows(x, d0) for x in range(piece_ref.shape[2])]
        for n, (j, cb) in enumerate(_SCORE_BLOCKS):
            o_ref[d0, n] = unmasked[cb - j + 1] + mask_ref[n]


def _mixers_kernel(q_ref, k_ref, kp_ref, kn_ref, v_ref, vp_ref, vn_ref,
                   p_ref, pp_ref, pn_ref, b_ref, qm_ref, mk_ref, mv_ref, bt_ref, cw_ref,
                   y_ref, kext, vext, pext, s0_ref, s1_ref, *, rows, tile_rows):
    i = pl.program_id(1)
    nt = pl.num_programs(1)
    tq = tile_rows * GRID_W
    halo = KEY_ROWS_HALO * GRID_W
    lo = _lo_lanes()
    n_hp = 2 * GRID_W

    for ext, prev, main, nxt in ((kext, kp_ref, k_ref, kn_ref), (vext, vp_ref, v_ref, vn_ref)):
        ext[0:halo] = prev[...]
        ext[halo:halo + tq] = main[...]
        ext[halo + tq:halo + tq + halo] = nxt[...]

    r0 = i * tile_rows

    def window(ext, start, c):
        return jnp.concatenate(
            [ext[pl.ds(pl.multiple_of(start + wr * GRID_W + cb * WIN_W, WIN_W), WIN_W), c]
             for cb in range(COL_BLOCKS) for wr in range(WIN_H)], axis=0)

    def row_geometry(lr):
        r = r0 + lr
        ws = jnp.clip(r - WIN_H // 2, 0, rows - WIN_H)
        start = pl.multiple_of((ws - r0 + KEY_ROWS_HALO) * GRID_W, GRID_W)
        dr0 = ws - r + (WIN_H - 1)
        return start, pl.multiple_of(lr * GRID_W, GRID_W), dr0

    def scores(lr, s_ref):
        start, qrow, dr0 = row_geometry(lr)
        zero = jnp.zeros((WIN_W, LANES), BF16)
        for quad in range(HEAD_PAIRS // 2):
            c2 = slice(2 * quad * LANES, (2 * quad + 2) * LANES)
            q4 = q_ref[pl.ds(qrow, GRID_W), c2]
            pieces = []
            for half in range(2):
                for j in range(Q_BLOCKS):
                    qj = q4[j * WIN_W:(j + 1) * WIN_W, half * LANES:(half + 1) * LANES]
                    for piece in (jnp.where(lo, qj, zero), jnp.where(lo, zero, qj)):
                        pieces.append(jnp.concatenate(
                            [piece, zero] if half == 0 else [zero, piece], axis=1))
            s = lax.dot_general(jnp.concatenate(pieces, axis=0), window(kext, start, c2),
                                (((1,), (1,)), ((), ())), preferred_element_type=F32)
            for half in range(2):
                hp = 2 * quad + half
                for n, (j, cb) in enumerate(_SCORE_BLOCKS):
                    rs = slice(j * SLAB, (j + 1) * SLAB)
                    cs = slice(cb * LANES, (cb + 1) * LANES)
                    s_ref[hp, rs, cs] = (s[half * n_hp + j * SLAB:half * n_hp + (j + 1) * SLAB, cs]
                                         + bt_ref[dr0, hp, n])

    def attend(lr, s_ref):
        start, qrow, _ = row_geometry(lr)
        for quad in range(HEAD_PAIRS // 2):
            c2 = slice(2 * quad * LANES, (2 * quad + 2) * LANES)
            p_rows, inv_l = [], []
            for half in range(2):
                hp = 2 * quad + half
                for j in range(Q_BLOCKS):
                    rs = slice(j * SLAB, (j + 1) * SLAB)
                    need = _NEEDED_COL_BLOCKS[j]
                    sb = [s_ref[hp, rs, cb * LANES:(cb + 1) * LANES] for cb in need]
                    m = jnp.max(functools.reduce(jnp.maximum, sb), axis=-1, keepdims=True)
                    e = [jnp.exp(x - m) for x in sb]
                    inv_l.append(
                        1.0 / jnp.sum(functools.reduce(jnp.add, e), axis=-1, keepdims=True))
                    zero = jnp.zeros((SLAB, LANES), BF16)
                    blocks = [zero] * COL_BLOCKS
                    for cb, x in zip(need, e):
                        blocks[cb] = x.astype(BF16)
                    p_rows.append(jnp.concatenate(blocks, axis=1))
            o = jnp.dot(jnp.concatenate(p_rows, axis=0), window(vext, start, c2),
                        preferred_element_type=F32)
            for half in range(2):
                c = slice((2 * quad + half) * LANES, (2 * quad + half + 1) * LANES)
                for j in range(Q_BLOCKS):
                    r_lo = half * n_hp + j * SLAB
                    oj = o[r_lo:r_lo + SLAB, half * LANES:(half + 1) * LANES]
                    oj = oj * inv_l[half * Q_BLOCKS + j]
                    y_ref[pl.ds(pl.multiple_of(qrow + j * WIN_W, WIN_W), WIN_W), c] = jnp.where(
                        lo, oj[:WIN_W], oj[WIN_W:]).astype(BF16)

    def gated_conv():
        hq = tq // 2
        before = jnp.where(i > 0, pp_ref[...].astype(F32)[F32_ROWS:], 0.0)
        after = jnp.where(i < nt - 1, pn_ref[...].astype(F32)[:F32_ROWS], 0.0)
        for r0h, lead, trail in ((0, before, None), (hq, None, after)):
            pext[0:F32_ROWS] = (p_ref[r0h - BF16_ROWS:r0h].astype(F32)[F32_ROWS:]
                                if lead is None else lead)
            pext[F32_ROWS:F32_ROWS + hq] = p_ref[r0h:r0h + hq].astype(F32)
            pext[F32_ROWS + hq:2 * F32_ROWS + hq] = (
                p_ref[r0h + hq:r0h + hq + BF16_ROWS].astype(F32)[:F32_ROWS]
                if trail is None else trail)
            conv = (cw_ref[0:1] * pext[F32_ROWS - 1:F32_ROWS - 1 + hq]
                    + cw_ref[1:2] * pext[F32_ROWS:F32_ROWS + hq]
                    + cw_ref[2:3] * pext[F32_ROWS + 1:F32_ROWS + 1 + hq])
            y_ref[r0h:r0h + hq, NA_WIDTH:NA_WIDTH + CONV_WIDTH] = (
                b_ref[r0h:r0h + hq].astype(F32) * conv).astype(BF16)

    def mem_attention():
        for h in range(MEM_HEADS):
            c = slice(h * MEM_HEAD_DIM, (h + 1) * MEM_HEAD_DIM)
            s = lax.dot_general(qm_ref[:, c], mk_ref[:, c], (((1,), (1,)), ((), ())),
                                preferred_element_type=F32) * (1.0 / math.sqrt(MEM_HEAD_DIM))
            o0 = NA_WIDTH + CONV_WIDTH + h * MEM_HEAD_DIM
            y_ref[:, o0:o0 + MEM_HEAD_DIM] = _softmax_pv(s, mv_ref[:, c]).astype(BF16)

    s_refs = (s0_ref, s1_ref)
    scores(0, s_refs[0])
    for lr in range(tile_rows):
        if lr + 1 < tile_rows:
            scores(lr + 1, s_refs[(lr + 1) % 2])
        attend(lr, s_refs[lr % 2])
    gated_conv()
    mem_attention()


def _mixers(q, k, v, p, bb, qm, mk, mv, bias_tab, conv_w, tile_rows):
    b, t, _ = q.shape
    rows = t // GRID_W
    tq = tile_rows * GRID_W
    nt = t // tq
    halo = KEY_ROWS_HALO * GRID_W
    assert rows >= WIN_H and rows % tile_rows == 0 and tile_rows >= KEY_ROWS_HALO
    assert tq % halo == 0 and tq % BF16_ROWS == 0
    n_mem = mk.shape[1]

    def main(w):
        return pl.BlockSpec((None, tq, w), lambda bi, i: (bi, i, 0))

    def prev(blk, w):
        per = tq // blk
        return pl.BlockSpec((None, blk, w), lambda bi, i: (bi, jnp.maximum(i * per - 1, 0), 0))

    def nxt(blk, w):
        per = tq // blk
        last = t // blk - 1
        return pl.BlockSpec((None, blk, w), lambda bi, i: (bi, jnp.minimum((i + 1) * per, last), 0))

    kv_specs = [main(NA_WIDTH), prev(halo, NA_WIDTH), nxt(halo, NA_WIDTH)]
    mem_spec = pl.BlockSpec((None, n_mem, MEM_WIDTH), lambda bi, i: (bi, 0, 0),
                            pipeline_mode=pl.Buffered(1))
    d_mix = NA_WIDTH + CONV_WIDTH + MEM_WIDTH
    return pl.pallas_call(
        functools.partial(_mixers_kernel, rows=rows, tile_rows=tile_rows),
        grid=(b, nt),
        in_specs=[main(NA_WIDTH)] + kv_specs + kv_specs
                 + [main(CONV_WIDTH), prev(BF16_ROWS, CONV_WIDTH), nxt(BF16_ROWS, CONV_WIDTH),
                    main(CONV_WIDTH), main(MEM_WIDTH), mem_spec, mem_spec,
                    pl.BlockSpec(bias_tab.shape, lambda bi, i: (0,) * bias_tab.ndim,
                                 pipeline_mode=pl.Buffered(1)),
                    pl.BlockSpec(conv_w.shape, lambda bi, i: (0, 0))],
        out_specs=main(d_mix),
        out_shape=jax.ShapeDtypeStruct((b, t, d_mix), BF16),
        scratch_shapes=[pltpu.VMEM((tq + 2 * halo, NA_WIDTH), BF16),
                        pltpu.VMEM((tq + 2 * halo, NA_WIDTH), BF16),
                        pltpu.VMEM((tq // 2 + 2 * F32_ROWS, CONV_WIDTH), F32),
                        pltpu.VMEM((HEAD_PAIRS, 2 * GRID_W, WIN_H * GRID_W), F32),
                        pltpu.VMEM((HEAD_PAIRS, 2 * GRID_W, WIN_H * GRID_W), F32)],
        compiler_params=pltpu.CompilerParams(
            dimension_semantics=("parallel", "parallel"), vmem_limit_bytes=VMEM_LIMIT),
        name="mixers",
    )(q, k, k, k, v, v, v, p, p, p, bb, qm, mk, mv, bias_tab, conv_w)


def _out_proj_kernel(x_ref, y_ref, w_ref, o_ref):
    o_ref[...] = x_ref[...] + jnp.dot(y_ref[...], w_ref[...], preferred_element_type=F32)


def _out_proj(x, y, w_b, tm):
    b, t, d = x.shape
    row = lambda w: pl.BlockSpec((None, tm, w), lambda bi, i: (bi, i, 0))
    return pl.pallas_call(
        _out_proj_kernel,
        grid=(b, t // tm),
        in_specs=[row(d), row(y.shape[-1]), pl.BlockSpec(w_b.shape, lambda bi, i: (0, 0))],
        out_specs=row(d),
        out_shape=jax.ShapeDtypeStruct((b, t, d), F32),
        compiler_params=pltpu.CompilerParams(
            dimension_semantics=("parallel", "parallel"), vmem_limit_bytes=VMEM_LIMIT),
        name="out_proj",
    )(x, y, w_b)


_FFN_UP_CHUNK = 256
_FFN_NORM_ROWS = 256
_FFN_GATE_ROWS = 256


def _ffn_kernel(x_ref, xp_ref, xn_ref, g_ref, wa_ref, wg_ref, cw_ref, wo_ref,
                o_ref, next_ref, pa_ref, pg_ref, *, tm):
    i = pl.program_id(1)
    j = pl.program_id(2)
    nt = pl.num_programs(1)
    nj = pl.num_programs(2)
    h0 = BF16_ROWS

    ups = [(dst_ref, w_ref, slice(n0, n0 + _FFN_UP_CHUNK))
           for dst_ref, w_ref in ((pa_ref, wa_ref), (pg_ref, wg_ref))
           for n0 in range(0, w_ref.shape[-1], _FFN_UP_CHUNK)]

    def up(lo, hi, chunks):
        lhs = next_ref[lo:hi]
        for dst_ref, w_ref, c in chunks:
            dst_ref[lo:hi, c] = jnp.dot(lhs, w_ref[:, c], preferred_element_type=F32)

    def conv(ref, cw, r, n):
        return (cw[0:1] * ref[h0 - 1 + r:h0 - 1 + r + n] + cw[1:2] * ref[h0 + r:h0 + r + n]
                + cw[2:3] * ref[h0 + 1 + r:h0 + 1 + r + n])

    def gate_down():
        for r in range(0, tm, _FFN_GATE_ROWS):
            n = _FFN_GATE_ROWS
            hidden = (jax.nn.silu(conv(pa_ref, cw_ref[j], r, n))
                      * conv(pg_ref, cw_ref[nj + j], r, n)).astype(BF16)
            o_ref[r:r + n] += jnp.dot(hidden, wo_ref[...], preferred_element_type=F32)

    @pl.when(j == 0)
    def _():
        g = g_ref[...]
        zeros = jnp.zeros((F32_ROWS, x_ref.shape[-1]), F32)
        prev = jnp.where(i > 0, _rms(xp_ref[...], g), 0.0)
        nxt = jnp.where(i < nt - 1, _rms(xn_ref[...], g), 0.0)
        next_ref[0:h0] = jnp.concatenate([zeros, prev], axis=0).astype(BF16)
        next_ref[h0 + tm:2 * h0 + tm] = jnp.concatenate([nxt, zeros], axis=0).astype(BF16)
        for r in range(0, tm, _FFN_NORM_ROWS):
            xr = x_ref[r:r + _FFN_NORM_ROWS]
            next_ref[h0 + r:h0 + r + _FFN_NORM_ROWS] = _rms(xr, g).astype(BF16)
            o_ref[r:r + _FFN_NORM_ROWS] = xr

    up(0, 2 * h0 + tm, ups[:-1])
    head = h0 + _FFN_GATE_ROWS + BF16_ROWS
    up(0, head, ups[-1:])
    up(head, 2 * h0 + tm, ups[-1:])
    gate_down()


def _ffn(x1, g, w_in_b, conv_w, w_out_b, tm, tf):
    b, t, d = x1.shape
    d_ff = w_out_b.shape[0]
    nj = d_ff // tf
    assert d_ff % tf == 0 and t % tm == 0 and tm % F32_ROWS == 0
    per = tm // F32_ROWS
    last = t // F32_ROWS - 1
    taps = conv_w.shape[0]
    conv_chunks = jnp.swapaxes(conv_w.reshape(taps, 2 * nj, tf), 0, 1)
    return pl.pallas_call(
        functools.partial(_ffn_kernel, tm=tm),
        grid=(b, t // tm, nj),
        in_specs=[
            pl.BlockSpec((None, tm, d), lambda bi, i, j: (bi, i, 0)),
            pl.BlockSpec((None, F32_ROWS, d), lambda bi, i, j: (bi, jnp.maximum(i * per - 1, 0), 0)),
            pl.BlockSpec((None, F32_ROWS, d),
                         lambda bi, i, j: (bi, jnp.minimum((i + 1) * per, last), 0)),
            pl.BlockSpec((1, d), lambda bi, i, j: (0, 0)),
            pl.BlockSpec((d, tf), lambda bi, i, j: (0, j)),
            pl.BlockSpec((d, tf), lambda bi, i, j: (0, nj + j)),
            pl.BlockSpec((2 * nj, taps, tf), lambda bi, i, j: (0, 0, 0)),
            pl.BlockSpec((tf, d), lambda bi, i, j: (j, 0)),
        ],
        out_specs=pl.BlockSpec((None, tm, d), lambda bi, i, j: (bi, i, 0)),
        out_shape=jax.ShapeDtypeStruct((b, t, d), F32),
        scratch_shapes=[pltpu.VMEM((tm + 2 * BF16_ROWS, d), BF16),
                        pltpu.VMEM((tm + 2 * BF16_ROWS, tf), F32),
                        pltpu.VMEM((tm + 2 * BF16_ROWS, tf), F32)],
        compiler_params=pltpu.CompilerParams(
            dimension_semantics=("parallel", "parallel", "arbitrary"),
            vmem_limit_bytes=VMEM_LIMIT),
        name="ffn",
    )(x1, x1, x1, g, w_in_b, w_in_b, conv_chunks, w_out_b)


_TM_PROJ = 512
_TILE_ROWS = 8
_TM_FFN = 1024
_TF_FFN = 512


def _layer(x, mem, p):
    mk, mv = _mem_kv(mem, p["mem_norm_g"], p["w_mem_kv"], p["mem_k_gain"])
    q, k, v, ch, bb, qm = _in_proj(x, p["g_mix"], p["w_in"], p["na_q_gain"], p["na_k_gain"],
                                   p["mem_q_gain"], _TM_PROJ)
    y = _mixers(q, k, v, ch, bb, qm, mk, mv, p["bias_tab"], p["conv_w"], _TILE_ROWS)
    x1 = _out_proj(x, y, p["w_out"], _TM_PROJ)
    return _ffn(x1, p["g_ffn"], p["w_ffn_in"], p["ffn_conv_w"], p["w_ffn_out"], _TM_FFN, _TF_FFN)


def kernel(x_prompt, x_sample, mem_prompt, mem_sample, g_mix, w_in, na_q_gain, na_k_gain,
           na_rel_bias, conv_w, mem_norm_g, w_mem_kv, mem_q_gain, mem_k_gain, w_out,
           g_ffn, w_ffn_in, ffn_conv_w, w_ffn_out):
    y_prompt, y_sample = x_prompt, x_sample
    for l in range(g_mix.shape[0]):
        row = lambda a: a[l].reshape(1, -1).astype(F32)
        q_scale = 1.0 / math.sqrt(HEAD_DIM)
        p = dict(
            g_mix=row(g_mix), g_ffn=row(g_ffn), mem_norm_g=row(mem_norm_g),
            na_q_gain=jnp.tile(row(na_q_gain), (1, 2)) * q_scale,
            na_k_gain=jnp.tile(row(na_k_gain), (1, 2)),
            mem_q_gain=row(mem_q_gain), mem_k_gain=row(mem_k_gain),
            w_in=w_in[l].astype(BF16), w_mem_kv=w_mem_kv[l].astype(BF16),
            w_out=w_out[l].astype(BF16), w_ffn_in=w_ffn_in[l].astype(BF16),
            w_ffn_out=w_ffn_out[l].astype(BF16),
            conv_w=conv_w[l].astype(F32), ffn_conv_w=ffn_conv_w[l].astype(F32),
            bias_tab=_bias_table(na_rel_bias[l]),
        )
        y_prompt = _layer(y_prompt, mem_prompt, p)
        y_sample = _layer(y_sample, mem_sample, p)
    return (y_prompt, y_sample)
```

```python
import functools
import math

import jax
import jax.numpy as jnp
import numpy as np
from jax import lax
from jax.experimental import pallas as pl
from jax.experimental.pallas import tpu as pltpu

F32 = jnp.float32
BF16 = jnp.bfloat16

GRID_W = 64
WIN_H = 8
WIN_W = 16
HEAD_DIM = 64
NA_HEADS = 16
NA_WIDTH = NA_HEADS * HEAD_DIM
CONV_WIDTH = 512
MEM_HEADS = 4
MEM_HEAD_DIM = 128
MEM_WIDTH = MEM_HEADS * MEM_HEAD_DIM
EPS = 1e-6
NEG_INF = -1e30

LANES = 128
BF16_ROWS = 16
F32_ROWS = 8
VMEM_LIMIT = 58 * 1024 * 1024

HEAD_PAIRS = NA_HEADS // 2
KEY_ROWS_HALO = WIN_H // 2
N_DR = 2 * WIN_H - 1


def _rms(x, g):
    return x * lax.rsqrt(jnp.mean(x * x, axis=-1, keepdims=True) + EPS) * g


def _lo_lanes():
    return lax.broadcasted_iota(jnp.int32, (1, LANES), 1) < HEAD_DIM


def _pair_rms(z, g2):
    lo = _lo_lanes()
    sq = z * z
    s0 = jnp.sum(jnp.where(lo, sq, 0.0), axis=-1, keepdims=True)
    s1 = jnp.sum(jnp.where(lo, 0.0, sq), axis=-1, keepdims=True)
    ms = jnp.where(lo, s0, s1) * (1.0 / HEAD_DIM)
    return z * lax.rsqrt(ms + EPS) * g2


def _softmax_pv(s, v):
    m = jnp.max(s, axis=-1, keepdims=True)
    e = jnp.exp(s - m)
    l = jnp.sum(e, axis=-1, keepdims=True)
    o = jnp.dot(e.astype(BF16), v, preferred_element_type=F32)
    return o * (1.0 / l)


def _mem_kv_kernel(mem_ref, g_ref, w_ref, kg_ref, mk_ref, mv_ref):
    n = _rms(mem_ref[...], g_ref[...]).astype(BF16)
    z = jnp.dot(n, w_ref[...], preferred_element_type=F32)
    for h in range(MEM_HEADS):
        c = slice(h * MEM_HEAD_DIM, (h + 1) * MEM_HEAD_DIM)
        mk_ref[:, c] = _rms(z[:, c], kg_ref[...]).astype(BF16)
    mv_ref[...] = z[:, MEM_WIDTH:].astype(BF16)


def _mem_kv(mem, g, w_b, k_gain):
    b, m, d = mem.shape
    out = jax.ShapeDtypeStruct((b, m, MEM_WIDTH), BF16)
    return pl.pallas_call(
        _mem_kv_kernel,
        grid=(b,),
        in_specs=[
            pl.BlockSpec((None, m, d), lambda i: (i, 0, 0)),
            pl.BlockSpec((1, d), lambda i: (0, 0)),
            pl.BlockSpec((d, 2 * MEM_WIDTH), lambda i: (0, 0)),
            pl.BlockSpec((1, MEM_HEAD_DIM), lambda i: (0, 0)),
        ],
        out_specs=[pl.BlockSpec((None, m, MEM_WIDTH), lambda i: (i, 0, 0))] * 2,
        out_shape=[out, out],
        compiler_params=pltpu.CompilerParams(
            dimension_semantics=("parallel",), vmem_limit_bytes=VMEM_LIMIT),
        name="mem_kv",
    )(mem, g, w_b, k_gain)


_Q0, _K0, _V0 = 0, NA_WIDTH, 2 * NA_WIDTH
_H0 = 3 * NA_WIDTH
_B0 = _H0 + CONV_WIDTH
_C0 = _B0 + CONV_WIDTH
_QM0 = _C0 + CONV_WIDTH
_PROJ_CHUNK = 512


def _in_proj_kernel(x_ref, g_ref, w_ref, qg_ref, kg_ref, mqg_ref,
                    q_ref, k_ref, v_ref, p_ref, b_ref, qm_ref):
    n = _rms(x_ref[...], g_ref[...]).astype(BF16)

    def proj(c0):
        return jnp.dot(n, w_ref[:, c0:c0 + _PROJ_CHUNK], preferred_element_type=F32)

    for base, gain_ref, out_ref in ((_Q0, qg_ref, q_ref), (_K0, kg_ref, k_ref)):
        for cc in range(NA_WIDTH // _PROJ_CHUNK):
            z = proj(base + cc * _PROJ_CHUNK)
            for s in range(_PROJ_CHUNK // LANES):
                o = cc * _PROJ_CHUNK + s * LANES
                out_ref[:, o:o + LANES] = _pair_rms(
                    z[:, s * LANES:(s + 1) * LANES], gain_ref[...]).astype(BF16)
    for cc in range(NA_WIDTH // _PROJ_CHUNK):
        o = cc * _PROJ_CHUNK
        v_ref[:, o:o + _PROJ_CHUNK] = proj(_V0 + o).astype(BF16)
    p_ref[...] = (proj(_C0) * proj(_H0)).astype(BF16)
    b_ref[...] = proj(_B0).astype(BF16)
    z = proj(_QM0)
    for h in range(MEM_HEADS):
        c = slice(h * MEM_HEAD_DIM, (h + 1) * MEM_HEAD_DIM)
        qm_ref[:, c] = _rms(z[:, c], mqg_ref[...]).astype(BF16)


def _in_proj(x, g, w_b, q_gain2, k_gain2, mq_gain, tm):
    b, t, d = x.shape
    nt = t // tm
    row = lambda w: pl.BlockSpec((None, tm, w), lambda bi, i: (bi, i, 0))
    const = lambda shape: pl.BlockSpec(shape, lambda bi, i: (0,) * len(shape))
    sds = lambda w: jax.ShapeDtypeStruct((b, t, w), BF16)
    return pl.pallas_call(
        _in_proj_kernel,
        grid=(b, nt),
        in_specs=[row(d), const((1, d)), const(w_b.shape),
                  const((1, LANES)), const((1, LANES)), const((1, MEM_HEAD_DIM))],
        out_specs=[row(NA_WIDTH)] * 3 + [row(CONV_WIDTH)] * 2 + [row(MEM_WIDTH)],
        out_shape=[sds(NA_WIDTH)] * 3 + [sds(CONV_WIDTH)] * 2 + [sds(MEM_WIDTH)],
        compiler_params=pltpu.CompilerParams(
            dimension_semantics=("parallel", "parallel"), vmem_limit_bytes=VMEM_LIMIT),
        name="in_proj",
    )(x, g, w_b, q_gain2, k_gain2, mq_gain)


COL_BLOCKS = GRID_W // WIN_W
Q_BLOCKS = GRID_W // WIN_W
SLAB = 2 * WIN_W
_NEEDED_COL_BLOCKS = tuple(
    tuple(cb for cb in range(COL_BLOCKS)
          if any(max(0, min(q - WIN_W // 2, GRID_W - WIN_W)) < (cb + 1) * WIN_W
                 and max(0, min(q - WIN_W // 2, GRID_W - WIN_W)) + WIN_W > cb * WIN_W
                 for q in range(j * WIN_W, (j + 1) * WIN_W)))
    for j in range(Q_BLOCKS))


_SCORE_BLOCKS = tuple((j, cb) for j in range(Q_BLOCKS) for cb in _NEEDED_COL_BLOCKS[j])


def _bias_table(rel_bias):
    h = rel_bias.shape[0]
    n_dc = 2 * WIN_W - 1
    n_blk = len(_SCORE_BLOCKS)
    cr = np.arange(WIN_W)[:, None]
    cl = np.arange(WIN_W)[None, :]
    select = np.stack([(WIN_W * (x - 1) + cl - cr + WIN_W - 1)[..., None] == np.arange(n_dc)
                       for x in range(3)]).astype(np.float32)
    mask = np.zeros((n_blk, 2, WIN_W, WIN_H, WIN_W), np.float32)
    for n, (j, cb) in enumerate(_SCORE_BLOCKS):
        q, k = j * WIN_W + cr, cb * WIN_W + cl
        q_cs = np.clip(q - WIN_W // 2, 0, GRID_W - WIN_W)
        mask[n] = np.where((k >= q_cs) & (k < q_cs + WIN_W), 0.0, NEG_INF)[None, :, None, :]
    rb = rel_bias.astype(F32).reshape(h // 2, 2, N_DR, n_dc)
    pieces = jnp.einsum("pedt,xrlt->pedxrl", rb, select, precision=lax.Precision.HIGHEST)
    pieces = jnp.pad(pieces, ((0, 0),) * 5 + ((0, LANES - WIN_W),))
    return pl.pallas_call(
        _bias_expand_kernel,
        grid=(h // 2,),
        in_specs=[pl.BlockSpec((None,) + pieces.shape[1:], lambda p: (p, 0, 0, 0, 0, 0)),
                  pl.BlockSpec((n_blk, SLAB, LANES), lambda p: (0, 0, 0))],
        out_specs=pl.BlockSpec((WIN_H, None, n_blk, SLAB, LANES), lambda p: (0, p, 0, 0, 0)),
        out_shape=jax.ShapeDtypeStruct((WIN_H, h // 2, n_blk, SLAB, LANES), F32),
        name="bias_expand",
    )(pieces, mask.reshape(n_blk, SLAB, LANES))


def _bias_expand_kernel(piece_ref, mask_ref, o_ref):
    def window_rows(x, d0):
        halves = []
        for hd in range(2):
            acc = piece_ref[hd, d0, x]
            for wr in range(1, WIN_H):
                acc = acc + pltpu.roll(piece_ref[hd, d0 + wr, x], wr * WIN_W, axis=1)
            halves.append(acc)
        return jnp.concatenate(halves, axis=0)

    for d0 in range(WIN_H):
        unmasked = [window_rows(x, d0) for x in range(piece_ref.shape[2])]
        for n, (j, cb) in enumerate(_SCORE_BLOCKS):
            o_ref[d0, n] = unmasked[cb - j + 1] + mask_ref[n]


def _mixers_kernel(q_ref, k_ref, kp_ref, kn_ref, v_ref, vp_ref, vn_ref,
                   p_ref, pp_ref, pn_ref, b_ref, qm_ref, mk_ref, mv_ref, bt_ref, cw_ref,
                   y_ref, kext, vext, pext, s0_ref, s1_ref, *, rows, tile_rows):
    i = pl.program_id(1)
    nt = pl.num_programs(1)
    tq = tile_rows * GRID_W
    halo = KEY_ROWS_HALO * GRID_W
    lo = _lo_lanes()
    n_hp = 2 * GRID_W

    for ext, prev, main, nxt in ((kext, kp_ref, k_ref, kn_ref), (vext, vp_ref, v_ref, vn_ref)):
        ext[0:halo] = prev[...]
        ext[halo:halo + tq] = main[...]
        ext[halo + tq:halo + tq + halo] = nxt[...]

    r0 = i * tile_rows

    def window(ext, start, c):
        return jnp.concatenate(
            [ext[pl.ds(pl.multiple_of(start + wr * GRID_W + cb * WIN_W, WIN_W), WIN_W), c]
             for cb in range(COL_BLOCKS) for wr in range(WIN_H)], axis=0)

    def row_geometry(lr):
        r = r0 + lr
        ws = jnp.clip(r - WIN_H // 2, 0, rows - WIN_H)
        start = pl.multiple_of((ws - r0 + KEY_ROWS_HALO) * GRID_W, GRID_W)
        dr0 = ws - r + (WIN_H - 1)
        return start, pl.multiple_of(lr * GRID_W, GRID_W), dr0

    def scores(lr, s_ref):
        start, qrow, dr0 = row_geometry(lr)
        zero = jnp.zeros((WIN_W, LANES), BF16)
        for quad in range(HEAD_PAIRS // 2):
            c2 = slice(2 * quad * LANES, (2 * quad + 2) * LANES)
            q4 = q_ref[pl.ds(qrow, GRID_W), c2]
            pieces = []
            for half in range(2):
                for j in range(Q_BLOCKS):
                    qj = q4[j * WIN_W:(j + 1) * WIN_W, half * LANES:(half + 1) * LANES]
                    for piece in (jnp.where(lo, qj, zero), jnp.where(lo, zero, qj)):
                        pieces.append(jnp.concatenate(
                            [piece, zero] if half == 0 else [zero, piece], axis=1))
            s = lax.dot_general(jnp.concatenate(pieces, axis=0), window(kext, start, c2),
                                (((1,), (1,)), ((), ())), preferred_element_type=F32)
            for half in range(2):
                hp = 2 * quad + half
                for n, (j, cb) in enumerate(_SCORE_BLOCKS):
                    rs = slice(j * SLAB, (j + 1) * SLAB)
                    cs = slice(cb * LANES, (cb + 1) * LANES)
                    s_ref[hp, rs, cs] = (s[half * n_hp + j * SLAB:half * n_hp + (j + 1) * SLAB, cs]
                                         + bt_ref[dr0, hp, n])

    def attend(lr, s_ref):
        start, qrow, _ = row_geometry(lr)
        for quad in range(HEAD_PAIRS // 2):
            c2 = slice(2 * quad * LANES, (2 * quad + 2) * LANES)
            p_rows, inv_l = [], []
            for half in range(2):
                hp = 2 * quad + half
                for j in range(Q_BLOCKS):
                    rs = slice(j * SLAB, (j + 1) * SLAB)
                    need = _NEEDED_COL_BLOCKS[j]
                    sb = [s_ref[hp, rs, cb * LANES:(cb + 1) * LANES] for cb in need]
                    m = jnp.max(functools.reduce(jnp.maximum, sb), axis=-1, keepdims=True)
                    e = [jnp.exp(x - m) for x in sb]
                    inv_l.append(
                        1.0 / jnp.sum(functools.reduce(jnp.add, e), axis=-1, keepdims=True))
                    zero = jnp.zeros((SLAB, LANES), BF16)
                    blocks = [zero] * COL_BLOCKS
                    for cb, x in zip(need, e):
                        blocks[cb] = x.astype(BF16)
                    p_rows.append(jnp.concatenate(blocks, axis=1))
            o = jnp.dot(jnp.concatenate(p_rows, axis=0), window(vext, start, c2),
                        preferred_element_type=F32)
            for half in range(2):
                c = slice((2 * quad + half) * LANES, (2 * quad + half + 1) * LANES)
                for j in range(Q_BLOCKS):
                    r_lo = half * n_hp + j * SLAB
                    oj = o[r_lo:r_lo + SLAB, half * LANES:(half + 1) * LANES]
                    oj = oj * inv_l[half * Q_BLOCKS + j]
                    y_ref[pl.ds(pl.multiple_of(qrow + j * WIN_W, WIN_W), WIN_W), c] = jnp.where(
                        lo, oj[:WIN_W], oj[WIN_W:]).astype(BF16)

    def gated_conv():
        hq = tq // 2
        before = jnp.where(i > 0, pp_ref[...].astype(F32)[F32_ROWS:], 0.0)
        after = jnp.where(i < nt - 1, pn_ref[...].astype(F32)[:F32_ROWS], 0.0)
        for r0h, lead, trail in ((0, before, None), (hq, None, after)):
            pext[0:F32_ROWS] = (p_ref[r0h - BF16_ROWS:r0h].astype(F32)[F32_ROWS:]
                                if lead is None else lead)
            pext[F32_ROWS:F32_ROWS + hq] = p_ref[r0h:r0h + hq].astype(F32)
            pext[F32_ROWS + hq:2 * F32_ROWS + hq] = (
                p_ref[r0h + hq:r0h + hq + BF16_ROWS].astype(F32)[:F32_ROWS]
                if trail is None else trail)
            conv = (cw_ref[0:1] * pext[F32_ROWS - 1:F32_ROWS - 1 + hq]
                    + cw_ref[1:2] * pext[F32_ROWS:F32_ROWS + hq]
                    + cw_ref[2:3] * pext[F32_ROWS + 1:F32_ROWS + 1 + hq])
            y_ref[r0h:r0h + hq, NA_WIDTH:NA_WIDTH + CONV_WIDTH] = (
                b_ref[r0h:r0h + hq].astype(F32) * conv).astype(BF16)

    def mem_attention():
        for h in range(MEM_HEADS):
            c = slice(h * MEM_HEAD_DIM, (h + 1) * MEM_HEAD_DIM)
            s = lax.dot_general(qm_ref[:, c], mk_ref[:, c], (((1,), (1,)), ((), ())),
                                preferred_element_type=F32) * (1.0 / math.sqrt(MEM_HEAD_DIM))
            o0 = NA_WIDTH + CONV_WIDTH + h * MEM_HEAD_DIM
            y_ref[:, o0:o0 + MEM_HEAD_DIM] = _softmax_pv(s, mv_ref[:, c]).astype(BF16)

    s_refs = (s0_ref, s1_ref)
    scores(0, s_refs[0])
    for lr in range(tile_rows):
        if lr + 1 < tile_rows:
            scores(lr + 1, s_refs[(lr + 1) % 2])
        attend(lr, s_refs[lr % 2])
    gated_conv()
    mem_attention()


def _mixers(q, k, v, p, bb, qm, mk, mv, bias_tab, conv_w, tile_rows):
    b, t, _ = q.shape
    rows = t // GRID_W
    tq = tile_rows * GRID_W
    nt = t // tq
    halo = KEY_ROWS_HALO * GRID_W
    assert rows >= WIN_H and rows % tile_rows == 0 and tile_rows >= KEY_ROWS_HALO
    assert tq % halo == 0 and tq % BF16_ROWS == 0
    n_mem = mk.shape[1]

    def main(w):
        return pl.BlockSpec((None, tq, w), lambda bi, i: (bi, i, 0))

    def prev(blk, w):
        per = tq // blk
        return pl.BlockSpec((None, blk, w), lambda bi, i: (bi, jnp.maximum(i * per - 1, 0), 0))

    def nxt(blk, w):
        per = tq // blk
        last = t // blk - 1
        return pl.BlockSpec((None, blk, w), lambda bi, i: (bi, jnp.minimum((i + 1) * per, last), 0))

    kv_specs = [main(NA_WIDTH), prev(halo, NA_WIDTH), nxt(halo, NA_WIDTH)]
    mem_spec = pl.BlockSpec((None, n_mem, MEM_WIDTH), lambda bi, i: (bi, 0, 0),
                            pipeline_mode=pl.Buffered(1))
    d_mix = NA_WIDTH + CONV_WIDTH + MEM_WIDTH
    return pl.pallas_call(
        functools.partial(_mixers_kernel, rows=rows, tile_rows=tile_rows),
        grid=(b, nt),
        in_specs=[main(NA_WIDTH)] + kv_specs + kv_specs
                 + [main(CONV_WIDTH), prev(BF16_ROWS, CONV_WIDTH), nxt(BF16_ROWS, CONV_WIDTH),
                    main(CONV_WIDTH), main(MEM_WIDTH), mem_spec, mem_spec,
                    pl.BlockSpec(bias_tab.shape, lambda bi, i: (0,) * bias_tab.ndim,
                                 pipeline_mode=pl.Buffered(1)),
                    pl.BlockSpec(conv_w.shape, lambda bi, i: (0, 0))],
        out_specs=main(d_mix),
        out_shape=jax.ShapeDtypeStruct((b, t, d_mix), BF16),
        scratch_shapes=[pltpu.VMEM((tq + 2 * halo, NA_WIDTH), BF16),
                        pltpu.VMEM((tq + 2 * halo, NA_WIDTH), BF16),
                        pltpu.VMEM((tq // 2 + 2 * F32_ROWS, CONV_WIDTH), F32),
                        pltpu.VMEM((HEAD_PAIRS, 2 * GRID_W, WIN_H * GRID_W), F32),
                        pltpu.VMEM((HEAD_PAIRS, 2 * GRID_W, WIN_H * GRID_W), F32)],
        compiler_params=pltpu.CompilerParams(
            dimension_semantics=("parallel", "parallel"), vmem_limit_bytes=VMEM_LIMIT),
        name="mixers",
    )(q, k, k, k, v, v, v, p, p, p, bb, qm, mk, mv, bias_tab, conv_w)


def _out_proj_kernel(x_ref, y_ref, w_ref, o_ref):
    o_ref[...] = x_ref[...] + jnp.dot(y_ref[...], w_ref[...], preferred_element_type=F32)


def _out_proj(x, y, w_b, tm):
    b, t, d = x.shape
    row = lambda w: pl.BlockSpec((None, tm, w), lambda bi, i: (bi, i, 0))
    return pl.pallas_call(
        _out_proj_kernel,
        grid=(b, t // tm),
        in_specs=[row(d), row(y.shape[-1]), pl.BlockSpec(w_b.shape, lambda bi, i: (0, 0))],
        out_specs=row(d),
        out_shape=jax.ShapeDtypeStruct((b, t, d), F32),
        compiler_params=pltpu.CompilerParams(
            dimension_semantics=("parallel", "parallel"), vmem_limit_bytes=VMEM_LIMIT),
        name="out_proj",
    )(x, y, w_b)


_FFN_UP_CHUNK = 512
_FFN_NORM_ROWS = 256
_FFN_GATE_ROWS = 512


def _ffn_kernel(x_ref, xp_ref, xn_ref, g_ref, wa_ref, wg_ref, cw_ref, wo_ref,
                o_ref, next_ref, pa_ref, pg_ref, *, tm):
    i = pl.program_id(1)
    j = pl.program_id(2)
    nt = pl.num_programs(1)
    nj = pl.num_programs(2)
    h0 = BF16_ROWS

    ups = [(dst_ref, w_ref, slice(n0, n0 + _FFN_UP_CHUNK))
           for dst_ref, w_ref in ((pa_ref, wa_ref), (pg_ref, wg_ref))
           for n0 in range(0, w_ref.shape[-1], _FFN_UP_CHUNK)]

    def up(lo, hi, chunks):
        lhs = next_ref[lo:hi]
        for dst_ref, w_ref, c in chunks:
            dst_ref[lo:hi, c] = jnp.dot(lhs, w_ref[:, c], preferred_element_type=F32)

    def conv(ref, cw, r, n):
        return (cw[0:1] * ref[h0 - 1 + r:h0 - 1 + r + n] + cw[1:2] * ref[h0 + r:h0 + r + n]
                + cw[2:3] * ref[h0 + 1 + r:h0 + 1 + r + n])

    def gate_down():
        for r in range(0, tm, _FFN_GATE_ROWS):
            n = _FFN_GATE_ROWS
            hidden = (jax.nn.silu(conv(pa_ref, cw_ref[j], r, n))
                      * conv(pg_ref, cw_ref[nj + j], r, n)).astype(BF16)
            o_ref[r:r + n] += jnp.dot(hidden, wo_ref[...], preferred_element_type=F32)

    @pl.when(j == 0)
    def _():
        g = g_ref[...]
        zeros = jnp.zeros((F32_ROWS, x_ref.shape[-1]), F32)
        prev = jnp.where(i > 0, _rms(xp_ref[...], g), 0.0)
        nxt = jnp.where(i < nt - 1, _rms(xn_ref[...], g), 0.0)
        next_ref[0:h0] = jnp.concatenate([zeros, prev], axis=0).astype(BF16)
        next_ref[h0 + tm:2 * h0 + tm] = jnp.concatenate([nxt, zeros], axis=0).astype(BF16)
        for r in range(0, tm, _FFN_NORM_ROWS):
            xr = x_ref[r:r + _FFN_NORM_ROWS]
            next_ref[h0 + r:h0 + r + _FFN_NORM_ROWS] = _rms(xr, g).astype(BF16)
            o_ref[r:r + _FFN_NORM_ROWS] = xr

    up(0, 2 * h0 + tm, ups[:-1])
    head = h0 + _FFN_GATE_ROWS + BF16_ROWS
    up(0, head, ups[-1:])
    up(head, 2 * h0 + tm, ups[-1:])
    gate_down()


def _ffn(x1, g, w_in_b, conv_w, w_out_b, tm, tf):
    b, t, d = x1.shape
    d_ff = w_out_b.shape[0]
    nj = d_ff // tf
    assert d_ff % tf == 0 and t % tm == 0 and tf % _FFN_UP_CHUNK == 0
    assert tm % _FFN_GATE_ROWS == 0 and tm % _FFN_NORM_ROWS == 0 and _FFN_GATE_ROWS % BF16_ROWS == 0
    per = tm // F32_ROWS
    last = t // F32_ROWS - 1
    taps = conv_w.shape[0]
    conv_chunks = jnp.swapaxes(conv_w.reshape(taps, 2 * nj, tf), 0, 1)
    return pl.pallas_call(
        functools.partial(_ffn_kernel, tm=tm),
        grid=(b, t // tm, nj),
        in_specs=[
            pl.BlockSpec((None, tm, d), lambda bi, i, j: (bi, i, 0)),
            pl.BlockSpec((None, F32_ROWS, d), lambda bi, i, j: (bi, jnp.maximum(i * per - 1, 0), 0)),
            pl.BlockSpec((None, F32_ROWS, d),
                         lambda bi, i, j: (bi, jnp.minimum((i + 1) * per, last), 0)),
            pl.BlockSpec((1, d), lambda bi, i, j: (0, 0)),
            pl.BlockSpec((d, tf), lambda bi, i, j: (0, j)),
            pl.BlockSpec((d, tf), lambda bi, i, j: (0, nj + j)),
            pl.BlockSpec((2 * nj, taps, tf), lambda bi, i, j: (0, 0, 0)),
            pl.BlockSpec((tf, d), lambda bi, i, j: (j, 0)),
        ],
        out_specs=pl.BlockSpec((None, tm, d), lambda bi, i, j: (bi, i, 0)),
        out_shape=jax.ShapeDtypeStruct((b, t, d), F32),
        scratch_shapes=[pltpu.VMEM((tm + 2 * BF16_ROWS, d), BF16),
                        pltpu.VMEM((tm + 2 * BF16_ROWS, tf), F32),
                        pltpu.VMEM((tm + 2 * BF16_ROWS, tf), F32)],
        compiler_params=pltpu.CompilerParams(
            dimension_semantics=("parallel", "parallel", "arbitrary"),
            vmem_limit_bytes=VMEM_LIMIT),
        name="ffn",
    )(x1, x1, x1, g, w_in_b, w_in_b, conv_chunks, w_out_b)


_TM_PROJ = 512
_TILE_ROWS = 8
_TM_FFN = 1024
_TF_FFN = 512


def _layer(x, mem, p):
    mk, mv = _mem_kv(mem, p["mem_norm_g"], p["w_mem_kv"], p["mem_k_gain"])
    q, k, v, ch, bb, qm = _in_proj(x, p["g_mix"], p["w_in"], p["na_q_gain"], p["na_k_gain"],
                                   p["mem_q_gain"], _TM_PROJ)
    y = _mixers(q, k, v, ch, bb, qm, mk, mv, p["bias_tab"], p["conv_w"], _TILE_ROWS)
    x1 = _out_proj(x, y, p["w_out"], _TM_PROJ)
    return _ffn(x1, p["g_ffn"], p["w_ffn_in"], p["ffn_conv_w"], p["w_ffn_out"], _TM_FFN, _TF_FFN)


def kernel(x_prompt, x_sample, mem_prompt, mem_sample, g_mix, w_in, na_q_gain, na_k_gain,
           na_rel_bias, conv_w, mem_norm_g, w_mem_kv, mem_q_gain, mem_k_gain, w_out,
           g_ffn, w_ffn_in, ffn_conv_w, w_ffn_out):
    y_prompt, y_sample = x_prompt, x_sample
    for l in range(g_mix.shape[0]):
        row = lambda a: a[l].reshape(1, -1).astype(F32)
        q_scale = 1.0 / math.sqrt(HEAD_DIM)
        p = dict(
            g_mix=row(g_mix), g_ffn=row(g_ffn), mem_norm_g=row(mem_norm_g),
            na_q_gain=jnp.tile(row(na_q_gain), (1, 2)) * q_scale,
            na_k_gain=jnp.tile(row(na_k_gain), (1, 2)),
            mem_q_gain=row(mem_q_gain), mem_k_gain=row(mem_k_gain),
            w_in=w_in[l].astype(BF16), w_mem_kv=w_mem_kv[l].astype(BF16),
            w_out=w_out[l].astype(BF16), w_ffn_in=w_ffn_in[l].astype(BF16),
            w_ffn_out=w_ffn_out[l].astype(BF16),
            conv_w=conv_w[l].astype(F32), ffn_conv_w=ffn_conv_w[l].astype(F32),
            bias_tab=_bias_table(na_rel_bias[l]),
        )
        y_prompt = _layer(y_prompt, mem_prompt, p)
        y_sample = _layer(y_sample, mem_sample, p)
    return (y_prompt, y_sample)
```
